```python
import math
import jax
import jax.numpy as jnp
from jax import lax
import numpy as np

D_MODEL = 2048
BATCH = 2
SEQ = 4096
DEPTH = 2
DEC_BATCH = 8
DEC_SEQ = 4
PAST_LEN = 16384
PAGE_SIZE = 128

HEAD_DIM = 128
ROPE_THETA = 10000.0
RMS_EPS = 1e-6
N_HYB = (DEPTH + 1) // 2
N_SSM = DEPTH // 2

GDN_HEADS = D_MODEL // 256
GDN_DK = 128
GDN_DV = 128
GDN_CONV = 4
GDN_CHUNK = 64
GDN_QK = GDN_HEADS * GDN_DK
GDN_VW = GDN_HEADS * GDN_DV
GDN_CONV_DIM = 2 * GDN_QK + GDN_VW

NSA_HEADS = D_MODEL // 256
NSA_KV_HEADS = 2
NSA_GROUP = NSA_HEADS // NSA_KV_HEADS
NSA_QW = NSA_HEADS * HEAD_DIM
NSA_KVW = NSA_KV_HEADS * HEAD_DIM
CMP_BLOCK = 32
CMP_STRIDE = 16
CMP_HIDDEN = 256
SEL_BLOCK = 64
SEL_TOPK = 16
WINDOW = 512
Q_BLOCK = 128
FORCE_BONUS = 1e4

HYB_SPLITS = (GDN_QK, GDN_QK, GDN_VW, GDN_VW, GDN_HEADS, GDN_HEADS,
              NSA_QW, NSA_KVW, NSA_KVW, NSA_KVW, NSA_KVW, NSA_KVW, NSA_KVW, 3 * NSA_HEADS)
HYB_IN = 2 * GDN_QK + 2 * GDN_VW + 2 * GDN_HEADS + NSA_QW + 6 * NSA_KVW + 3 * NSA_HEADS
HYB_MIX = GDN_VW + NSA_QW

SSM_EXPAND = 2
SSM_D_INNER = SSM_EXPAND * D_MODEL
SSM_HEAD_DIM = 64
SSM_HEADS = SSM_D_INNER // SSM_HEAD_DIM
SSM_GROUPS = 8
SSM_HPG = SSM_HEADS // SSM_GROUPS
SSM_STATE = 128
SSM_CONV = 4
SSM_CHUNK = 128
SSM_BC = SSM_GROUPS * SSM_STATE
SSM_CONV_DIM = SSM_D_INNER + 2 * SSM_BC
SSM_IN = SSM_D_INNER + SSM_CONV_DIM + SSM_HEADS

D_FF = 11 * D_MODEL // 4
N_EXPERTS = 8
TOP_K = 2
D_FF_EXPERT = 7 * D_MODEL // 2

kernel_name = 'hybrid_gdn_nsa_mamba2_moe_step'


def rmsnorm(x, w):
    xf = x.astype(jnp.float32)
    y = xf * lax.rsqrt(jnp.mean(xf * xf, axis=-1, keepdims=True) + RMS_EPS)
    return (y * w.astype(jnp.float32)).astype(x.dtype)


def l2norm(x):
    xf = x.astype(jnp.float32)
    return (xf * lax.rsqrt(jnp.sum(xf * xf, axis=-1, keepdims=True) + 1e-6)).astype(x.dtype)


def rope(x, pos):
    half = x.shape[-1] // 2
    inv_freq = 1.0 / (ROPE_THETA ** (jnp.arange(half, dtype=jnp.float32) / half))
    ang = pos.astype(jnp.float32)[:, None] * inv_freq[None, :]
    cos = jnp.cos(ang)[None, :, None, :]
    sin = jnp.sin(ang)[None, :, None, :]
    xf = x.astype(jnp.float32)
    x1, x2 = xf[..., :half], xf[..., half:]
    return jnp.concatenate([x1 * cos - x2 * sin, x2 * cos + x1 * sin], axis=-1).astype(x.dtype)


def split_cols(a, sizes):
    out, s = [], 0
    for n in sizes:
        out.append(a[..., s:s + n])
        s += n
    return out


def causal_dwconv(x_ext, w):
    c = x_ext.shape[-1]
    return lax.conv_general_dilated(x_ext, w[:, None, :], window_strides=(1,), padding='VALID',
                                    dimension_numbers=('NWC', 'WIO', 'NWC'), feature_group_count=c)


def masked_softmax(s, mask):
    s = jnp.where(mask, s, -jnp.inf)
    m = jnp.max(s, axis=-1, keepdims=True)
    m = jnp.where(jnp.isfinite(m), m, 0.0)
    e = jnp.where(mask, jnp.exp(s - m), 0.0)
    den = jnp.sum(e, axis=-1, keepdims=True)
    return e / jnp.where(den > 0, den, 1.0)


def to_chunks(a, n, c, heads_first):
    b, l = a.shape[:2]
    a = jnp.pad(a.astype(jnp.float32), [(0, 0), (0, n * c - l)] + [(0, 0)] * (a.ndim - 2))
    a = a.reshape((b, n, c) + a.shape[2:])
    if heads_first:
        perm = (1, 0, 3, 2) + tuple(range(4, a.ndim))
    else:
        perm = (1, 0) + tuple(range(2, a.ndim))
    return a.transpose(perm)


def gdn_chunked(q, k, v, g, beta, s0):
    b, l = q.shape[:2]
    dv = v.shape[-1]
    c = min(GDN_CHUNK, l)
    n = -(-l // c)
    qc, kc, vc = (to_chunks(a, n, c, True) for a in (q, k, v))
    gc, bc = (to_chunks(a, n, c, True) for a in (g, beta))
    incl = jnp.tri(c, dtype=bool)
    strict = jnp.tri(c, k=-1, dtype=jnp.float32)
    eye = jnp.eye(c, dtype=jnp.float32)

    def step(s, inp):
        qi, ki, vi, gi, bi = inp
        gam = jnp.cumsum(gi, axis=-1)
        dmask = jnp.exp(jnp.where(incl, gam[..., :, None] - gam[..., None, :], -jnp.inf))
        kb = ki * bi[..., None]
        a_mat = eye + jnp.einsum('bhid,bhjd->bhij', kb, ki) * dmask * strict
        rhs = jnp.concatenate([vi * bi[..., None], kb * jnp.exp(gam)[..., None]], axis=-1)
        sol = lax.linalg.triangular_solve(a_mat, rhs, left_side=True, lower=True, unit_diagonal=True)
        u, w = sol[..., :dv], sol[..., dv:]
        v_new = u - jnp.einsum('bhcd,bhde->bhce', w, s)
        attn = jnp.einsum('bhid,bhjd->bhij', qi, ki) * dmask
        o = (jnp.einsum('bhcd,bhde->bhce', qi * jnp.exp(gam)[..., None], s)
             + jnp.einsum('bhij,bhje->bhie', attn, v_new))
        g_end = gam[..., -1:]
        s = (s * jnp.exp(g_end)[..., None]
             + jnp.einsum('bhcd,bhce->bhde', ki * jnp.exp(g_end - gam)[..., None], v_new))
        return s, o

    s_fin, o = lax.scan(step, s0.astype(jnp.float32), (qc, kc, vc, gc, bc))
    o = o.transpose(1, 0, 3, 2, 4).reshape(b, n * c, q.shape[2], dv)[:, :l]
    return o.astype(v.dtype), s_fin.astype(s0.dtype)


def ssd_chunked(x, dt, a_neg, bmat, cmat, h0):
    b, l = x.shape[:2]
    c = min(SSM_CHUNK, l)
    n = -(-l // c)
    xc, dtc, bc, cc = (to_chunks(a, n, c, False) for a in (x, dt, bmat, cmat))
    incl = jnp.tri(c, dtype=bool)
    a_neg = a_neg.astype(jnp.float32)

    def step(h, inp):
        xi, dti, bi, ci = inp
        bn = dti.shape[0]
        gam = jnp.cumsum(dti * a_neg, axis=1)
        diff = gam[:, :, None, :] - gam[:, None, :, :]
        lmat = jnp.exp(jnp.where(incl[None, :, :, None], diff, -jnp.inf)).reshape(bn, c, c, SSM_GROUPS, SSM_HPG)
        xdt = (xi * dti[..., None]).reshape(bn, c, SSM_GROUPS, SSM_HPG, SSM_HEAD_DIM)
        cb = jnp.einsum('bign,bjgn->bijg', ci, bi)
        y_in = jnp.einsum('bijg,bijgh,bjghp->bighp', cb, lmat, xdt)
        h5 = h.reshape(bn, SSM_GROUPS, SSM_HPG, SSM_HEAD_DIM, SSM_STATE)
        dec_in = jnp.exp(gam).reshape(bn, c, SSM_GROUPS, SSM_HPG)
        y_st = jnp.einsum('bign,bghpn->bighp', ci, h5) * dec_in[..., None]
        g_end = gam[:, -1]
        dec_out = jnp.exp(g_end[:, None, :] - gam).reshape(bn, c, SSM_GROUPS, SSM_HPG)
        h_new = (h * jnp.exp(g_end)[:, :, None, None]
                 + jnp.einsum('bjgn,bjghp->bghpn', bi, xdt * dec_out[..., None]).reshape(h.shape))
        return h_new, (y_in + y_st).reshape(bn, c, SSM_HEADS, SSM_HEAD_DIM)

    h_fin, y = lax.scan(step, h0.astype(jnp.float32), (xc, dtc, bc, cc))
    y = y.transpose(1, 0, 2, 3, 4).reshape(b, n * c, SSM_HEADS, SSM_HEAD_DIM)[:, :l]
    return y, h_fin.astype(h0.dtype)


def compress_blocks(rows, pe, w1, w2):
    b, lk = rows.shape[:2]
    n_cmp = (lk - CMP_BLOCK) // CMP_STRIDE + 1
    per = CMP_BLOCK // CMP_STRIDE
    segs = rows[:, :(n_cmp + per - 1) * CMP_STRIDE].reshape(
        b, n_cmp + per - 1, CMP_STRIDE, 2, NSA_KV_HEADS, HEAD_DIM)
    blocks = jnp.concatenate([segs[:, i:i + n_cmp] for i in range(per)], axis=2)
    blocks = blocks + jnp.transpose(pe, (1, 0, 2))[:, :, None, :]
    flat = jnp.transpose(blocks, (0, 1, 3, 4, 2, 5)).reshape(b, n_cmp, 2, NSA_KV_HEADS, CMP_BLOCK * HEAD_DIM)
    hid = jax.nn.silu(jnp.einsum('bcskf,sfh->bcskh', flat, w1))
    out = jnp.einsum('bcskh,she->bcske', hid, w2)
    c_end = jnp.arange(n_cmp) * CMP_STRIDE + CMP_BLOCK - 1
    return out[:, :, 0], out[:, :, 1], c_end


def selection_blocks(rows):
    b, lk = rows.shape[:2]
    n_sel = -(-lk // SEL_BLOCK)
    rows = jnp.pad(rows, ((0, 0), (0, n_sel * SEL_BLOCK - lk), (0, 0), (0, 0), (0, 0)))
    blk = rows.reshape(b, n_sel, SEL_BLOCK, 2, NSA_KV_HEADS, HEAD_DIM).transpose(3, 0, 4, 1, 2, 5)
    return blk[0], blk[1]


def cmp_to_sel_map(n_cmp, n_sel):
    cs = jnp.arange(n_cmp)[:, None] * CMP_STRIDE
    ss = jnp.arange(n_sel)[None, :] * SEL_BLOCK
    return ((cs < ss + SEL_BLOCK) & (cs + CMP_BLOCK > ss)).astype(jnp.float32)


def nsa_core(q, q_pos, kc, vc, c_end, ks_blk, vs_blk, kw, vw, kw_pos, gates):
    f32 = jnp.float32
    b, lq = q.shape[:2]
    qg = q.astype(f32).reshape(b, lq, NSA_KV_HEADS, NSA_GROUP, HEAD_DIM) * HEAD_DIM ** -0.5
    s_c = jnp.einsum('bqkgd,bckd->bqkgc', qg, kc.astype(f32))
    m_c = c_end[None, :] <= q_pos[:, None]
    p_c = masked_softmax(s_c, m_c[None, :, None, None, :])
    o_c = jnp.einsum('bqkgc,bckd->bqkgd', p_c, vc.astype(f32))
    n_cmp, n_sel = kc.shape[1], ks_blk.shape[2]
    imp = jnp.einsum('bqkgc,cj->bqkj', p_c, cmp_to_sel_map(n_cmp, n_sel))
    blk = jnp.arange(n_sel)[None, :]
    cur = (q_pos // SEL_BLOCK)[:, None]
    forced = (blk == 0) | (blk == cur) | (blk == cur - 1)
    score = jnp.where((blk <= cur)[None, :, None, :],
                      imp + jnp.where(forced, FORCE_BONUS, 0.0)[None, :, None, :], -jnp.inf)
    n_top = min(SEL_TOPK, n_sel)
    top_s, top_i = lax.top_k(score, n_top)
    bi = jnp.arange(b)[:, None, None, None]
    ki = jnp.arange(NSA_KV_HEADS)[None, None, :, None]
    k_s = ks_blk[bi, ki, top_i].astype(f32).reshape(b, lq, NSA_KV_HEADS, n_top * SEL_BLOCK, HEAD_DIM)
    v_s = vs_blk[bi, ki, top_i].astype(f32).reshape(b, lq, NSA_KV_HEADS, n_top * SEL_BLOCK, HEAD_DIM)
    key_pos = top_i[..., None] * SEL_BLOCK + jnp.arange(SEL_BLOCK)
    m_s = jnp.isfinite(top_s)[..., None] & (key_pos <= q_pos[None, :, None, None, None])
    s_s = jnp.einsum('bqkgd,bqkjd->bqkgj', qg, k_s)
    p_s = masked_softmax(s_s, m_s.reshape(b, lq, NSA_KV_HEADS, 1, n_top * SEL_BLOCK))
    o_s = jnp.einsum('bqkgj,bqkjd->bqkgd', p_s, v_s)
    s_w = jnp.einsum('bqkgd,blkd->bqkgl', qg, kw.astype(f32))
    dist = q_pos[:, None] - kw_pos[None, :]
    m_w = (dist >= 0) & (dist < WINDOW) & (kw_pos[None, :] >= 0)
    p_w = masked_softmax(s_w, m_w[None, :, None, None, :])
    o_w = jnp.einsum('bqkgl,blkd->bqkgd', p_w, vw.astype(f32))
    g = gates.astype(f32)[..., None]
    o = g[:, :, 0] * o_c + g[:, :, 1] * o_s + g[:, :, 2] * o_w
    return o.reshape(b, lq, NSA_QW).astype(q.dtype)


def hybrid_mixer(h, pos0, w_in, gdn_conv_w, gdn_a_log, gdn_dt_bias, gdn_norm_w, cmp_pe, cmp_w1, cmp_w2, w_out,
                 gdn_conv_buf, gdn_s0, past_cmp, past_sel, win_buf):
    b, l, _ = h.shape
    (gq, gk, gv, gz, ga, gb, nq, ck, cv, sk, sv, wk, wv, ng) = split_cols(h @ w_in, HYB_SPLITS)
    qkv_ext = jnp.concatenate([gdn_conv_buf, jnp.concatenate([gq, gk, gv], axis=-1)], axis=1)
    new_conv = qkv_ext[:, -(GDN_CONV - 1):]
    qkv = jax.nn.silu(causal_dwconv(qkv_ext, gdn_conv_w))
    q_a, k_a, v_a = split_cols(qkv, (GDN_QK, GDN_QK, GDN_VW))
    q_a = l2norm(q_a.reshape(b, l, GDN_HEADS, GDN_DK)) * GDN_DK ** -0.5
    k_a = l2norm(k_a.reshape(b, l, GDN_HEADS, GDN_DK))
    v_a = v_a.reshape(b, l, GDN_HEADS, GDN_DV)
    beta = jax.nn.sigmoid(gb.astype(jnp.float32))
    g_log = -jnp.exp(gdn_a_log.astype(jnp.float32)) * jax.nn.softplus(ga.astype(jnp.float32) + gdn_dt_bias.astype(jnp.float32))
    o_a, s_new = gdn_chunked(q_a, k_a, v_a, g_log, beta, gdn_s0)
    o_a = rmsnorm(o_a, gdn_norm_w) * jax.nn.silu(gz.reshape(b, l, GDN_HEADS, GDN_DV))
    gdn_out = o_a.reshape(b, l, GDN_VW)
    pos = pos0 + jnp.arange(l)
    q_b = rope(nq.reshape(b, l, NSA_HEADS, HEAD_DIM), pos)
    kvr = lambda a: a.reshape(b, l, NSA_KV_HEADS, HEAD_DIM)
    cmp_rows = jnp.stack([rope(kvr(ck), pos), kvr(cv)], axis=2)
    sel_rows = jnp.stack([rope(kvr(sk), pos), kvr(sv)], axis=2)
    win_rows = jnp.stack([rope(kvr(wk), pos), kvr(wv)], axis=2)
    gates = jax.nn.sigmoid(ng.astype(jnp.float32)).reshape(b, l, 3, NSA_KV_HEADS, NSA_GROUP)
    cmp_all = cmp_rows if past_cmp is None else jnp.concatenate([past_cmp, cmp_rows], axis=1)
    sel_all = sel_rows if past_sel is None else jnp.concatenate([past_sel, sel_rows], axis=1)
    kc, vc, c_end = compress_blocks(cmp_all, cmp_pe, cmp_w1, cmp_w2)
    ks_blk, vs_blk = selection_blocks(sel_all)
    if win_buf is None:
        qb_len = min(Q_BLOCK, l)
        n_qb = l // qb_len
        kw_pad = jnp.pad(win_rows, ((0, 0), (WINDOW, 0), (0, 0), (0, 0), (0, 0)))

        def query_block(i):
            s0 = i * qb_len
            q_blk = lax.dynamic_slice_in_dim(q_b, s0, qb_len, axis=1)
            g_blk = lax.dynamic_slice_in_dim(gates, s0, qb_len, axis=1)
            kw_blk = lax.dynamic_slice_in_dim(kw_pad, s0, WINDOW + qb_len, axis=1)
            q_pos = pos0 + s0 + jnp.arange(qb_len)
            kw_pos = pos0 + s0 - WINDOW + jnp.arange(WINDOW + qb_len)
            return nsa_core(q_blk, q_pos, kc, vc, c_end, ks_blk, vs_blk,
                            kw_blk[:, :, 0], kw_blk[:, :, 1], kw_pos, g_blk)

        nsa_out = lax.map(query_block, jnp.arange(n_qb))
        nsa_out = jnp.moveaxis(nsa_out, 0, 1).reshape(b, l, NSA_QW)
        new_win = win_rows[:, -min(WINDOW, l):]
    else:
        w_buf = win_buf.shape[1]
        win_all = jnp.concatenate([win_buf, win_rows], axis=1)
        kw_pos = pos0 - w_buf + jnp.arange(w_buf + l)
        nsa_out = nsa_core(q_b, pos, kc, vc, c_end, ks_blk, vs_blk,
                           win_all[:, :, 0], win_all[:, :, 1], kw_pos, gates)
        new_win = win_all[:, -w_buf:]
    out = jnp.concatenate([gdn_out, nsa_out], axis=-1) @ w_out
    return out, cmp_rows, sel_rows, new_win, new_conv, s_new


def ssm_mixer(h, w_in, conv_w, conv_b, a_log, dt_bias, d_skip, norm_w, w_out, conv_buf, h0):
    b, l, _ = h.shape
    z, xbc, dt = split_cols(h @ w_in, (SSM_D_INNER, SSM_CONV_DIM, SSM_HEADS))
    xbc_ext = jnp.concatenate([conv_buf, xbc], axis=1)
    new_conv = xbc_ext[:, -(SSM_CONV - 1):]
    xbc = jax.nn.silu(causal_dwconv(xbc_ext, conv_w) + conv_b)
    x, bm, cm = split_cols(xbc, (SSM_D_INNER, SSM_BC, SSM_BC))
    x = x.reshape(b, l, SSM_HEADS, SSM_HEAD_DIM)
    bm = bm.reshape(b, l, SSM_GROUPS, SSM_STATE)
    cm = cm.reshape(b, l, SSM_GROUPS, SSM_STATE)
    dt = jax.nn.softplus(dt.astype(jnp.float32) + dt_bias.astype(jnp.float32))
    a_neg = -jnp.exp(a_log.astype(jnp.float32))
    y, h_new = ssd_chunked(x, dt, a_neg, bm, cm, h0)
    y = y + d_skip.astype(jnp.float32)[:, None] * x.astype(jnp.float32)
    y = y.reshape(b, l, SSM_D_INNER) * jax.nn.silu(z.astype(jnp.float32))
    y = rmsnorm(y.reshape(b, l, SSM_GROUPS, SSM_D_INNER // SSM_GROUPS), norm_w.reshape(SSM_GROUPS, -1))
    return y.reshape(b, l, SSM_D_INNER).astype(h.dtype) @ w_out, new_conv, h_new


def swiglu(h, w_gate, w_up, w_down):
    return (jax.nn.silu(h @ w_gate) * (h @ w_up)) @ w_down


def moe_ffn(h, router, w_gate, w_up, w_down):
    logits = jnp.einsum('bld,de->ble', h, router).astype(jnp.float32)
    top_v, top_i = lax.top_k(logits, TOP_K)
    top_w = jax.nn.softmax(top_v, axis=-1)
    gate = jnp.sum(jax.nn.one_hot(top_i, N_EXPERTS, dtype=jnp.float32) * top_w[..., None], axis=-2)
    out = jnp.zeros(h.shape, jnp.float32)
    for e in range(N_EXPERTS):
        out = out + swiglu(h, w_gate[e], w_up[e], w_down[e]).astype(jnp.float32) * gate[..., e:e + 1]
    return out.astype(h.dtype)


def gather_pages(pool, page_table):
    g = pool[page_table]
    return g.reshape((g.shape[0], g.shape[1] * g.shape[2]) + g.shape[3:])


def _normal(key, shape, scale):
    return jax.random.normal(key, shape, jnp.float32) * scale


def _gain(key, shape):
    return 1.0 + 0.02 * jax.random.normal(key, shape, jnp.float32)


def _a_log(key, shape):
    return jnp.log(jax.random.uniform(key, shape, jnp.float32, 1.0, 16.0))


def _dt_bias(key, shape):
    dt = jnp.exp(jax.random.uniform(key, shape, jnp.float32, math.log(1e-3), math.log(1e-1)))
    return jnp.log(jnp.expm1(dt))


def setup_inputs(seed: int = 0) -> dict:
    key = jax.random.key(seed)
    k = jax.random.split(key, 48)
    n_pages = PAST_LEN // PAGE_SIZE
    n_used = DEC_BATCH * n_pages
    n_phys = n_used + n_used // 4
    w_buf = min(WINDOW, PAST_LEN)
    page_table = jax.random.permutation(k[9], n_phys)[:n_used].reshape(DEC_BATCH, n_pages).astype(jnp.int32)
    return {
        'x_prompt': _normal(k[0], (BATCH, SEQ, D_MODEL), 1.0),
        'x_sample': _normal(k[1], (DEC_BATCH, DEC_SEQ, D_MODEL), 1.0),
        'cache_cmp_kv': _normal(k[2], (N_HYB, n_phys, PAGE_SIZE, 2, NSA_KV_HEADS, HEAD_DIM), 1.0),
        'cache_sel_kv': _normal(k[3], (N_HYB, n_phys, PAGE_SIZE, 2, NSA_KV_HEADS, HEAD_DIM), 1.0),
        'cache_win_kv': _normal(k[4], (N_HYB, DEC_BATCH, w_buf, 2, NSA_KV_HEADS, HEAD_DIM), 1.0),
        'state_gdn_conv': _normal(k[5], (N_HYB, DEC_BATCH, GDN_CONV - 1, GDN_CONV_DIM), 1.0),
        'state_gdn': _normal(k[6], (N_HYB, DEC_BATCH, GDN_HEADS, GDN_DK, GDN_DV), 0.1),
        'state_ssm_conv': _normal(k[7], (N_SSM, DEC_BATCH, SSM_CONV - 1, SSM_CONV_DIM), 1.0),
        'state_ssm': _normal(k[8], (N_SSM, DEC_BATCH, SSM_HEADS, SSM_HEAD_DIM, SSM_STATE), 0.05),
        'page_table': page_table,
        'hyb_norm_mix': _gain(k[10], (N_HYB, D_MODEL)),
        'hyb_w_in': _normal(k[11], (N_HYB, D_MODEL, HYB_IN), D_MODEL ** -0.5),
        'hyb_gdn_conv_w': _normal(k[12], (N_HYB, GDN_CONV, GDN_CONV_DIM), GDN_CONV ** -0.5),
        'hyb_gdn_a_log': _a_log(k[13], (N_HYB, GDN_HEADS)),
        'hyb_gdn_dt_bias': _dt_bias(k[14], (N_HYB, GDN_HEADS)),
        'hyb_gdn_norm_w': _gain(k[15], (N_HYB, GDN_DV)),
        'hyb_cmp_pe': _normal(k[16], (N_HYB, 2, CMP_BLOCK, HEAD_DIM), 0.1),
        'hyb_cmp_w1': _normal(k[17], (N_HYB, 2, CMP_BLOCK * HEAD_DIM, CMP_HIDDEN), (CMP_BLOCK * HEAD_DIM) ** -0.5),
        'hyb_cmp_w2': _normal(k[18], (N_HYB, 2, CMP_HIDDEN, HEAD_DIM), CMP_HIDDEN ** -0.5),
        'hyb_w_out': _normal(k[19], (N_HYB, HYB_MIX, D_MODEL), HYB_MIX ** -0.5),
        'hyb_norm_ffn': _gain(k[20], (N_HYB, D_MODEL)),
        'ffn_w_gate': _normal(k[21], (N_HYB, D_MODEL, D_FF), D_MODEL ** -0.5),
        'ffn_w_up': _normal(k[22], (N_HYB, D_MODEL, D_FF), D_MODEL ** -0.5),
        'ffn_w_down': _normal(k[23], (N_HYB, D_FF, D_MODEL), D_FF ** -0.5),
        'ssm_norm_mix': _gain(k[24], (N_SSM, D_MODEL)),
        'ssm_w_in': _normal(k[25], (N_SSM, D_MODEL, SSM_IN), D_MODEL ** -0.5),
        'ssm_conv_w': _normal(k[26], (N_SSM, SSM_CONV, SSM_CONV_DIM), SSM_CONV ** -0.5),
        'ssm_conv_b': _normal(k[27], (N_SSM, SSM_CONV_DIM), 0.02),
        'ssm_a_log': _a_log(k[28], (N_SSM, SSM_HEADS)),
        'ssm_dt_bias': _dt_bias(k[29], (N_SSM, SSM_HEADS)),
        'ssm_d_skip': 1.0 + _normal(k[30], (N_SSM, SSM_HEADS), 0.1),
        'ssm_norm_w': _gain(k[31], (N_SSM, SSM_D_INNER)),
        'ssm_w_out': _normal(k[32], (N_SSM, SSM_D_INNER, D_MODEL), SSM_D_INNER ** -0.5),
        'ssm_norm_ffn': _gain(k[33], (N_SSM, D_MODEL)),
        'moe_router': _normal(k[34], (N_SSM, D_MODEL, N_EXPERTS), D_MODEL ** -0.5),
        'moe_w_gate': _normal(k[35], (N_SSM, N_EXPERTS, D_MODEL, D_FF_EXPERT), D_MODEL ** -0.5),
        'moe_w_up': _normal(k[36], (N_SSM, N_EXPERTS, D_MODEL, D_FF_EXPERT), D_MODEL ** -0.5),
        'moe_w_down': _normal(k[37], (N_SSM, N_EXPERTS, D_FF_EXPERT, D_MODEL), D_FF_EXPERT ** -0.5),
        'final_norm': _gain(k[38], (D_MODEL,)),
    }


def reference(x_prompt, x_sample, cache_cmp_kv, cache_sel_kv, cache_win_kv, state_gdn_conv, state_gdn,
              state_ssm_conv, state_ssm, page_table, hyb_norm_mix, hyb_w_in, hyb_gdn_conv_w, hyb_gdn_a_log,
              hyb_gdn_dt_bias, hyb_gdn_norm_w, hyb_cmp_pe, hyb_cmp_w1, hyb_cmp_w2, hyb_w_out, hyb_norm_ffn,
              ffn_w_gate, ffn_w_up, ffn_w_down, ssm_norm_mix, ssm_w_in, ssm_conv_w, ssm_conv_b, ssm_a_log,
              ssm_dt_bias, ssm_d_skip, ssm_norm_w, ssm_w_out, ssm_norm_ffn, moe_router, moe_w_gate, moe_w_up,
              moe_w_down, final_norm):
    hp, hs = x_prompt, x_sample
    bp = hp.shape[0]
    (cmp_p, cmp_s, sel_p, sel_s, win_p, win_s, gconv_p, gconv_s,
     gst_p, gst_s, sconv_p, sconv_s, sst_p, sst_s) = ([] for _ in range(14))
    for layer in range(DEPTH):
        i = layer // 2
        if layer % 2 == 0:
            hw = (hyb_w_in[i], hyb_gdn_conv_w[i], hyb_gdn_a_log[i], hyb_gdn_dt_bias[i], hyb_gdn_norm_w[i],
                  hyb_cmp_pe[i], hyb_cmp_w1[i], hyb_cmp_w2[i], hyb_w_out[i])
            zero_conv = jnp.zeros((bp, GDN_CONV - 1, GDN_CONV_DIM), hp.dtype)
            zero_s = jnp.zeros((bp, GDN_HEADS, GDN_DK, GDN_DV), hp.dtype)
            m, c_rows, s_rows, w_rows, conv_new, s_new = hybrid_mixer(
                rmsnorm(hp, hyb_norm_mix[i]), 0, *hw, zero_conv, zero_s, None, None, None)
            hp = hp + m
            hp = hp + swiglu(rmsnorm(hp, hyb_norm_ffn[i]), ffn_w_gate[i], ffn_w_up[i], ffn_w_down[i])
            cmp_p.append(c_rows); sel_p.append(s_rows); win_p.append(w_rows)
            gconv_p.append(conv_new); gst_p.append(s_new)
            past_cmp = gather_pages(cache_cmp_kv[i], page_table)
            past_sel = gather_pages(cache_sel_kv[i], page_table)
            m, c_rows, s_rows, w_rows, conv_new, s_new = hybrid_mixer(
                rmsnorm(hs, hyb_norm_mix[i]), PAST_LEN, *hw, state_gdn_conv[i], state_gdn[i],
                past_cmp, past_sel, cache_win_kv[i])
            hs = hs + m
            hs = hs + swiglu(rmsnorm(hs, hyb_norm_ffn[i]), ffn_w_gate[i], ffn_w_up[i], ffn_w_down[i])
            cmp_s.append(c_rows); sel_s.append(s_rows); win_s.append(w_rows)
            gconv_s.append(conv_new); gst_s.append(s_new)
        else:
            sw = (ssm_w_in[i], ssm_conv_w[i], ssm_conv_b[i], ssm_a_log[i], ssm_dt_bias[i], ssm_d_skip[i],
                  ssm_norm_w[i], ssm_w_out[i])
            mw = (moe_router[i], moe_w_gate[i], moe_w_up[i], moe_w_down[i])
            zero_conv = jnp.zeros((bp, SSM_CONV - 1, SSM_CONV_DIM), hp.dtype)
            zero_h = jnp.zeros((bp, SSM_HEADS, SSM_HEAD_DIM, SSM_STATE), hp.dtype)
            m, conv_new, h_new = ssm_mixer(rmsnorm(hp, ssm_norm_mix[i]), *sw, zero_conv, zero_h)
            hp = hp + m
            hp = hp + moe_ffn(rmsnorm(hp, ssm_norm_ffn[i]), *mw)
            sconv_p.append(conv_new); sst_p.append(h_new)
            m, conv_new, h_new = ssm_mixer(rmsnorm(hs, ssm_norm_mix[i]), *sw, state_ssm_conv[i], state_ssm[i])
            hs = hs + m
            hs = hs + moe_ffn(rmsnorm(hs, ssm_norm_ffn[i]), *mw)
            sconv_s.append(conv_new); sst_s.append(h_new)
    y_prompt = rmsnorm(hp, final_norm)
    y_sample = rmsnorm(hs, final_norm)
    new_cmp_kv_prompt, new_cmp_kv_sample = jnp.stack(cmp_p), jnp.stack(cmp_s)
    new_sel_kv_prompt, new_sel_kv_sample = jnp.stack(sel_p), jnp.stack(sel_s)
    new_win_kv_prompt, new_win_kv_sample = jnp.stack(win_p), jnp.stack(win_s)
    new_gdn_conv_prompt, new_gdn_conv_sample = jnp.stack(gconv_p), jnp.stack(gconv_s)
    new_gdn_state_prompt, new_gdn_state_sample = jnp.stack(gst_p), jnp.stack(gst_s)
    new_ssm_conv_prompt, new_ssm_conv_sample = jnp.stack(sconv_p), jnp.stack(sconv_s)
    new_ssm_state_prompt, new_ssm_state_sample = jnp.stack(sst_p), jnp.stack(sst_s)
    return (y_prompt, y_sample,
            new_cmp_kv_prompt, new_cmp_kv_sample,
            new_sel_kv_prompt, new_sel_kv_sample,
            new_win_kv_prompt, new_win_kv_sample,
            new_gdn_conv_prompt, new_gdn_conv_sample,
            new_gdn_state_prompt, new_gdn_state_sample,
            new_ssm_conv_prompt, new_ssm_conv_sample,
            new_ssm_state_prompt, new_ssm_state_sample)
```

```python
import functools
import math

import jax
import jax.numpy as jnp
from jax import lax
from jax.experimental import pallas as pl
from jax.experimental.pallas import tpu as pltpu

D_MODEL = 2048
PAST_LEN = 16384
HEAD_DIM = 128
ROPE_THETA = 10000.0
RMS_EPS = 1e-6

GDN_HEADS = D_MODEL // 256
GDN_DK = 128
GDN_DV = 128
GDN_CONV = 4
GDN_CHUNK = 64
GDN_QK = GDN_HEADS * GDN_DK
GDN_VW = GDN_HEADS * GDN_DV
GDN_CONV_DIM = 2 * GDN_QK + GDN_VW

NSA_HEADS = D_MODEL // 256
NSA_KV_HEADS = 2
NSA_GROUP = NSA_HEADS // NSA_KV_HEADS
NSA_QW = NSA_HEADS * HEAD_DIM
NSA_KVW = NSA_KV_HEADS * HEAD_DIM
CMP_BLOCK = 32
CMP_STRIDE = 16
SEL_BLOCK = 64
SEL_TOPK = 16
WINDOW = 512
Q_BLOCK = 128
FORCE_BONUS = 1e4

HYB_SPLITS = (GDN_QK, GDN_QK, GDN_VW, GDN_VW, GDN_HEADS, GDN_HEADS,
              NSA_QW, NSA_KVW, NSA_KVW, NSA_KVW, NSA_KVW, NSA_KVW, NSA_KVW, 3 * NSA_HEADS)
HYB_IN = sum(HYB_SPLITS)

SSM_D_INNER = 2 * D_MODEL
SSM_HEAD_DIM = 64
SSM_HEADS = SSM_D_INNER // SSM_HEAD_DIM
SSM_GROUPS = 8
SSM_HPG = SSM_HEADS // SSM_GROUPS
SSM_STATE = 128
SSM_CONV = 4
SSM_CHUNK = 128
SSM_BC = SSM_GROUPS * SSM_STATE
SSM_CONV_DIM = SSM_D_INNER + 2 * SSM_BC
SSM_IN = SSM_D_INNER + SSM_CONV_DIM + SSM_HEADS

N_EXPERTS = 8
TOP_K = 2

VMEM_LIMIT_BYTES = 56 * 1024 * 1024
LANES = 128

F32 = jnp.float32
BF16 = jnp.bfloat16


def _params(*sem):
    return pltpu.CompilerParams(dimension_semantics=sem, vmem_limit_bytes=VMEM_LIMIT_BYTES)


def _pick(n, prefs):
    for p in prefs:
        if n % p == 0:
            return p
    return n


def _rms_to_bf16(x, g):
    ms = jnp.mean(x * x, axis=-1, keepdims=True)
    return (x * lax.rsqrt(ms + RMS_EPS) * g).astype(BF16)


def _rms_mm_kernel(x_ref, g_ref, w_ref, o_ref, a_scr):
    @pl.when(pl.program_id(1) == 0)
    def _():
        a_scr[...] = _rms_to_bf16(x_ref[...], g_ref[...])

    o_ref[...] = jnp.dot(a_scr[...], w_ref[...], preferred_element_type=F32).astype(o_ref.dtype)


def rms_mm(x, gain, w, out_dtype=F32):
    m, k = x.shape
    n = w.shape[1]
    tm = _pick(m, (512, 256, 128))
    tn = _pick(n, (512, 256, 128))
    return pl.pallas_call(
        _rms_mm_kernel,
        grid=(m // tm, n // tn),
        in_specs=[pl.BlockSpec((tm, k), lambda i, j: (i, 0)),
                  pl.BlockSpec((1, k), lambda i, j: (0, 0)),
                  pl.BlockSpec((k, tn), lambda i, j: (0, j))],
        out_specs=pl.BlockSpec((tm, tn), lambda i, j: (i, j)),
        out_shape=jax.ShapeDtypeStruct((m, n), out_dtype),
        scratch_shapes=[pltpu.VMEM((tm, k), BF16)],
        compiler_params=_params("parallel", "arbitrary"),
        name="rms_mm",
    )(x, gain.reshape(1, k).astype(F32), w)


def _rms_glu_kernel(x_ref, g_ref, wg_ref, wu_ref, o_ref, a_scr):
    @pl.when(pl.program_id(1) == 0)
    def _():
        a_scr[...] = _rms_to_bf16(x_ref[...], g_ref[...])

    a = a_scr[...]
    gt = jnp.dot(a, wg_ref[...], preferred_element_type=F32)
    up = jnp.dot(a, wu_ref[...], preferred_element_type=F32)
    o_ref[...] = (gt * jax.nn.sigmoid(gt) * up).astype(o_ref.dtype)


def rms_glu(x, gain, wg, wu):
    m, k = x.shape
    n = wg.shape[1]
    tm = _pick(m, (512, 256, 128))
    tn = _pick(n, (512, 256, 128))
    return pl.pallas_call(
        _rms_glu_kernel,
        grid=(m // tm, n // tn),
        in_specs=[pl.BlockSpec((tm, k), lambda i, j: (i, 0)),
                  pl.BlockSpec((1, k), lambda i, j: (0, 0)),
                  pl.BlockSpec((k, tn), lambda i, j: (0, j)),
                  pl.BlockSpec((k, tn), lambda i, j: (0, j))],
        out_specs=pl.BlockSpec((tm, tn), lambda i, j: (i, j)),
        out_shape=jax.ShapeDtypeStruct((m, n), BF16),
        scratch_shapes=[pltpu.VMEM((tm, k), BF16)],
        compiler_params=_params("parallel", "arbitrary"),
        name="rms_glu",
    )(x, gain.reshape(1, k).astype(F32), wg, wu)


def _mm_resid_kernel(a_ref, w_ref, r_ref, o_ref, acc_ref, *, nk):
    kk = pl.program_id(2)

    @pl.when(kk == 0)
    def _():
        acc_ref[...] = r_ref[...]

    acc_ref[...] += jnp.dot(a_ref[...], w_ref[...], preferred_element_type=F32)

    @pl.when(kk == nk - 1)
    def _():
        o_ref[...] = acc_ref[...]


def mm_resid(a, w, resid):
    m, k = a.shape
    n = w.shape[1]
    tm = _pick(m, (512, 256, 128))
    tn = _pick(n, (512, 256, 128))
    tk = _pick(k, (2048, 1792, 1408, 1024, 512))
    nk = k // tk
    return pl.pallas_call(
        functools.partial(_mm_resid_kernel, nk=nk),
        grid=(m // tm, n // tn, nk),
        in_specs=[pl.BlockSpec((tm, tk), lambda i, j, kk: (i, kk)),
                  pl.BlockSpec((tk, tn), lambda i, j, kk: (kk, j)),
                  pl.BlockSpec((tm, tn), lambda i, j, kk: (i, j))],
        out_specs=pl.BlockSpec((tm, tn), lambda i, j, kk: (i, j)),
        out_shape=jax.ShapeDtypeStruct((m, n), F32),
        scratch_shapes=[pltpu.VMEM((tm, tn), F32)],
        compiler_params=_params("parallel", "parallel", "arbitrary"),
        name="mm_resid",
    )(a, w, resid)


def _rms_kernel(x_ref, g_ref, o_ref):
    x = x_ref[...]
    ms = jnp.mean(x * x, axis=-1, keepdims=True)
    o_ref[...] = (x * lax.rsqrt(ms + RMS_EPS) * g_ref[...]).astype(o_ref.dtype)


def rms(x, gain, out_dtype=F32):
    m, k = x.shape
    tm = _pick(m, (512, 256, 128))
    return pl.pallas_call(
        _rms_kernel,
        grid=(m // tm,),
        in_specs=[pl.BlockSpec((tm, k), lambda i: (i, 0)),
                  pl.BlockSpec((1, k), lambda i: (0, 0))],
        out_specs=pl.BlockSpec((tm, k), lambda i: (i, 0)),
        out_shape=jax.ShapeDtypeStruct((m, k), out_dtype),
        compiler_params=_params("parallel"),
        name="rms",
    )(x, gain.reshape(1, k).astype(F32))


def _rms_router_kernel(x_ref, g_ref, wh_ref, wl_ref, h_ref, lg_ref):
    x = x_ref[...]
    ms = jnp.mean(x * x, axis=-1, keepdims=True)
    h = x * lax.rsqrt(ms + RMS_EPS) * g_ref[...]
    hh = h.astype(BF16)
    hl = (h - hh.astype(F32)).astype(BF16)
    h_ref[...] = hh
    wh = wh_ref[...]
    wl = wl_ref[...]
    lg = jnp.dot(hh, wh, preferred_element_type=F32)
    lg += jnp.dot(hl, wh, preferred_element_type=F32)
    lg += jnp.dot(hh, wl, preferred_element_type=F32)
    lg_ref[...] = lg


def rms_router(x, gain, router):
    m, k = x.shape
    e = router.shape[1]
    rp = jnp.pad(router.astype(F32), ((0, 0), (0, LANES - e)))
    rh = rp.astype(BF16)
    rl = (rp - rh.astype(F32)).astype(BF16)
    tm = _pick(m, (512, 256, 128, 32))
    h, lg = pl.pallas_call(
        _rms_router_kernel,
        grid=(m // tm,),
        in_specs=[pl.BlockSpec((tm, k), lambda i: (i, 0)),
                  pl.BlockSpec((1, k), lambda i: (0, 0)),
                  pl.BlockSpec((k, LANES), lambda i: (0, 0)),
                  pl.BlockSpec((k, LANES), lambda i: (0, 0))],
        out_specs=[pl.BlockSpec((tm, k), lambda i: (i, 0)),
                   pl.BlockSpec((tm, LANES), lambda i: (i, 0))],
        out_shape=[jax.ShapeDtypeStruct((m, k), BF16), jax.ShapeDtypeStruct((m, LANES), F32)],
        compiler_params=_params("parallel"),
        name="rms_router",
    )(x, gain.reshape(1, k).astype(F32), rh, rl)
    return h, lg[:, :e]


MOE_TM = 256


def _moe_glu_kernel(te_ref, tv_ref, a_ref, wg_ref, wu_ref, o_ref):
    i = pl.program_id(1)

    @pl.when(tv_ref[i] != 0)
    def _():
        a = a_ref[...]
        gt = jnp.dot(a, wg_ref[...], preferred_element_type=F32)
        up = jnp.dot(a, wu_ref[...], preferred_element_type=F32)
        o_ref[...] = (gt * jax.nn.sigmoid(gt) * up).astype(o_ref.dtype)

    @pl.when(tv_ref[i] == 0)
    def _():
        o_ref[...] = jnp.zeros_like(o_ref)


def _moe_down_kernel(te_ref, tv_ref, a_ref, w_ref, o_ref):
    i = pl.program_id(1)

    @pl.when(tv_ref[i] != 0)
    def _():
        o_ref[...] = jnp.dot(a_ref[...], w_ref[...], preferred_element_type=F32)

    @pl.when(tv_ref[i] == 0)
    def _():
        o_ref[...] = jnp.zeros_like(o_ref)


def moe_experts(a_sorted, tile_expert, tile_valid, wg, wu, wd):
    r, d = a_sorted.shape
    f = wg.shape[2]
    tm = MOE_TM
    tf = _pick(f, (1024, 512))
    act = pl.pallas_call(
        _moe_glu_kernel,
        grid_spec=pltpu.PrefetchScalarGridSpec(
            num_scalar_prefetch=2,
            grid=(f // tf, r // tm),
            in_specs=[pl.BlockSpec((tm, d), lambda j, i, te, tv: (i, 0)),
                      pl.BlockSpec((None, d, tf), lambda j, i, te, tv: (te[i], 0, j)),
                      pl.BlockSpec((None, d, tf), lambda j, i, te, tv: (te[i], 0, j))],
            out_specs=pl.BlockSpec((tm, tf), lambda j, i, te, tv: (i, j)),
        ),
        out_shape=jax.ShapeDtypeStruct((r, f), BF16),
        compiler_params=_params("arbitrary", "arbitrary"),
        name="moe_glu",
    )(tile_expert, tile_valid, a_sorted, wg, wu)
    tn = _pick(d, (1024, 512))
    return pl.pallas_call(
        _moe_down_kernel,
        grid_spec=pltpu.PrefetchScalarGridSpec(
            num_scalar_prefetch=2,
            grid=(d // tn, r // tm),
            in_specs=[pl.BlockSpec((tm, f), lambda j, i, te, tv: (i, 0)),
                      pl.BlockSpec((None, f, tn), lambda j, i, te, tv: (te[i], 0, j))],
            out_specs=pl.BlockSpec((tm, tn), lambda j, i, te, tv: (i, j)),
        ),
        out_shape=jax.ShapeDtypeStruct((r, d), F32),
        compiler_params=_params("arbitrary", "arbitrary"),
        name="moe_down",
    )(tile_expert, tile_valid, act, wd)


def moe_ffn(x, gain, router, wg, wu, wd):
    t, d = x.shape
    tm = MOE_TM
    h, logits = rms_router(x, gain, router)
    top_v, top_i = lax.top_k(logits, TOP_K)
    top_w = jax.nn.softmax(top_v, axis=-1)
    n_asg = t * TOP_K
    n_rows = (-(-n_asg // tm) + N_EXPERTS) * tm
    flat_e = top_i.reshape(-1).astype(jnp.int32)
    order = jnp.argsort(flat_e, stable=True).astype(jnp.int32)
    sorted_e = flat_e[order]
    counts = jnp.sum(flat_e[:, None] == jnp.arange(N_EXPERTS, dtype=jnp.int32)[None, :], axis=0).astype(jnp.int32)
    padded = ((counts + tm - 1) // tm) * tm
    pad_end = jnp.cumsum(padded)
    pad_start = pad_end - padded
    start = jnp.cumsum(counts) - counts
    dest = pad_start[sorted_e] + (jnp.arange(n_asg, dtype=jnp.int32) - start[sorted_e])
    row_src = jnp.zeros((n_rows,), jnp.int32).at[dest].set(order // TOP_K)
    pos = jnp.zeros((n_asg,), jnp.int32).at[order].set(dest)
    tile_start = jnp.arange(n_rows // tm, dtype=jnp.int32) * tm
    tile_valid = (tile_start < pad_end[-1]).astype(jnp.int32)
    tile_expert = jnp.minimum(jnp.searchsorted(pad_end, tile_start, side="right"), N_EXPERTS - 1).astype(jnp.int32)
    last_e = tile_expert[jnp.maximum(pad_end[-1] // tm - 1, 0)]
    tile_expert = jnp.where(tile_valid != 0, tile_expert, last_e)
    a_sorted = jnp.take(h, row_src, axis=0)
    y = moe_experts(a_sorted, tile_expert, tile_valid, wg, wu, wd)
    yk = jnp.take(y, pos, axis=0).reshape(t, TOP_K, d)
    return jnp.sum(yk * top_w[..., None], axis=1)


def rmsnorm(x, w):
    xf = x.astype(F32)
    y = xf * lax.rsqrt(jnp.mean(xf * xf, axis=-1, keepdims=True) + RMS_EPS)
    return (y * w.astype(F32)).astype(x.dtype)


def l2norm(x):
    xf = x.astype(F32)
    return (xf * lax.rsqrt(jnp.sum(xf * xf, axis=-1, keepdims=True) + 1e-6)).astype(x.dtype)


def rope(x, pos):
    half = x.shape[-1] // 2
    inv_freq = 1.0 / (ROPE_THETA ** (jnp.arange(half, dtype=F32) / half))
    ang = pos.astype(F32)[:, None] * inv_freq[None, :]
    cos = jnp.cos(ang)[None, :, None, :]
    sin = jnp.sin(ang)[None, :, None, :]
    xf = x.astype(F32)
    x1, x2 = xf[..., :half], xf[..., half:]
    return jnp.concatenate([x1 * cos - x2 * sin, x2 * cos + x1 * sin], axis=-1).astype(x.dtype)


def split_cols(a, sizes):
    out, s = [], 0
    for n in sizes:
        out.append(a[..., s:s + n])
        s += n
    return out


def causal_dwconv(x_ext, w):
    c = x_ext.shape[-1]
    return lax.conv_general_dilated(x_ext, w[:, None, :], window_strides=(1,), padding='VALID',
                                    dimension_numbers=('NWC', 'WIO', 'NWC'), feature_group_count=c)


def masked_softmax(s, mask):
    s = jnp.where(mask, s, -jnp.inf)
    m = jnp.max(s, axis=-1, keepdims=True)
    m = jnp.where(jnp.isfinite(m), m, 0.0)
    e = jnp.where(mask, jnp.exp(s - m), 0.0)
    den = jnp.sum(e, axis=-1, keepdims=True)
    return e / jnp.where(den > 0, den, 1.0)


def to_chunks(a, n, c, heads_first):
    b, l = a.shape[:2]
    a = jnp.pad(a.astype(F32), [(0, 0), (0, n * c - l)] + [(0, 0)] * (a.ndim - 2))
    a = a.reshape((b, n, c) + a.shape[2:])
    if heads_first:
        perm = (1, 0, 3, 2) + tuple(range(4, a.ndim))
    else:
        perm = (1, 0) + tuple(range(2, a.ndim))
    return a.transpose(perm)


def gdn_chunked(q, k, v, g, beta, s0):
    b, l = q.shape[:2]
    dv = v.shape[-1]
    c = min(GDN_CHUNK, l)
    n = -(-l // c)
    qc, kc, vc = (to_chunks(a, n, c, True) for a in (q, k, v))
    gc, bc = (to_chunks(a, n, c, True) for a in (g, beta))
    incl = jnp.tri(c, dtype=bool)
    strict = jnp.tri(c, k=-1, dtype=F32)
    eye = jnp.eye(c, dtype=F32)

    def step(s, inp):
        qi, ki, vi, gi, bi = inp
        gam = jnp.cumsum(gi, axis=-1)
        dmask = jnp.exp(jnp.where(incl, gam[..., :, None] - gam[..., None, :], -jnp.inf))
        kb = ki * bi[..., None]
        a_mat = eye + jnp.einsum('bhid,bhjd->bhij', kb, ki) * dmask * strict
        rhs = jnp.concatenate([vi * bi[..., None], kb * jnp.exp(gam)[..., None]], axis=-1)
        sol = lax.linalg.triangular_solve(a_mat, rhs, left_side=True, lower=True, unit_diagonal=True)
        u, w = sol[..., :dv], sol[..., dv:]
        v_new = u - jnp.einsum('bhcd,bhde->bhce', w, s)
        attn = jnp.einsum('bhid,bhjd->bhij', qi, ki) * dmask
        o = (jnp.einsum('bhcd,bhde->bhce', qi * jnp.exp(gam)[..., None], s)
             + jnp.einsum('bhij,bhje->bhie', attn, v_new))
        g_end = gam[..., -1:]
        s = (s * jnp.exp(g_end)[..., None]
             + jnp.einsum('bhcd,bhce->bhde', ki * jnp.exp(g_end - gam)[..., None], v_new))
        return s, o

    s_fin, o = lax.scan(step, s0.astype(F32), (qc, kc, vc, gc, bc))
    o = o.transpose(1, 0, 3, 2, 4).reshape(b, n * c, q.shape[2], dv)[:, :l]
    return o.astype(v.dtype), s_fin.astype(s0.dtype)


def ssd_chunked(x, dt, a_neg, bmat, cmat, h0):
    b, l = x.shape[:2]
    c = min(SSM_CHUNK, l)
    n = -(-l // c)
    xc, dtc, bc, cc = (to_chunks(a, n, c, False) for a in (x, dt, bmat, cmat))
    incl = jnp.tri(c, dtype=bool)
    a_neg = a_neg.astype(F32)

    def step(h, inp):
        xi, dti, bi, ci = inp
        bn = dti.shape[0]
        gam = jnp.cumsum(dti * a_neg, axis=1)
        diff = gam[:, :, None, :] - gam[:, None, :, :]
        lmat = jnp.exp(jnp.where(incl[None, :, :, None], diff, -jnp.inf)).reshape(bn, c, c, SSM_GROUPS, SSM_HPG)
        xdt = (xi * dti[..., None]).reshape(bn, c, SSM_GROUPS, SSM_HPG, SSM_HEAD_DIM)
        cb = jnp.einsum('bign,bjgn->bijg', ci, bi)
        y_in = jnp.einsum('bijg,bijgh,bjghp->bighp', cb, lmat, xdt)
        h5 = h.reshape(bn, SSM_GROUPS, SSM_HPG, SSM_HEAD_DIM, SSM_STATE)
        dec_in = jnp.exp(gam).reshape(bn, c, SSM_GROUPS, SSM_HPG)
        y_st = jnp.einsum('bign,bghpn->bighp', ci, h5) * dec_in[..., None]
        g_end = gam[:, -1]
        dec_out = jnp.exp(g_end[:, None, :] - gam).reshape(bn, c, SSM_GROUPS, SSM_HPG)
        h_new = (h * jnp.exp(g_end)[:, :, None, None]
                 + jnp.einsum('bjgn,bjghp->bghpn', bi, xdt * dec_out[..., None]).reshape(h.shape))
        return h_new, (y_in + y_st).reshape(bn, c, SSM_HEADS, SSM_HEAD_DIM)

    h_fin, y = lax.scan(step, h0.astype(F32), (xc, dtc, bc, cc))
    y = y.transpose(1, 0, 2, 3, 4).reshape(b, n * c, SSM_HEADS, SSM_HEAD_DIM)[:, :l]
    return y, h_fin.astype(h0.dtype)


def compress_blocks(rows, pe, w1, w2):
    b, lk = rows.shape[:2]
    n_cmp = (lk - CMP_BLOCK) // CMP_STRIDE + 1
    per = CMP_BLOCK // CMP_STRIDE
    segs = rows[:, :(n_cmp + per - 1) * CMP_STRIDE].reshape(
        b, n_cmp + per - 1, CMP_STRIDE, 2, NSA_KV_HEADS, HEAD_DIM)
    blocks = jnp.concatenate([segs[:, i:i + n_cmp] for i in range(per)], axis=2)
    blocks = blocks + jnp.transpose(pe, (1, 0, 2))[:, :, None, :]
    flat = jnp.transpose(blocks, (0, 1, 3, 4, 2, 5)).reshape(b, n_cmp, 2, NSA_KV_HEADS, CMP_BLOCK * HEAD_DIM)
    hid = jax.nn.silu(jnp.einsum('bcskf,sfh->bcskh', flat, w1))
    out = jnp.einsum('bcskh,she->bcske', hid, w2)
    c_end = jnp.arange(n_cmp) * CMP_STRIDE + CMP_BLOCK - 1
    return out[:, :, 0], out[:, :, 1], c_end


def selection_blocks(rows):
    b, lk = rows.shape[:2]
    n_sel = -(-lk // SEL_BLOCK)
    rows = jnp.pad(rows, ((0, 0), (0, n_sel * SEL_BLOCK - lk), (0, 0), (0, 0), (0, 0)))
    blk = rows.reshape(b, n_sel, SEL_BLOCK, 2, NSA_KV_HEADS, HEAD_DIM).transpose(3, 0, 4, 1, 2, 5)
    return blk[0], blk[1]


def cmp_to_sel_map(n_cmp, n_sel):
    cs = jnp.arange(n_cmp)[:, None] * CMP_STRIDE
    ss = jnp.arange(n_sel)[None, :] * SEL_BLOCK
    return ((cs < ss + SEL_BLOCK) & (cs + CMP_BLOCK > ss)).astype(F32)


def nsa_core(q, q_pos, kc, vc, c_end, ks_blk, vs_blk, kw, vw, kw_pos, gates):
    b, lq = q.shape[:2]
    qg = q.astype(F32).reshape(b, lq, NSA_KV_HEADS, NSA_GROUP, HEAD_DIM) * HEAD_DIM ** -0.5
    s_c = jnp.einsum('bqkgd,bckd->bqkgc', qg, kc.astype(F32))
    m_c = c_end[None, :] <= q_pos[:, None]
    p_c = masked_softmax(s_c, m_c[None, :, None, None, :])
    o_c = jnp.einsum('bqkgc,bckd->bqkgd', p_c, vc.astype(F32))
    n_cmp, n_sel = kc.shape[1], ks_blk.shape[2]
    imp = jnp.einsum('bqkgc,cj->bqkj', p_c, cmp_to_sel_map(n_cmp, n_sel))
    blk = jnp.arange(n_sel)[None, :]
    cur = (q_pos // SEL_BLOCK)[:, None]
    forced = (blk == 0) | (blk == cur) | (blk == cur - 1)
    score = jnp.where((blk <= cur)[None, :, None, :],
                      imp + jnp.where(forced, FORCE_BONUS, 0.0)[None, :, None, :], -jnp.inf)
    n_top = min(SEL_TOPK, n_sel)
    top_s, top_i = lax.top_k(score, n_top)
    bi = jnp.arange(b)[:, None, None, None]
    ki = jnp.arange(NSA_KV_HEADS)[None, None, :, None]
    k_s = ks_blk[bi, ki, top_i].astype(F32).reshape(b, lq, NSA_KV_HEADS, n_top * SEL_BLOCK, HEAD_DIM)
    v_s = vs_blk[bi, ki, top_i].astype(F32).reshape(b, lq, NSA_KV_HEADS, n_top * SEL_BLOCK, HEAD_DIM)
    key_pos = top_i[..., None] * SEL_BLOCK + jnp.arange(SEL_BLOCK)
    m_s = jnp.isfinite(top_s)[..., None] & (key_pos <= q_pos[None, :, None, None, None])
    s_s = jnp.einsum('bqkgd,bqkjd->bqkgj', qg, k_s)
    p_s = masked_softmax(s_s, m_s.reshape(b, lq, NSA_KV_HEADS, 1, n_top * SEL_BLOCK))
    o_s = jnp.einsum('bqkgj,bqkjd->bqkgd', p_s, v_s)
    s_w = jnp.einsum('bqkgd,blkd->bqkgl', qg, kw.astype(F32))
    dist = q_pos[:, None] - kw_pos[None, :]
    m_w = (dist >= 0) & (dist < WINDOW) & (kw_pos[None, :] >= 0)
    p_w = masked_softmax(s_w, m_w[None, :, None, None, :])
    o_w = jnp.einsum('bqkgl,blkd->bqkgd', p_w, vw.astype(F32))
    g = gates.astype(F32)[..., None]
    o = g[:, :, 0] * o_c + g[:, :, 1] * o_s + g[:, :, 2] * o_w
    return o.reshape(b, lq, NSA_QW).astype(q.dtype)


def hybrid_mixer(x, norm_w, pos0, w_in_p, gdn_conv_w, gdn_a_log, gdn_dt_bias, gdn_norm_w, cmp_pe, cmp_w1, cmp_w2,
                 w_out_b, gdn_conv_buf, gdn_s0, past_cmp, past_sel, win_buf):
    b, l, d = x.shape
    x2 = x.reshape(b * l, d)
    proj = rms_mm(x2, norm_w, w_in_p)[:, :HYB_IN].reshape(b, l, HYB_IN)
    (gq, gk, gv, gz, ga, gb, nq, ck, cv, sk, sv, wk, wv, ng) = split_cols(proj, HYB_SPLITS)
    qkv_ext = jnp.concatenate([gdn_conv_buf, jnp.concatenate([gq, gk, gv], axis=-1)], axis=1)
    new_conv = qkv_ext[:, -(GDN_CONV - 1):]
    qkv = jax.nn.silu(causal_dwconv(qkv_ext, gdn_conv_w))
    q_a, k_a, v_a = split_cols(qkv, (GDN_QK, GDN_QK, GDN_VW))
    q_a = l2norm(q_a.reshape(b, l, GDN_HEADS, GDN_DK)) * GDN_DK ** -0.5
    k_a = l2norm(k_a.reshape(b, l, GDN_HEADS, GDN_DK))
    v_a = v_a.reshape(b, l, GDN_HEADS, GDN_DV)
    beta = jax.nn.sigmoid(gb.astype(F32))
    g_log = -jnp.exp(gdn_a_log.astype(F32)) * jax.nn.softplus(ga.astype(F32) + gdn_dt_bias.astype(F32))
    o_a, s_new = gdn_chunked(q_a, k_a, v_a, g_log, beta, gdn_s0)
    o_a = rmsnorm(o_a, gdn_norm_w) * jax.nn.silu(gz.reshape(b, l, GDN_HEADS, GDN_DV))
    gdn_out = o_a.reshape(b, l, GDN_VW)
    pos = pos0 + jnp.arange(l)
    q_b = rope(nq.reshape(b, l, NSA_HEADS, HEAD_DIM), pos)
    kvr = lambda a: a.reshape(b, l, NSA_KV_HEADS, HEAD_DIM)
    cmp_rows = jnp.stack([rope(kvr(ck), pos), kvr(cv)], axis=2)
    sel_rows = jnp.stack([rope(kvr(sk), pos), kvr(sv)], axis=2)
    win_rows = jnp.stack([rope(kvr(wk), pos), kvr(wv)], axis=2)
    gates = jax.nn.sigmoid(ng.astype(F32)).reshape(b, l, 3, NSA_KV_HEADS, NSA_GROUP)
    cmp_all = cmp_rows if past_cmp is None else jnp.concatenate([past_cmp, cmp_rows], axis=1)
    sel_all = sel_rows if past_sel is None else jnp.concatenate([past_sel, sel_rows], axis=1)
    kc, vc, c_end = compress_blocks(cmp_all, cmp_pe, cmp_w1, cmp_w2)
    ks_blk, vs_blk = selection_blocks(sel_all)
    if win_buf is None:
        qb_len = min(Q_BLOCK, l)
        n_qb = l // qb_len
        kw_pad = jnp.pad(win_rows, ((0, 0), (WINDOW, 0), (0, 0), (0, 0), (0, 0)))

        def query_block(i):
            s0 = i * qb_len
            q_blk = lax.dynamic_slice_in_dim(q_b, s0, qb_len, axis=1)
            g_blk = lax.dynamic_slice_in_dim(gates, s0, qb_len, axis=1)
            kw_blk = lax.dynamic_slice_in_dim(kw_pad, s0, WINDOW + qb_len, axis=1)
            q_pos = pos0 + s0 + jnp.arange(qb_len)
            kw_pos = pos0 + s0 - WINDOW + jnp.arange(WINDOW + qb_len)
            return nsa_core(q_blk, q_pos, kc, vc, c_end, ks_blk, vs_blk,
                            kw_blk[:, :, 0], kw_blk[:, :, 1], kw_pos, g_blk)

        nsa_out = lax.map(query_block, jnp.arange(n_qb))
        nsa_out = jnp.moveaxis(nsa_out, 0, 1).reshape(b, l, NSA_QW)
        new_win = win_rows[:, -min(WINDOW, l):]
    else:
        w_buf = win_buf.shape[1]
        win_all = jnp.concatenate([win_buf, win_rows], axis=1)
        kw_pos = pos0 - w_buf + jnp.arange(w_buf + l)
        nsa_out = nsa_core(q_b, pos, kc, vc, c_end, ks_blk, vs_blk,
                           win_all[:, :, 0], win_all[:, :, 1], kw_pos, gates)
        new_win = win_all[:, -w_buf:]
    mix = jnp.concatenate([gdn_out, nsa_out], axis=-1).reshape(b * l, -1).astype(BF16)
    out = mm_resid(mix, w_out_b, x2).reshape(b, l, d)
    return out, cmp_rows, sel_rows, new_win, new_conv, s_new


def ssm_mixer(x, norm_w, w_in_p, conv_w, conv_b, a_log, dt_bias, d_skip, gn_w, w_out_b, conv_buf, h0):
    b, l, d = x.shape
    x2 = x.reshape(b * l, d)
    proj = rms_mm(x2, norm_w, w_in_p)[:, :SSM_IN].reshape(b, l, SSM_IN)
    z, xbc, dt = split_cols(proj, (SSM_D_INNER, SSM_CONV_DIM, SSM_HEADS))
    xbc_ext = jnp.concatenate([conv_buf, xbc], axis=1)
    new_conv = xbc_ext[:, -(SSM_CONV - 1):]
    xbc = jax.nn.silu(causal_dwconv(xbc_ext, conv_w) + conv_b)
    xs, bm, cm = split_cols(xbc, (SSM_D_INNER, SSM_BC, SSM_BC))
    xs = xs.reshape(b, l, SSM_HEADS, SSM_HEAD_DIM)
    bm = bm.reshape(b, l, SSM_GROUPS, SSM_STATE)
    cm = cm.reshape(b, l, SSM_GROUPS, SSM_STATE)
    dt = jax.nn.softplus(dt.astype(F32) + dt_bias.astype(F32))
    a_neg = -jnp.exp(a_log.astype(F32))
    y, h_new = ssd_chunked(xs, dt, a_neg, bm, cm, h0)
    y = y + d_skip.astype(F32)[:, None] * xs.astype(F32)
    y = y.reshape(b, l, SSM_D_INNER) * jax.nn.silu(z.astype(F32))
    y = rmsnorm(y.reshape(b, l, SSM_GROUPS, SSM_D_INNER // SSM_GROUPS), gn_w.reshape(SSM_GROUPS, -1))
    y = y.reshape(b * l, SSM_D_INNER).astype(BF16)
    return mm_resid(y, w_out_b, x2).reshape(b, l, d), new_conv, h_new


def gather_pages(pool, page_table):
    g = pool[page_table]
    return g.reshape((g.shape[0], g.shape[1] * g.shape[2]) + g.shape[3:])


def _pad_cols(w, mult):
    n = w.shape[-1]
    return jnp.pad(w, ((0, 0), (0, (-n) % mult)))


def kernel(x_prompt, x_sample, cache_cmp_kv, cache_sel_kv, cache_win_kv, state_gdn_conv, state_gdn, state_ssm_conv, state_ssm, page_table, hyb_norm_mix, hyb_w_in, hyb_gdn_conv_w, hyb_gdn_a_log, hyb_gdn_dt_bias, hyb_gdn_norm_w, hyb_cmp_pe, hyb_cmp_w1, hyb_cmp_w2, hyb_w_out, hyb_norm_ffn, ffn_w_gate, ffn_w_up, ffn_w_down, ssm_norm_mix, ssm_w_in, ssm_conv_w, ssm_conv_b, ssm_a_log, ssm_dt_bias, ssm_d_skip, ssm_norm_w, ssm_w_out, ssm_norm_ffn, moe_router, moe_w_gate, moe_w_up, moe_w_down, final_norm):
    hp, hs = x_prompt, x_sample
    bp, lp, d = hp.shape
    bs, ls, _ = hs.shape

    w_in_p = _pad_cols(hyb_w_in[0].astype(BF16), 512)
    w_out_b = hyb_w_out[0].astype(BF16)
    hw = (w_in_p, hyb_gdn_conv_w[0], hyb_gdn_a_log[0], hyb_gdn_dt_bias[0], hyb_gdn_norm_w[0],
          hyb_cmp_pe[0], hyb_cmp_w1[0], hyb_cmp_w2[0], w_out_b)
    zero_conv = jnp.zeros((bp, GDN_CONV - 1, GDN_CONV_DIM), hp.dtype)
    zero_s = jnp.zeros((bp, GDN_HEADS, GDN_DK, GDN_DV), hp.dtype)
    hp, cmp_p, sel_p, win_p, gconv_p, gst_p = hybrid_mixer(
        hp, hyb_norm_mix[0], 0, *hw, zero_conv, zero_s, None, None, None)
    past_cmp = gather_pages(cache_cmp_kv[0], page_table)
    past_sel = gather_pages(cache_sel_kv[0], page_table)
    hs, cmp_s, sel_s, win_s, gconv_s, gst_s = hybrid_mixer(
        hs, hyb_norm_mix[0], PAST_LEN, *hw, state_gdn_conv[0], state_gdn[0], past_cmp, past_sel, cache_win_kv[0])

    wg, wu, wd = ffn_w_gate[0].astype(BF16), ffn_w_up[0].astype(BF16), ffn_w_down[0].astype(BF16)

    def dense_ffn(x):
        b, l, _ = x.shape
        x2 = x.reshape(b * l, d)
        return mm_resid(rms_glu(x2, hyb_norm_ffn[0], wg, wu), wd, x2).reshape(b, l, d)

    hp = dense_ffn(hp)
    hs = dense_ffn(hs)

    sw_in_p = _pad_cols(ssm_w_in[0].astype(BF16), 512)
    sw_out_b = ssm_w_out[0].astype(BF16)
    sw = (sw_in_p, ssm_conv_w[0], ssm_conv_b[0], ssm_a_log[0], ssm_dt_bias[0], ssm_d_skip[0], ssm_norm_w[0], sw_out_b)
    zero_conv = jnp.zeros((bp, SSM_CONV - 1, SSM_CONV_DIM), hp.dtype)
    zero_h = jnp.zeros((bp, SSM_HEADS, SSM_HEAD_DIM, SSM_STATE), hp.dtype)
    hp, sconv_p, sst_p = ssm_mixer(hp, ssm_norm_mix[0], *sw, zero_conv, zero_h)
    hs, sconv_s, sst_s = ssm_mixer(hs, ssm_norm_mix[0], *sw, state_ssm_conv[0], state_ssm[0])

    tok = jnp.concatenate([hp.reshape(bp * lp, d), hs.reshape(bs * ls, d)], axis=0)
    n_tok = tok.shape[0]
    tok_pad = jnp.pad(tok, ((0, (-n_tok) % MOE_TM), (0, 0)))
    moe = moe_ffn(tok_pad, ssm_norm_ffn[0], moe_router[0], moe_w_gate[0].astype(BF16),
                  moe_w_up[0].astype(BF16), moe_w_down[0].astype(BF16))
    y = rms(tok_pad + moe, final_norm)
    y_prompt = y[:bp * lp].reshape(bp, lp, d)
    y_sample = y[bp * lp:n_tok].reshape(bs, ls, d)
    st = lambda a: a[None]
    return (y_prompt, y_sample,
            st(cmp_p), st(cmp_s), st(sel_p), st(sel_s), st(win_p), st(win_s),
            st(gconv_p), st(gconv_s), st(gst_p), st(gst_s),
            st(sconv_p), st(sconv_s), st(sst_p), st(sst_s))
```

```python
import functools
import math

import jax
import jax.numpy as jnp
from jax import lax
from jax.experimental import pallas as pl
from jax.experimental.pallas import tpu as pltpu

D_MODEL = 2048
PAST_LEN = 16384
HEAD_DIM = 128
ROPE_THETA = 10000.0
RMS_EPS = 1e-6

GDN_HEADS = D_MODEL // 256
GDN_DK = 128
GDN_DV = 128
GDN_CONV = 4
GDN_CHUNK = 64
GDN_QK = GDN_HEADS * GDN_DK
GDN_VW = GDN_HEADS * GDN_DV
GDN_CONV_DIM = 2 * GDN_QK + GDN_VW

NSA_HEADS = D_MODEL // 256
NSA_KV_HEADS = 2
NSA_GROUP = NSA_HEADS // NSA_KV_HEADS
NSA_QW = NSA_HEADS * HEAD_DIM
NSA_KVW = NSA_KV_HEADS * HEAD_DIM
CMP_BLOCK = 32
CMP_STRIDE = 16
SEL_BLOCK = 64
SEL_TOPK = 16
WINDOW = 512
Q_BLOCK = 128
FORCE_BONUS = 1e4

HYB_SPLITS = (GDN_QK, GDN_QK, GDN_VW, GDN_VW, GDN_HEADS, GDN_HEADS,
              NSA_QW, NSA_KVW, NSA_KVW, NSA_KVW, NSA_KVW, NSA_KVW, NSA_KVW, 3 * NSA_HEADS)
HYB_IN = sum(HYB_SPLITS)

SSM_D_INNER = 2 * D_MODEL
SSM_HEAD_DIM = 64
SSM_HEADS = SSM_D_INNER // SSM_HEAD_DIM
SSM_GROUPS = 8
SSM_HPG = SSM_HEADS // SSM_GROUPS
SSM_STATE = 128
SSM_CONV = 4
SSM_CHUNK = 128
SSM_BC = SSM_GROUPS * SSM_STATE
SSM_CONV_DIM = SSM_D_INNER + 2 * SSM_BC
SSM_IN = SSM_D_INNER + SSM_CONV_DIM + SSM_HEADS

N_EXPERTS = 8
TOP_K = 2

VMEM_LIMIT_BYTES = 56 * 1024 * 1024
LANES = 128

F32 = jnp.float32
BF16 = jnp.bfloat16


def _params(*sem):
    return pltpu.CompilerParams(dimension_semantics=sem, vmem_limit_bytes=VMEM_LIMIT_BYTES)


def _pick(n, prefs):
    for p in prefs:
        if n % p == 0:
            return p
    return n


def _rms_to_bf16(x, g):
    ms = jnp.mean(x * x, axis=-1, keepdims=True)
    return (x * lax.rsqrt(ms + RMS_EPS) * g).astype(BF16)


def _rms_mm_kernel(x_ref, g_ref, w_ref, o_ref, a_scr):
    @pl.when(pl.program_id(1) == 0)
    def _():
        a_scr[...] = _rms_to_bf16(x_ref[...], g_ref[...])

    o_ref[...] = jnp.dot(a_scr[...], w_ref[...], preferred_element_type=F32).astype(o_ref.dtype)


def rms_mm(x, gain, w, out_dtype=F32):
    m, k = x.shape
    n = w.shape[1]
    tm = _pick(m, (512, 256, 128))
    tn = _pick(n, (512, 256, 128))
    return pl.pallas_call(
        _rms_mm_kernel,
        grid=(m // tm, n // tn),
        in_specs=[pl.BlockSpec((tm, k), lambda i, j: (i, 0)),
                  pl.BlockSpec((1, k), lambda i, j: (0, 0)),
                  pl.BlockSpec((k, tn), lambda i, j: (0, j))],
        out_specs=pl.BlockSpec((tm, tn), lambda i, j: (i, j)),
        out_shape=jax.ShapeDtypeStruct((m, n), out_dtype),
        scratch_shapes=[pltpu.VMEM((tm, k), BF16)],
        compiler_params=_params("parallel", "arbitrary"),
        name="rms_mm",
    )(x, gain.reshape(1, k).astype(F32), w)


def _rms_glu_kernel(x_ref, g_ref, wg_ref, wu_ref, o_ref, a_scr):
    @pl.when(pl.program_id(1) == 0)
    def _():
        a_scr[...] = _rms_to_bf16(x_ref[...], g_ref[...])

    a = a_scr[...]
    gt = jnp.dot(a, wg_ref[...], preferred_element_type=F32)
    up = jnp.dot(a, wu_ref[...], preferred_element_type=F32)
    o_ref[...] = (gt * jax.nn.sigmoid(gt) * up).astype(o_ref.dtype)


def rms_glu(x, gain, wg, wu):
    m, k = x.shape
    n = wg.shape[1]
    tm = _pick(m, (512, 256, 128))
    tn = _pick(n, (512, 256, 128))
    return pl.pallas_call(
        _rms_glu_kernel,
        grid=(m // tm, n // tn),
        in_specs=[pl.BlockSpec((tm, k), lambda i, j: (i, 0)),
                  pl.BlockSpec((1, k), lambda i, j: (0, 0)),
                  pl.BlockSpec((k, tn), lambda i, j: (0, j)),
                  pl.BlockSpec((k, tn), lambda i, j: (0, j))],
        out_specs=pl.BlockSpec((tm, tn), lambda i, j: (i, j)),
        out_shape=jax.ShapeDtypeStruct((m, n), BF16),
        scratch_shapes=[pltpu.VMEM((tm, k), BF16)],
        compiler_params=_params("parallel", "arbitrary"),
        name="rms_glu",
    )(x, gain.reshape(1, k).astype(F32), wg, wu)


def _mm_resid_kernel(a_ref, w_ref, r_ref, o_ref, acc_ref, *, nk):
    kk = pl.program_id(2)

    @pl.when(kk == 0)
    def _():
        acc_ref[...] = r_ref[...]

    acc_ref[...] += jnp.dot(a_ref[...], w_ref[...], preferred_element_type=F32)

    @pl.when(kk == nk - 1)
    def _():
        o_ref[...] = acc_ref[...]


def mm_resid(a, w, resid):
    m, k = a.shape
    n = w.shape[1]
    tm = _pick(m, (512, 256, 128))
    tn = _pick(n, (512, 256, 128))
    tk = _pick(k, (2048, 1792, 1408, 1024, 512))
    nk = k // tk
    return pl.pallas_call(
        functools.partial(_mm_resid_kernel, nk=nk),
        grid=(m // tm, n // tn, nk),
        in_specs=[pl.BlockSpec((tm, tk), lambda i, j, kk: (i, kk)),
                  pl.BlockSpec((tk, tn), lambda i, j, kk: (kk, j)),
                  pl.BlockSpec((tm, tn), lambda i, j, kk: (i, j))],
        out_specs=pl.BlockSpec((tm, tn), lambda i, j, kk: (i, j)),
        out_shape=jax.ShapeDtypeStruct((m, n), F32),
        scratch_shapes=[pltpu.VMEM((tm, tn), F32)],
        compiler_params=_params("parallel", "parallel", "arbitrary"),
        name="mm_resid",
    )(a, w, resid)


def _rms_kernel(x_ref, g_ref, o_ref):
    x = x_ref[...]
    ms = jnp.mean(x * x, axis=-1, keepdims=True)
    o_ref[...] = (x * lax.rsqrt(ms + RMS_EPS) * g_ref[...]).astype(o_ref.dtype)


def rms(x, gain, out_dtype=F32):
    m, k = x.shape
    tm = _pick(m, (512, 256, 128))
    return pl.pallas_call(
        _rms_kernel,
        grid=(m // tm,),
        in_specs=[pl.BlockSpec((tm, k), lambda i: (i, 0)),
                  pl.BlockSpec((1, k), lambda i: (0, 0))],
        out_specs=pl.BlockSpec((tm, k), lambda i: (i, 0)),
        out_shape=jax.ShapeDtypeStruct((m, k), out_dtype),
        compiler_params=_params("parallel"),
        name="rms",
    )(x, gain.reshape(1, k).astype(F32))


def _rms_router_kernel(x_ref, g_ref, wh_ref, wl_ref, h_ref, lg_ref):
    x = x_ref[...]
    ms = jnp.mean(x * x, axis=-1, keepdims=True)
    h = x * lax.rsqrt(ms + RMS_EPS) * g_ref[...]
    hh = h.astype(BF16)
    hl = (h - hh.astype(F32)).astype(BF16)
    h_ref[...] = hh
    wh = wh_ref[...]
    wl = wl_ref[...]
    lg = jnp.dot(hh, wh, preferred_element_type=F32)
    lg += jnp.dot(hl, wh, preferred_element_type=F32)
    lg += jnp.dot(hh, wl, preferred_element_type=F32)
    lg_ref[...] = lg


def rms_router(x, gain, router):
    m, k = x.shape
    e = router.shape[1]
    rp = jnp.pad(router.astype(F32), ((0, 0), (0, LANES - e)))
    rh = rp.astype(BF16)
    rl = (rp - rh.astype(F32)).astype(BF16)
    tm = _pick(m, (512, 256, 128, 32))
    h, lg = pl.pallas_call(
        _rms_router_kernel,
        grid=(m // tm,),
        in_specs=[pl.BlockSpec((tm, k), lambda i: (i, 0)),
                  pl.BlockSpec((1, k), lambda i: (0, 0)),
                  pl.BlockSpec((k, LANES), lambda i: (0, 0)),
                  pl.BlockSpec((k, LANES), lambda i: (0, 0))],
        out_specs=[pl.BlockSpec((tm, k), lambda i: (i, 0)),
                   pl.BlockSpec((tm, LANES), lambda i: (i, 0))],
        out_shape=[jax.ShapeDtypeStruct((m, k), BF16), jax.ShapeDtypeStruct((m, LANES), F32)],
        compiler_params=_params("parallel"),
        name="rms_router",
    )(x, gain.reshape(1, k).astype(F32), rh, rl)
    return h, lg[:, :e]


MOE_TM = 256


def _moe_glu_kernel(te_ref, tv_ref, a_ref, wg_ref, wu_ref, o_ref):
    i = pl.program_id(1)

    @pl.when(tv_ref[i] != 0)
    def _():
        a = a_ref[...]
        gt = jnp.dot(a, wg_ref[...], preferred_element_type=F32)
        up = jnp.dot(a, wu_ref[...], preferred_element_type=F32)
        o_ref[...] = (gt * jax.nn.sigmoid(gt) * up).astype(o_ref.dtype)

    @pl.when(tv_ref[i] == 0)
    def _():
        o_ref[...] = jnp.zeros_like(o_ref)


def _moe_down_kernel(te_ref, tv_ref, a_ref, w_ref, o_ref):
    i = pl.program_id(1)

    @pl.when(tv_ref[i] != 0)
    def _():
        o_ref[...] = jnp.dot(a_ref[...], w_ref[...], preferred_element_type=F32)

    @pl.when(tv_ref[i] == 0)
    def _():
        o_ref[...] = jnp.zeros_like(o_ref)


def moe_experts(a_sorted, tile_expert, tile_valid, wg, wu, wd):
    r, d = a_sorted.shape
    f = wg.shape[2]
    tm = MOE_TM
    tf = _pick(f, (1024, 512))
    act = pl.pallas_call(
        _moe_glu_kernel,
        grid_spec=pltpu.PrefetchScalarGridSpec(
            num_scalar_prefetch=2,
            grid=(f // tf, r // tm),
            in_specs=[pl.BlockSpec((tm, d), lambda j, i, te, tv: (i, 0)),
                      pl.BlockSpec((None, d, tf), lambda j, i, te, tv: (te[i], 0, j)),
                      pl.BlockSpec((None, d, tf), lambda j, i, te, tv: (te[i], 0, j))],
            out_specs=pl.BlockSpec((tm, tf), lambda j, i, te, tv: (i, j)),
        ),
        out_shape=jax.ShapeDtypeStruct((r, f), BF16),
        compiler_params=_params("arbitrary", "arbitrary"),
        name="moe_glu",
    )(tile_expert, tile_valid, a_sorted, wg, wu)
    tn = _pick(d, (1024, 512))
    return pl.pallas_call(
        _moe_down_kernel,
        grid_spec=pltpu.PrefetchScalarGridSpec(
            num_scalar_prefetch=2,
            grid=(d // tn, r // tm),
            in_specs=[pl.BlockSpec((tm, f), lambda j, i, te, tv: (i, 0)),
                      pl.BlockSpec((None, f, tn), lambda j, i, te, tv: (te[i], 0, j))],
            out_specs=pl.BlockSpec((tm, tn), lambda j, i, te, tv: (i, j)),
        ),
        out_shape=jax.ShapeDtypeStruct((r, d), F32),
        compiler_params=_params("arbitrary", "arbitrary"),
        name="moe_down",
    )(tile_expert, tile_valid, act, wd)


def moe_ffn(x, gain, router, wg, wu, wd):
    t, d = x.shape
    tm = MOE_TM
    h, logits = rms_router(x, gain, router)
    top_v, top_i = lax.top_k(logits, TOP_K)
    top_w = jax.nn.softmax(top_v, axis=-1)
    n_asg = t * TOP_K
    n_rows = (-(-n_asg // tm) + N_EXPERTS) * tm
    flat_e = top_i.reshape(-1).astype(jnp.int32)
    order = jnp.argsort(flat_e, stable=True).astype(jnp.int32)
    sorted_e = flat_e[order]
    counts = jnp.sum(flat_e[:, None] == jnp.arange(N_EXPERTS, dtype=jnp.int32)[None, :], axis=0).astype(jnp.int32)
    padded = ((counts + tm - 1) // tm) * tm
    pad_end = jnp.cumsum(padded)
    pad_start = pad_end - padded
    start = jnp.cumsum(counts) - counts
    dest = pad_start[sorted_e] + (jnp.arange(n_asg, dtype=jnp.int32) - start[sorted_e])
    row_src = jnp.zeros((n_rows,), jnp.int32).at[dest].set(order // TOP_K)
    pos = jnp.zeros((n_asg,), jnp.int32).at[order].set(dest)
    tile_start = jnp.arange(n_rows // tm, dtype=jnp.int32) * tm
    tile_valid = (tile_start < pad_end[-1]).astype(jnp.int32)
    tile_expert = jnp.minimum(jnp.searchsorted(pad_end, tile_start, side="right"), N_EXPERTS - 1).astype(jnp.int32)
    last_e = tile_expert[jnp.maximum(pad_end[-1] // tm - 1, 0)]
    tile_expert = jnp.where(tile_valid != 0, tile_expert, last_e)
    a_sorted = jnp.take(h, row_src, axis=0)
    y = moe_experts(a_sorted, tile_expert, tile_valid, wg, wu, wd)
    yk = jnp.take(y, pos, axis=0).reshape(t, TOP_K, d)
    return jnp.sum(yk * top_w[..., None], axis=1)


NEG_BIG = -1e30
NSA_TQ = 128
SEL_CHUNK = 256
CMP_HIDDEN = 256


def _dot_nt(a, b):
    return lax.dot_general(a, b, (((1,), (1,)), ((), ())), preferred_element_type=F32)


def _cmp_mlp_kernel(x_ref, pe_ref, w1_ref, w2_ref, o_ref, hi_scr, *, n_seg):
    lo = jnp.zeros((n_seg, CMP_HIDDEN), F32)
    hi = jnp.zeros((n_seg, CMP_HIDDEN), F32)
    for r in range(CMP_STRIDE):
        xr = x_ref[pl.ds(r, n_seg, stride=CMP_STRIDE), :]
        a_lo = (xr + pe_ref[r:r + 1, :]).astype(BF16)
        a_hi = (xr + pe_ref[CMP_STRIDE + r:CMP_STRIDE + r + 1, :]).astype(BF16)
        lo += jnp.dot(a_lo, w1_ref[r * HEAD_DIM:(r + 1) * HEAD_DIM, :], preferred_element_type=F32)
        hi += jnp.dot(a_hi, w1_ref[(CMP_STRIDE + r) * HEAD_DIM:(CMP_STRIDE + r + 1) * HEAD_DIM, :],
                      preferred_element_type=F32)
    hi_scr[0:n_seg, :] = hi
    hi_scr[n_seg:n_seg + 8, :] = jnp.zeros((8, CMP_HIDDEN), F32)
    pre = lo + hi_scr[pl.ds(1, n_seg), :]
    hid = pre * jax.nn.sigmoid(pre)
    o_ref[...] = jnp.dot(hid.astype(BF16), w2_ref[...], preferred_element_type=F32)


def cmp_mlp_prompt(rows2d, pe, w1b, w2b):
    b, l, _ = rows2d.shape
    n_seg = l // CMP_STRIDE
    return pl.pallas_call(
        functools.partial(_cmp_mlp_kernel, n_seg=n_seg),
        grid=(b, 2, NSA_KV_HEADS),
        in_specs=[pl.BlockSpec((None, l, HEAD_DIM), lambda i, s, k: (i, 0, s * NSA_KV_HEADS + k)),
                  pl.BlockSpec((None, CMP_BLOCK, HEAD_DIM), lambda i, s, k: (s, 0, 0)),
                  pl.BlockSpec((None, CMP_BLOCK * HEAD_DIM, CMP_HIDDEN), lambda i, s, k: (s, 0, 0)),
                  pl.BlockSpec((None, CMP_HIDDEN, HEAD_DIM), lambda i, s, k: (s, 0, 0))],
        out_specs=pl.BlockSpec((None, None, None, n_seg, HEAD_DIM), lambda i, s, k: (i, s, k, 0, 0)),
        out_shape=jax.ShapeDtypeStruct((b, 2, NSA_KV_HEADS, n_seg, HEAD_DIM), F32),
        scratch_shapes=[pltpu.VMEM((n_seg + 8, CMP_HIDDEN), F32)],
        compiler_params=_params("parallel", "parallel", "parallel"),
        name="cmp_mlp_prompt",
    )(rows2d, pe, w1b, w2b)


def _nsa_prompt_kernel(q_ref, kc_ref, vc_ref, ks_ref, vs_ref, kw_ref, vw_ref, g_ref, o_ref,
                       m_scr, l_scr, acc_scr, o_scr, sc_scr, *, tq, n_cmp, n_sel):
    t = pl.program_id(2)
    ng = NSA_GROUP
    rows = ng * tq
    q = q_ref[...].reshape(rows, HEAD_DIM)
    qpos = t * tq + lax.broadcasted_iota(jnp.int32, (tq, 1), 0)
    gate = g_ref[...]

    def tile_rows(x):
        return jnp.concatenate([x] * ng, axis=0)

    def gate_col(branch):
        return jnp.concatenate([gate[:, branch * ng + h:branch * ng + h + 1] for h in range(ng)], axis=0)

    def reset():
        m_scr[...] = jnp.full(m_scr.shape, NEG_BIG, F32)
        l_scr[...] = jnp.zeros(l_scr.shape, F32)
        acc_scr[...] = jnp.zeros(acc_scr.shape, F32)

    def update(k, v, mask):
        s = _dot_nt(q, k)
        mask4 = tile_rows(mask)
        sm = jnp.where(mask4, s, NEG_BIG)
        m_prev = m_scr[:, :1]
        m_new = jnp.maximum(m_prev, jnp.max(sm, axis=1, keepdims=True))
        e = jnp.where(mask4, jnp.exp(sm - m_new), 0.0)
        alpha = jnp.exp(m_prev - m_new)
        l_new = alpha * l_scr[:, :1] + jnp.sum(e, axis=1, keepdims=True)
        m_scr[...] = jnp.broadcast_to(m_new, m_scr.shape)
        l_scr[...] = jnp.broadcast_to(l_new, l_scr.shape)
        acc_scr[...] = alpha * acc_scr[...] + jnp.dot(e.astype(BF16), v, preferred_element_type=F32)

    def finish():
        den = l_scr[:, :1]
        return acc_scr[...] / jnp.where(den > 0, den, 1.0)

    n_cp = kc_ref.shape[0]
    kc = kc_ref[...].astype(BF16)
    vc = vc_ref[...].astype(BF16)
    s = _dot_nt(q, kc)
    cidx = lax.broadcasted_iota(jnp.int32, (1, n_cp), 1)
    mask_c = tile_rows((cidx * CMP_STRIDE + (CMP_BLOCK - 1) <= qpos) & (cidx < n_cmp))
    sm = jnp.where(mask_c, s, NEG_BIG)
    e = jnp.where(mask_c, jnp.exp(sm - jnp.max(sm, axis=1, keepdims=True)), 0.0)
    den = jnp.sum(e, axis=1, keepdims=True)
    p = e / jnp.where(den > 0, den, 1.0)
    o_scr[...] = gate_col(0) * jnp.dot(p.astype(BF16), vc, preferred_element_type=F32)
    p_sum = p[0:tq]
    for h in range(1, ng):
        p_sum = p_sum + p[h * tq:(h + 1) * tq]
    p_hi = p_sum.astype(BF16)
    p_lo = (p_sum - p_hi.astype(F32)).astype(BF16)
    cs = lax.broadcasted_iota(jnp.int32, (n_cp, LANES), 0) * CMP_STRIDE
    ss = lax.broadcasted_iota(jnp.int32, (n_cp, LANES), 1) * SEL_BLOCK
    overlap = ((cs < ss + SEL_BLOCK) & (cs + CMP_BLOCK > ss) & (cs < n_cmp * CMP_STRIDE)).astype(BF16)
    imp = jnp.dot(p_hi, overlap, preferred_element_type=F32) + jnp.dot(p_lo, overlap, preferred_element_type=F32)

    blk = lax.broadcasted_iota(jnp.int32, (tq, LANES), 1)
    cur = qpos // SEL_BLOCK
    forced = (blk == 0) | (blk == cur) | (blk == cur - 1)
    score = jnp.where((blk <= cur) & (blk < n_sel), imp + jnp.where(forced, FORCE_BONUS, 0.0), NEG_BIG)
    n_sp = sc_scr.shape[0]
    s_t = score.T[0:n_sp]
    sc_scr[...] = s_t
    jidx = lax.broadcasted_iota(jnp.int32, (n_sp, tq), 0)
    rank = jnp.zeros((n_sp, tq), jnp.int32)
    for i in range(n_sel):
        si = sc_scr[i:i + 1, :]
        rank += ((si > s_t) | ((si == s_t) & (i < jidx))).astype(jnp.int32)
    sel_t = ((rank < SEL_TOPK) & (s_t > 0.5 * NEG_BIG)).astype(F32)
    if n_sp < LANES:
        sel_t = jnp.concatenate([sel_t, jnp.zeros((LANES - n_sp, tq), F32)], axis=0)
    sel = sel_t.T.astype(BF16)

    reset()

    def sel_step(c, carry):
        start = pl.multiple_of(c * SEL_CHUNK, SEL_CHUNK)
        k = ks_ref[pl.ds(start, SEL_CHUNK), :]
        v = vs_ref[pl.ds(start, SEL_CHUNK), :]
        jrow = lax.broadcasted_iota(jnp.int32, (LANES, SEL_CHUNK), 0)
        kblk = (start + lax.broadcasted_iota(jnp.int32, (LANES, SEL_CHUNK), 1)) // SEL_BLOCK
        expand = (jrow == kblk).astype(BF16)
        chosen = jnp.dot(sel, expand, preferred_element_type=F32) > 0.5
        kpos = start + lax.broadcasted_iota(jnp.int32, (1, SEL_CHUNK), 1)
        update(k, v, chosen & (kpos <= qpos))
        return carry

    lax.fori_loop(0, (t * tq + tq + SEL_CHUNK - 1) // SEL_CHUNK, sel_step, 0)
    o_scr[...] += gate_col(1) * finish()

    reset()
    for i in range(WINDOW // tq + 1):
        cw = t - WINDOW // tq + i
        start = pl.multiple_of(jnp.maximum(cw, 0) * tq, tq)
        kpos = cw * tq + lax.broadcasted_iota(jnp.int32, (1, tq), 1)
        dist = qpos - kpos
        update(kw_ref[pl.ds(start, tq), :], vw_ref[pl.ds(start, tq), :],
               (dist >= 0) & (dist < WINDOW) & (kpos >= 0))
    out = o_scr[...] + gate_col(2) * finish()
    for h in range(ng):
        o_ref[:, h * HEAD_DIM:(h + 1) * HEAD_DIM] = out[h * tq:(h + 1) * tq]


def nsa_prompt(q_b, cmp_rows, sel_rows, win_rows, gates, pe, w1b, w2b):
    b, l = q_b.shape[:2]
    tq = NSA_TQ
    n_cmp = (l - CMP_BLOCK) // CMP_STRIDE + 1
    n_sel = -(-l // SEL_BLOCK)
    kvc = cmp_mlp_prompt(cmp_rows.reshape(b, l, 2 * NSA_KVW), pe, w1b, w2b)
    n_cp = kvc.shape[3]
    q = (q_b * HEAD_DIM ** -0.5).astype(BF16).transpose(0, 2, 1, 3)
    heads_first = lambda r, s: r[:, :, s].transpose(0, 2, 1, 3).astype(BF16)
    ks, vs = heads_first(sel_rows, 0), heads_first(sel_rows, 1)
    kw, vw = heads_first(win_rows, 0), heads_first(win_rows, 1)
    g = gates.transpose(0, 3, 1, 2, 4).reshape(b, NSA_KV_HEADS, l, 3 * NSA_GROUP)
    n_sp = -(-n_sel // 8) * 8
    rows = NSA_GROUP * tq
    kv_spec = pl.BlockSpec((None, None, l, HEAD_DIM), lambda i, k, t: (i, k, 0, 0))
    return pl.pallas_call(
        functools.partial(_nsa_prompt_kernel, tq=tq, n_cmp=n_cmp, n_sel=n_sel),
        grid=(b, NSA_KV_HEADS, l // tq),
        in_specs=[pl.BlockSpec((None, NSA_GROUP, tq, HEAD_DIM), lambda i, k, t: (i, k, t, 0)),
                  pl.BlockSpec((None, None, None, n_cp, HEAD_DIM), lambda i, k, t: (i, 0, k, 0, 0)),
                  pl.BlockSpec((None, None, None, n_cp, HEAD_DIM), lambda i, k, t: (i, 1, k, 0, 0)),
                  kv_spec, kv_spec, kv_spec, kv_spec,
                  pl.BlockSpec((None, None, tq, 3 * NSA_GROUP), lambda i, k, t: (i, k, t, 0))],
        out_specs=pl.BlockSpec((None, tq, NSA_GROUP * HEAD_DIM), lambda i, k, t: (i, t, k)),
        out_shape=jax.ShapeDtypeStruct((b, l, NSA_QW), F32),
        scratch_shapes=[pltpu.VMEM((rows, LANES), F32), pltpu.VMEM((rows, LANES), F32),
                        pltpu.VMEM((rows, HEAD_DIM), F32), pltpu.VMEM((rows, HEAD_DIM), F32),
                        pltpu.VMEM((n_sp, tq), F32)],
        compiler_params=_params("parallel", "parallel", "arbitrary"),
        name="nsa_prompt",
    )(q, kvc, kvc, ks, vs, kw, vw, g)


def _dot_exact01(a01, x):
    hi = x.astype(BF16)
    r = x - hi.astype(F32)
    mid = r.astype(BF16)
    lo = (r - mid.astype(F32)).astype(BF16)
    return (jnp.dot(a01, hi, preferred_element_type=F32) + jnp.dot(a01, mid, preferred_element_type=F32)
            + jnp.dot(a01, lo, preferred_element_type=F32))


def _gdn_kernel(q_ref, k_ref, v_ref, g_ref, b_ref, z_ref, nw_ref, s0_ref, o_ref, s_ref, *, c):
    @pl.when(pl.program_id(1) == 0)
    def _():
        s_ref[...] = s0_ref[...]

    ii = lax.broadcasted_iota(jnp.int32, (c, c), 0)
    jj = lax.broadcasted_iota(jnp.int32, (c, c), 1)
    incl = ii >= jj
    strict = ii > jj
    t01 = incl.astype(BF16)
    w01 = strict.astype(F32)
    g_all = g_ref[...]
    b_all = b_ref[...]
    nw = nw_ref[...]
    for h in range(GDN_HEADS):
        sl = slice(h * GDN_DV, (h + 1) * GDN_DV)
        q = q_ref[:, sl]
        k = k_ref[:, sl]
        v = v_ref[:, sl]
        g_b = jnp.broadcast_to(g_all[:, h:h + 1], (c, GDN_DK))
        b_b = jnp.broadcast_to(b_all[:, h:h + 1], (c, GDN_DK))
        gam = _dot_exact01(t01, g_b)
        diff = _dot_exact01(t01, g_b[:, :c] * w01)
        decay = jnp.exp(diff)
        kb = k * b_b
        kbf = k.astype(BF16)
        n_mat = jnp.where(strict, _dot_nt(kb.astype(BF16), kbf) * decay, 0.0)
        e_gam = jnp.exp(gam)
        y = jnp.concatenate([v * b_b, kb * e_gam], axis=1)
        y = y - jnp.dot(n_mat.astype(BF16), y.astype(BF16), preferred_element_type=F32)
        p = n_mat
        span = 2
        while span < c:
            pb = p.astype(BF16)
            p = jnp.dot(pb, pb, preferred_element_type=F32)
            y = y + jnp.dot(p.astype(BF16), y.astype(BF16), preferred_element_type=F32)
            span *= 2
        u = y[:, :GDN_DV]
        w = y[:, GDN_DV:]
        s = s_ref[h]
        sb = s.astype(BF16)
        v_new = u - jnp.dot(w.astype(BF16), sb, preferred_element_type=F32)
        vnb = v_new.astype(BF16)
        attn = jnp.where(incl, _dot_nt(q.astype(BF16), kbf) * decay, 0.0)
        o = (jnp.dot((q * e_gam).astype(BF16), sb, preferred_element_type=F32)
             + jnp.dot(attn.astype(BF16), vnb, preferred_element_type=F32))
        g_end = gam[c - 1:c, :]
        k_dec = (k * jnp.exp(g_end - gam)).astype(BF16)
        s_ref[h] = s * jnp.exp(g_end) + lax.dot_general(k_dec, vnb, (((0,), (0,)), ((), ())),
                                                         preferred_element_type=F32)
        on = o * lax.rsqrt(jnp.mean(o * o, axis=-1, keepdims=True) + RMS_EPS) * nw
        z = z_ref[:, sl]
        o_ref[:, sl] = on * (z * jax.nn.sigmoid(z))


def gdn_gated(q, k, v, g, beta, z, norm_w, s0):
    b, l, _ = q.shape
    c = GDN_CHUNK
    row = lambda w: pl.BlockSpec((None, c, w), lambda i, n: (i, n, 0))
    st = pl.BlockSpec((None, GDN_HEADS, GDN_DK, GDN_DV), lambda i, n: (i, 0, 0, 0))
    return pl.pallas_call(
        functools.partial(_gdn_kernel, c=c),
        grid=(b, l // c),
        in_specs=[row(GDN_QK), row(GDN_QK), row(GDN_VW), row(GDN_HEADS), row(GDN_HEADS), row(GDN_VW),
                  pl.BlockSpec((1, GDN_DV), lambda i, n: (0, 0)), st],
        out_specs=[row(GDN_VW), st],
        out_shape=[jax.ShapeDtypeStruct((b, l, GDN_VW), F32),
                   jax.ShapeDtypeStruct((b, GDN_HEADS, GDN_DK, GDN_DV), F32)],
        compiler_params=_params("parallel", "arbitrary"),
        name="gdn_chunked",
    )(q, k, v, g, beta, z, norm_w.reshape(1, GDN_DV).astype(F32), s0.astype(F32))


def rmsnorm(x, w):
    xf = x.astype(F32)
    y = xf * lax.rsqrt(jnp.mean(xf * xf, axis=-1, keepdims=True) + RMS_EPS)
    return (y * w.astype(F32)).astype(x.dtype)


def l2norm(x):
    xf = x.astype(F32)
    return (xf * lax.rsqrt(jnp.sum(xf * xf, axis=-1, keepdims=True) + 1e-6)).astype(x.dtype)


def rope(x, pos):
    half = x.shape[-1] // 2
    inv_freq = 1.0 / (ROPE_THETA ** (jnp.arange(half, dtype=F32) / half))
    ang = pos.astype(F32)[:, None] * inv_freq[None, :]
    cos = jnp.cos(ang)[None, :, None, :]
    sin = jnp.sin(ang)[None, :, None, :]
    xf = x.astype(F32)
    x1, x2 = xf[..., :half], xf[..., half:]
    return jnp.concatenate([x1 * cos - x2 * sin, x2 * cos + x1 * sin], axis=-1).astype(x.dtype)


def split_cols(a, sizes):
    out, s = [], 0
    for n in sizes:
        out.append(a[..., s:s + n])
        s += n
    return out


def causal_dwconv(x_ext, w):
    c = x_ext.shape[-1]
    return lax.conv_general_dilated(x_ext, w[:, None, :], window_strides=(1,), padding='VALID',
                                    dimension_numbers=('NWC', 'WIO', 'NWC'), feature_group_count=c)


def masked_softmax(s, mask):
    s = jnp.where(mask, s, -jnp.inf)
    m = jnp.max(s, axis=-1, keepdims=True)
    m = jnp.where(jnp.isfinite(m), m, 0.0)
    e = jnp.where(mask, jnp.exp(s - m), 0.0)
    den = jnp.sum(e, axis=-1, keepdims=True)
    return e / jnp.where(den > 0, den, 1.0)


def to_chunks(a, n, c, heads_first):
    b, l = a.shape[:2]
    a = jnp.pad(a.astype(F32), [(0, 0), (0, n * c - l)] + [(0, 0)] * (a.ndim - 2))
    a = a.reshape((b, n, c) + a.shape[2:])
    if heads_first:
        perm = (1, 0, 3, 2) + tuple(range(4, a.ndim))
    else:
        perm = (1, 0) + tuple(range(2, a.ndim))
    return a.transpose(perm)


def ssd_chunked(x, dt, a_neg, bmat, cmat, h0):
    b, l = x.shape[:2]
    c = min(SSM_CHUNK, l)
    n = -(-l // c)
    xc, dtc, bc, cc = (to_chunks(a, n, c, False) for a in (x, dt, bmat, cmat))
    incl = jnp.tri(c, dtype=bool)
    a_neg = a_neg.astype(F32)

    def step(h, inp):
        xi, dti, bi, ci = inp
        bn = dti.shape[0]
        gam = jnp.cumsum(dti * a_neg, axis=1)
        diff = gam[:, :, None, :] - gam[:, None, :, :]
        lmat = jnp.exp(jnp.where(incl[None, :, :, None], diff, -jnp.inf)).reshape(bn, c, c, SSM_GROUPS, SSM_HPG)
        xdt = (xi * dti[..., None]).reshape(bn, c, SSM_GROUPS, SSM_HPG, SSM_HEAD_DIM)
        cb = jnp.einsum('bign,bjgn->bijg', ci, bi)
        y_in = jnp.einsum('bijg,bijgh,bjghp->bighp', cb, lmat, xdt)
        h5 = h.reshape(bn, SSM_GROUPS, SSM_HPG, SSM_HEAD_DIM, SSM_STATE)
        dec_in = jnp.exp(gam).reshape(bn, c, SSM_GROUPS, SSM_HPG)
        y_st = jnp.einsum('bign,bghpn->bighp', ci, h5) * dec_in[..., None]
        g_end = gam[:, -1]
        dec_out = jnp.exp(g_end[:, None, :] - gam).reshape(bn, c, SSM_GROUPS, SSM_HPG)
        h_new = (h * jnp.exp(g_end)[:, :, None, None]
                 + jnp.einsum('bjgn,bjghp->bghpn', bi, xdt * dec_out[..., None]).reshape(h.shape))
        return h_new, (y_in + y_st).reshape(bn, c, SSM_HEADS, SSM_HEAD_DIM)

    h_fin, y = lax.scan(step, h0.astype(F32), (xc, dtc, bc, cc))
    y = y.transpose(1, 0, 2, 3, 4).reshape(b, n * c, SSM_HEADS, SSM_HEAD_DIM)[:, :l]
    return y, h_fin.astype(h0.dtype)


def compress_blocks(rows, pe, w1, w2):
    b, lk = rows.shape[:2]
    n_cmp = (lk - CMP_BLOCK) // CMP_STRIDE + 1
    per = CMP_BLOCK // CMP_STRIDE
    segs = rows[:, :(n_cmp + per - 1) * CMP_STRIDE].reshape(
        b, n_cmp + per - 1, CMP_STRIDE, 2, NSA_KV_HEADS, HEAD_DIM)
    blocks = jnp.concatenate([segs[:, i:i + n_cmp] for i in range(per)], axis=2)
    blocks = blocks + jnp.transpose(pe, (1, 0, 2))[:, :, None, :]
    flat = jnp.transpose(blocks, (0, 1, 3, 4, 2, 5)).reshape(b, n_cmp, 2, NSA_KV_HEADS, CMP_BLOCK * HEAD_DIM)
    hid = jax.nn.silu(jnp.einsum('bcskf,sfh->bcskh', flat, w1))
    out = jnp.einsum('bcskh,she->bcske', hid, w2)
    c_end = jnp.arange(n_cmp) * CMP_STRIDE + CMP_BLOCK - 1
    return out[:, :, 0], out[:, :, 1], c_end


def selection_blocks(rows):
    b, lk = rows.shape[:2]
    n_sel = -(-lk // SEL_BLOCK)
    rows = jnp.pad(rows, ((0, 0), (0, n_sel * SEL_BLOCK - lk), (0, 0), (0, 0), (0, 0)))
    blk = rows.reshape(b, n_sel, SEL_BLOCK, 2, NSA_KV_HEADS, HEAD_DIM).transpose(3, 0, 4, 1, 2, 5)
    return blk[0], blk[1]


def cmp_to_sel_map(n_cmp, n_sel):
    cs = jnp.arange(n_cmp)[:, None] * CMP_STRIDE
    ss = jnp.arange(n_sel)[None, :] * SEL_BLOCK
    return ((cs < ss + SEL_BLOCK) & (cs + CMP_BLOCK > ss)).astype(F32)


def nsa_core(q, q_pos, kc, vc, c_end, ks_blk, vs_blk, kw, vw, kw_pos, gates):
    b, lq = q.shape[:2]
    qg = q.astype(F32).reshape(b, lq, NSA_KV_HEADS, NSA_GROUP, HEAD_DIM) * HEAD_DIM ** -0.5
    s_c = jnp.einsum('bqkgd,bckd->bqkgc', qg, kc.astype(F32))
    m_c = c_end[None, :] <= q_pos[:, None]
    p_c = masked_softmax(s_c, m_c[None, :, None, None, :])
    o_c = jnp.einsum('bqkgc,bckd->bqkgd', p_c, vc.astype(F32))
    n_cmp, n_sel = kc.shape[1], ks_blk.shape[2]
    imp = jnp.einsum('bqkgc,cj->bqkj', p_c, cmp_to_sel_map(n_cmp, n_sel))
    blk = jnp.arange(n_sel)[None, :]
    cur = (q_pos // SEL_BLOCK)[:, None]
    forced = (blk == 0) | (blk == cur) | (blk == cur - 1)
    score = jnp.where((blk <= cur)[None, :, None, :],
                      imp + jnp.where(forced, FORCE_BONUS, 0.0)[None, :, None, :], -jnp.inf)
    n_top = min(SEL_TOPK, n_sel)
    top_s, top_i = lax.top_k(score, n_top)
    bi = jnp.arange(b)[:, None, None, None]
    ki = jnp.arange(NSA_KV_HEADS)[None, None, :, None]
    k_s = ks_blk[bi, ki, top_i].astype(F32).reshape(b, lq, NSA_KV_HEADS, n_top * SEL_BLOCK, HEAD_DIM)
    v_s = vs_blk[bi, ki, top_i].astype(F32).reshape(b, lq, NSA_KV_HEADS, n_top * SEL_BLOCK, HEAD_DIM)
    key_pos = top_i[..., None] * SEL_BLOCK + jnp.arange(SEL_BLOCK)
    m_s = jnp.isfinite(top_s)[..., None] & (key_pos <= q_pos[None, :, None, None, None])
    s_s = jnp.einsum('bqkgd,bqkjd->bqkgj', qg, k_s)
    p_s = masked_softmax(s_s, m_s.reshape(b, lq, NSA_KV_HEADS, 1, n_top * SEL_BLOCK))
    o_s = jnp.einsum('bqkgj,bqkjd->bqkgd', p_s, v_s)
    s_w = jnp.einsum('bqkgd,blkd->bqkgl', qg, kw.astype(F32))
    dist = q_pos[:, None] - kw_pos[None, :]
    m_w = (dist >= 0) & (dist < WINDOW) & (kw_pos[None, :] >= 0)
    p_w = masked_softmax(s_w, m_w[None, :, None, None, :])
    o_w = jnp.einsum('bqkgl,blkd->bqkgd', p_w, vw.astype(F32))
    g = gates.astype(F32)[..., None]
    o = g[:, :, 0] * o_c + g[:, :, 1] * o_s + g[:, :, 2] * o_w
    return o.reshape(b, lq, NSA_QW).astype(q.dtype)


def hybrid_mixer(x, norm_w, pos0, w_in_p, gdn_conv_w, gdn_a_log, gdn_dt_bias, gdn_norm_w, cmp_pe, cmp_w1, cmp_w2,
                 w_out_b, gdn_conv_buf, gdn_s0, past_cmp, past_sel, win_buf):
    b, l, d = x.shape
    x2 = x.reshape(b * l, d)
    proj = rms_mm(x2, norm_w, w_in_p)[:, :HYB_IN].reshape(b, l, HYB_IN)
    (gq, gk, gv, gz, ga, gb, nq, ck, cv, sk, sv, wk, wv, ng) = split_cols(proj, HYB_SPLITS)
    qkv_ext = jnp.concatenate([gdn_conv_buf, jnp.concatenate([gq, gk, gv], axis=-1)], axis=1)
    new_conv = qkv_ext[:, -(GDN_CONV - 1):]
    qkv = jax.nn.silu(causal_dwconv(qkv_ext, gdn_conv_w))
    q_a, k_a, v_a = split_cols(qkv, (GDN_QK, GDN_QK, GDN_VW))
    q_a = l2norm(q_a.reshape(b, l, GDN_HEADS, GDN_DK)) * GDN_DK ** -0.5
    k_a = l2norm(k_a.reshape(b, l, GDN_HEADS, GDN_DK))
    v_a = v_a.reshape(b, l, GDN_HEADS, GDN_DV)
    beta = jax.nn.sigmoid(gb.astype(F32))
    g_log = -jnp.exp(gdn_a_log.astype(F32)) * jax.nn.softplus(ga.astype(F32) + gdn_dt_bias.astype(F32))
    lpad = (-l) % GDN_CHUNK
    flat_pad = lambda a: jnp.pad(a.reshape(b, l, -1), ((0, 0), (0, lpad), (0, 0)))
    gdn_out, s_new = gdn_gated(flat_pad(q_a), flat_pad(k_a), flat_pad(v_a), flat_pad(g_log), flat_pad(beta),
                               flat_pad(gz), gdn_norm_w, gdn_s0)
    gdn_out = gdn_out[:, :l]
    pos = pos0 + jnp.arange(l)
    q_b = rope(nq.reshape(b, l, NSA_HEADS, HEAD_DIM), pos)
    kvr = lambda a: a.reshape(b, l, NSA_KV_HEADS, HEAD_DIM)
    cmp_rows = jnp.stack([rope(kvr(ck), pos), kvr(cv)], axis=2)
    sel_rows = jnp.stack([rope(kvr(sk), pos), kvr(sv)], axis=2)
    win_rows = jnp.stack([rope(kvr(wk), pos), kvr(wv)], axis=2)
    gates = jax.nn.sigmoid(ng.astype(F32)).reshape(b, l, 3, NSA_KV_HEADS, NSA_GROUP)
    cmp_all = cmp_rows if past_cmp is None else jnp.concatenate([past_cmp, cmp_rows], axis=1)
    sel_all = sel_rows if past_sel is None else jnp.concatenate([past_sel, sel_rows], axis=1)
    if win_buf is None:
        nsa_out = nsa_prompt(q_b, cmp_rows, sel_rows, win_rows, gates, cmp_pe,
                             cmp_w1.astype(BF16), cmp_w2.astype(BF16))
        new_win = win_rows[:, -min(WINDOW, l):]
    else:
        kc, vc, c_end = compress_blocks(cmp_all, cmp_pe, cmp_w1, cmp_w2)
        ks_blk, vs_blk = selection_blocks(sel_all)
        w_buf = win_buf.shape[1]
        win_all = jnp.concatenate([win_buf, win_rows], axis=1)
        kw_pos = pos0 - w_buf + jnp.arange(w_buf + l)
        nsa_out = nsa_core(q_b, pos, kc, vc, c_end, ks_blk, vs_blk,
                           win_all[:, :, 0], win_all[:, :, 1], kw_pos, gates)
        new_win = win_all[:, -w_buf:]
    mix = jnp.concatenate([gdn_out, nsa_out], axis=-1).reshape(b * l, -1).astype(BF16)
    out = mm_resid(mix, w_out_b, x2).reshape(b, l, d)
    return out, cmp_rows, sel_rows, new_win, new_conv, s_new


def ssm_mixer(x, norm_w, w_in_p, conv_w, conv_b, a_log, dt_bias, d_skip, gn_w, w_out_b, conv_buf, h0):
    b, l, d = x.shape
    x2 = x.reshape(b * l, d)
    proj = rms_mm(x2, norm_w, w_in_p)[:, :SSM_IN].reshape(b, l, SSM_IN)
    z, xbc, dt = split_cols(proj, (SSM_D_INNER, SSM_CONV_DIM, SSM_HEADS))
    xbc_ext = jnp.concatenate([conv_buf, xbc], axis=1)
    new_conv = xbc_ext[:, -(SSM_CONV - 1):]
    xbc = jax.nn.silu(causal_dwconv(xbc_ext, conv_w) + conv_b)
    xs, bm, cm = split_cols(xbc, (SSM_D_INNER, SSM_BC, SSM_BC))
    xs = xs.reshape(b, l, SSM_HEADS, SSM_HEAD_DIM)
    bm = bm.reshape(b, l, SSM_GROUPS, SSM_STATE)
    cm = cm.reshape(b, l, SSM_GROUPS, SSM_STATE)
    dt = jax.nn.softplus(dt.astype(F32) + dt_bias.astype(F32))
    a_neg = -jnp.exp(a_log.astype(F32))
    y, h_new = ssd_chunked(xs, dt, a_neg, bm, cm, h0)
    y = y + d_skip.astype(F32)[:, None] * xs.astype(F32)
    y = y.reshape(b, l, SSM_D_INNER) * jax.nn.silu(z.astype(F32))
    y = rmsnorm(y.reshape(b, l, SSM_GROUPS, SSM_D_INNER // SSM_GROUPS), gn_w.reshape(SSM_GROUPS, -1))
    y = y.reshape(b * l, SSM_D_INNER).astype(BF16)
    return mm_resid(y, w_out_b, x2).reshape(b, l, d), new_conv, h_new


def gather_pages(pool, page_table):
    g = pool[page_table]
    return g.reshape((g.shape[0], g.shape[1] * g.shape[2]) + g.shape[3:])


def _pad_cols(w, mult):
    n = w.shape[-1]
    return jnp.pad(w, ((0, 0), (0, (-n) % mult)))


def kernel(x_prompt, x_sample, cache_cmp_kv, cache_sel_kv, cache_win_kv, state_gdn_conv, state_gdn, state_ssm_conv, state_ssm, page_table, hyb_norm_mix, hyb_w_in, hyb_gdn_conv_w, hyb_gdn_a_log, hyb_gdn_dt_bias, hyb_gdn_norm_w, hyb_cmp_pe, hyb_cmp_w1, hyb_cmp_w2, hyb_w_out, hyb_norm_ffn, ffn_w_gate, ffn_w_up, ffn_w_down, ssm_norm_mix, ssm_w_in, ssm_conv_w, ssm_conv_b, ssm_a_log, ssm_dt_bias, ssm_d_skip, ssm_norm_w, ssm_w_out, ssm_norm_ffn, moe_router, moe_w_gate, moe_w_up, moe_w_down, final_norm):
    hp, hs = x_prompt, x_sample
    bp, lp, d = hp.shape
    bs, ls, _ = hs.shape

    w_in_p = _pad_cols(hyb_w_in[0].astype(BF16), 512)
    w_out_b = hyb_w_out[0].astype(BF16)
    hw = (w_in_p, hyb_gdn_conv_w[0], hyb_gdn_a_log[0], hyb_gdn_dt_bias[0], hyb_gdn_norm_w[0],
          hyb_cmp_pe[0], hyb_cmp_w1[0], hyb_cmp_w2[0], w_out_b)
    zero_conv = jnp.zeros((bp, GDN_CONV - 1, GDN_CONV_DIM), hp.dtype)
    zero_s = jnp.zeros((bp, GDN_HEADS, GDN_DK, GDN_DV), hp.dtype)
    hp, cmp_p, sel_p, win_p, gconv_p, gst_p = hybrid_mixer(
        hp, hyb_norm_mix[0], 0, *hw, zero_conv, zero_s, None, None, None)
    past_cmp = gather_pages(cache_cmp_kv[0], page_table)
    past_sel = gather_pages(cache_sel_kv[0], page_table)
    hs, cmp_s, sel_s, win_s, gconv_s, gst_s = hybrid_mixer(
        hs, hyb_norm_mix[0], PAST_LEN, *hw, state_gdn_conv[0], state_gdn[0], past_cmp, past_sel, cache_win_kv[0])

    wg, wu, wd = ffn_w_gate[0].astype(BF16), ffn_w_up[0].astype(BF16), ffn_w_down[0].astype(BF16)

    def dense_ffn(x):
        b, l, _ = x.shape
        x2 = x.reshape(b * l, d)
        return mm_resid(rms_glu(x2, hyb_norm_ffn[0], wg, wu), wd, x2).reshape(b, l, d)

    hp = dense_ffn(hp)
    hs = dense_ffn(hs)

    sw_in_p = _pad_cols(ssm_w_in[0].astype(BF16), 512)
    sw_out_b = ssm_w_out[0].astype(BF16)
    sw = (sw_in_p, ssm_conv_w[0], ssm_conv_b[0], ssm_a_log[0], ssm_dt_bias[0], ssm_d_skip[0], ssm_norm_w[0], sw_out_b)
    zero_conv = jnp.zeros((bp, SSM_CONV - 1, SSM_CONV_DIM), hp.dtype)
    zero_h = jnp.zeros((bp, SSM_HEADS, SSM_HEAD_DIM, SSM_STATE), hp.dtype)
    hp, sconv_p, sst_p = ssm_mixer(hp, ssm_norm_mix[0], *sw, zero_conv, zero_h)
    hs, sconv_s, sst_s = ssm_mixer(hs, ssm_norm_mix[0], *sw, state_ssm_conv[0], state_ssm[0])

    tok = jnp.concatenate([hp.reshape(bp * lp, d), hs.reshape(bs * ls, d)], axis=0)
    n_tok = tok.shape[0]
    tok_pad = jnp.pad(tok, ((0, (-n_tok) % MOE_TM), (0, 0)))
    moe = moe_ffn(tok_pad, ssm_norm_ffn[0], moe_router[0], moe_w_gate[0].astype(BF16),
                  moe_w_up[0].astype(BF16), moe_w_down[0].astype(BF16))
    y = rms(tok_pad + moe, final_norm)
    y_prompt = y[:bp * lp].reshape(bp, lp, d)
    y_sample = y[bp * lp:n_tok].reshape(bs, ls, d)
    st = lambda a: a[None]
    return (y_prompt, y_sample,
            st(cmp_p), st(cmp_s), st(sel_p), st(sel_s), st(win_p), st(win_s),
            st(gconv_p), st(gconv_s), st(gst_p), st(gst_s),
            st(sconv_p), st(sconv_s), st(sst_p), st(sst_s))
```

```python
import functools
import math

import jax
import jax.numpy as jnp
from jax import lax
from jax.experimental import pallas as pl
from jax.experimental.pallas import tpu as pltpu

D_MODEL = 2048
PAST_LEN = 16384
HEAD_DIM = 128
ROPE_THETA = 10000.0
RMS_EPS = 1e-6

GDN_HEADS = D_MODEL // 256
GDN_DK = 128
GDN_DV = 128
GDN_CONV = 4
GDN_CHUNK = 64
GDN_QK = GDN_HEADS * GDN_DK
GDN_VW = GDN_HEADS * GDN_DV
GDN_CONV_DIM = 2 * GDN_QK + GDN_VW

NSA_HEADS = D_MODEL // 256
NSA_KV_HEADS = 2
NSA_GROUP = NSA_HEADS // NSA_KV_HEADS
NSA_QW = NSA_HEADS * HEAD_DIM
NSA_KVW = NSA_KV_HEADS * HEAD_DIM
CMP_BLOCK = 32
CMP_STRIDE = 16
SEL_BLOCK = 64
SEL_TOPK = 16
WINDOW = 512
Q_BLOCK = 128
FORCE_BONUS = 1e4

HYB_SPLITS = (GDN_QK, GDN_QK, GDN_VW, GDN_VW, GDN_HEADS, GDN_HEADS,
              NSA_QW, NSA_KVW, NSA_KVW, NSA_KVW, NSA_KVW, NSA_KVW, NSA_KVW, 3 * NSA_HEADS)
HYB_IN = sum(HYB_SPLITS)

SSM_D_INNER = 2 * D_MODEL
SSM_HEAD_DIM = 64
SSM_HEADS = SSM_D_INNER // SSM_HEAD_DIM
SSM_GROUPS = 8
SSM_HPG = SSM_HEADS // SSM_GROUPS
SSM_STATE = 128
SSM_CONV = 4
SSM_CHUNK = 128
SSM_BC = SSM_GROUPS * SSM_STATE
SSM_CONV_DIM = SSM_D_INNER + 2 * SSM_BC
SSM_IN = SSM_D_INNER + SSM_CONV_DIM + SSM_HEADS

N_EXPERTS = 8
TOP_K = 2

VMEM_LIMIT_BYTES = 56 * 1024 * 1024
LANES = 128

F32 = jnp.float32
BF16 = jnp.bfloat16


def _params(*sem):
    return pltpu.CompilerParams(dimension_semantics=sem, vmem_limit_bytes=VMEM_LIMIT_BYTES)


def _pick(n, prefs):
    for p in prefs:
        if n % p == 0:
            return p
    return n


def _rms_to_bf16(x, g):
    ms = jnp.mean(x * x, axis=-1, keepdims=True)
    return (x * lax.rsqrt(ms + RMS_EPS) * g).astype(BF16)


def _rms_mm_kernel(x_ref, g_ref, w_ref, o_ref, a_scr):
    @pl.when(pl.program_id(1) == 0)
    def _():
        a_scr[...] = _rms_to_bf16(x_ref[...], g_ref[...])

    o_ref[...] = jnp.dot(a_scr[...], w_ref[...], preferred_element_type=F32).astype(o_ref.dtype)


def rms_mm(x, gain, w, out_dtype=F32):
    m, k = x.shape
    n = w.shape[1]
    tm = _pick(m, (512, 256, 128))
    tn = _pick(n, (512, 256, 128))
    return pl.pallas_call(
        _rms_mm_kernel,
        grid=(m // tm, n // tn),
        in_specs=[pl.BlockSpec((tm, k), lambda i, j: (i, 0)),
                  pl.BlockSpec((1, k), lambda i, j: (0, 0)),
                  pl.BlockSpec((k, tn), lambda i, j: (0, j))],
        out_specs=pl.BlockSpec((tm, tn), lambda i, j: (i, j)),
        out_shape=jax.ShapeDtypeStruct((m, n), out_dtype),
        scratch_shapes=[pltpu.VMEM((tm, k), BF16)],
        compiler_params=_params("parallel", "arbitrary"),
        name="rms_mm",
    )(x, gain.reshape(1, k).astype(F32), w)


def _rms_glu_kernel(x_ref, g_ref, wg_ref, wu_ref, o_ref, a_scr):
    @pl.when(pl.program_id(1) == 0)
    def _():
        a_scr[...] = _rms_to_bf16(x_ref[...], g_ref[...])

    a = a_scr[...]
    gt = jnp.dot(a, wg_ref[...], preferred_element_type=F32)
    up = jnp.dot(a, wu_ref[...], preferred_element_type=F32)
    o_ref[...] = (gt * jax.nn.sigmoid(gt) * up).astype(o_ref.dtype)


def rms_glu(x, gain, wg, wu):
    m, k = x.shape
    n = wg.shape[1]
    tm = _pick(m, (512, 256, 128))
    tn = _pick(n, (512, 256, 128))
    return pl.pallas_call(
        _rms_glu_kernel,
        grid=(m // tm, n // tn),
        in_specs=[pl.BlockSpec((tm, k), lambda i, j: (i, 0)),
                  pl.BlockSpec((1, k), lambda i, j: (0, 0)),
                  pl.BlockSpec((k, tn), lambda i, j: (0, j)),
                  pl.BlockSpec((k, tn), lambda i, j: (0, j))],
        out_specs=pl.BlockSpec((tm, tn), lambda i, j: (i, j)),
        out_shape=jax.ShapeDtypeStruct((m, n), BF16),
        scratch_shapes=[pltpu.VMEM((tm, k), BF16)],
        compiler_params=_params("parallel", "arbitrary"),
        name="rms_glu",
    )(x, gain.reshape(1, k).astype(F32), wg, wu)


def _mm_resid_kernel(a_ref, w_ref, r_ref, o_ref, acc_ref, *, nk):
    kk = pl.program_id(2)

    @pl.when(kk == 0)
    def _():
        acc_ref[...] = r_ref[...]

    acc_ref[...] += jnp.dot(a_ref[...], w_ref[...], preferred_element_type=F32)

    @pl.when(kk == nk - 1)
    def _():
        o_ref[...] = acc_ref[...]


def mm_resid(a, w, resid):
    m, k = a.shape
    n = w.shape[1]
    tm = _pick(m, (512, 256, 128))
    tn = _pick(n, (512, 256, 128))
    tk = _pick(k, (2048, 1792, 1408, 1024, 512))
    nk = k // tk
    return pl.pallas_call(
        functools.partial(_mm_resid_kernel, nk=nk),
        grid=(m // tm, n // tn, nk),
        in_specs=[pl.BlockSpec((tm, tk), lambda i, j, kk: (i, kk)),
                  pl.BlockSpec((tk, tn), lambda i, j, kk: (kk, j)),
                  pl.BlockSpec((tm, tn), lambda i, j, kk: (i, j))],
        out_specs=pl.BlockSpec((tm, tn), lambda i, j, kk: (i, j)),
        out_shape=jax.ShapeDtypeStruct((m, n), F32),
        scratch_shapes=[pltpu.VMEM((tm, tn), F32)],
        compiler_params=_params("parallel", "parallel", "arbitrary"),
        name="mm_resid",
    )(a, w, resid)


def _rms_kernel(x_ref, g_ref, o_ref):
    x = x_ref[...]
    ms = jnp.mean(x * x, axis=-1, keepdims=True)
    o_ref[...] = (x * lax.rsqrt(ms + RMS_EPS) * g_ref[...]).astype(o_ref.dtype)


def rms(x, gain, out_dtype=F32):
    m, k = x.shape
    tm = _pick(m, (512, 256, 128))
    return pl.pallas_call(
        _rms_kernel,
        grid=(m // tm,),
        in_specs=[pl.BlockSpec((tm, k), lambda i: (i, 0)),
                  pl.BlockSpec((1, k), lambda i: (0, 0))],
        out_specs=pl.BlockSpec((tm, k), lambda i: (i, 0)),
        out_shape=jax.ShapeDtypeStruct((m, k), out_dtype),
        compiler_params=_params("parallel"),
        name="rms",
    )(x, gain.reshape(1, k).astype(F32))


def _rms_router_kernel(x_ref, g_ref, wh_ref, wl_ref, h_ref, lg_ref):
    x = x_ref[...]
    ms = jnp.mean(x * x, axis=-1, keepdims=True)
    h = x * lax.rsqrt(ms + RMS_EPS) * g_ref[...]
    hh = h.astype(BF16)
    hl = (h - hh.astype(F32)).astype(BF16)
    h_ref[...] = hh
    wh = wh_ref[...]
    wl = wl_ref[...]
    lg = jnp.dot(hh, wh, preferred_element_type=F32)
    lg += jnp.dot(hl, wh, preferred_element_type=F32)
    lg += jnp.dot(hh, wl, preferred_element_type=F32)
    lg_ref[...] = lg


def rms_router(x, gain, router):
    m, k = x.shape
    e = router.shape[1]
    rp = jnp.pad(router.astype(F32), ((0, 0), (0, LANES - e)))
    rh = rp.astype(BF16)
    rl = (rp - rh.astype(F32)).astype(BF16)
    tm = _pick(m, (512, 256, 128, 32))
    h, lg = pl.pallas_call(
        _rms_router_kernel,
        grid=(m // tm,),
        in_specs=[pl.BlockSpec((tm, k), lambda i: (i, 0)),
                  pl.BlockSpec((1, k), lambda i: (0, 0)),
                  pl.BlockSpec((k, LANES), lambda i: (0, 0)),
                  pl.BlockSpec((k, LANES), lambda i: (0, 0))],
        out_specs=[pl.BlockSpec((tm, k), lambda i: (i, 0)),
                   pl.BlockSpec((tm, LANES), lambda i: (i, 0))],
        out_shape=[jax.ShapeDtypeStruct((m, k), BF16), jax.ShapeDtypeStruct((m, LANES), F32)],
        compiler_params=_params("parallel"),
        name="rms_router",
    )(x, gain.reshape(1, k).astype(F32), rh, rl)
    return h, lg[:, :e]


MOE_TM = 256


def _moe_glu_kernel(te_ref, tv_ref, tf_ref, a_ref, wg_ref, wu_ref, o_ref, wg_b, wu_b):
    i = pl.program_id(1)

    @pl.when(tf_ref[i] != 0)
    def _():
        wg_b[...] = wg_ref[...].astype(BF16)
        wu_b[...] = wu_ref[...].astype(BF16)

    @pl.when(tv_ref[i] != 0)
    def _():
        a = a_ref[...]
        gt = jnp.dot(a, wg_b[...], preferred_element_type=F32)
        up = jnp.dot(a, wu_b[...], preferred_element_type=F32)
        o_ref[...] = (gt * jax.nn.sigmoid(gt) * up).astype(o_ref.dtype)

    @pl.when(tv_ref[i] == 0)
    def _():
        o_ref[...] = jnp.zeros_like(o_ref)


def _moe_down_kernel(te_ref, tv_ref, tf_ref, a_ref, w_ref, o_ref, w_b):
    i = pl.program_id(1)

    @pl.when(tf_ref[i] != 0)
    def _():
        w_b[...] = w_ref[...].astype(BF16)

    @pl.when(tv_ref[i] != 0)
    def _():
        o_ref[...] = jnp.dot(a_ref[...], w_b[...], preferred_element_type=F32)

    @pl.when(tv_ref[i] == 0)
    def _():
        o_ref[...] = jnp.zeros_like(o_ref)


def moe_experts(a_sorted, tile_expert, tile_valid, tile_first, wg, wu, wd):
    r, d = a_sorted.shape
    f = wg.shape[2]
    tm = MOE_TM
    tf = _pick(f, (1024, 512))
    act = pl.pallas_call(
        _moe_glu_kernel,
        grid_spec=pltpu.PrefetchScalarGridSpec(
            num_scalar_prefetch=3,
            grid=(f // tf, r // tm),
            in_specs=[pl.BlockSpec((tm, d), lambda j, i, te, tv, t1: (i, 0)),
                      pl.BlockSpec((None, d, tf), lambda j, i, te, tv, t1: (te[i], 0, j)),
                      pl.BlockSpec((None, d, tf), lambda j, i, te, tv, t1: (te[i], 0, j))],
            out_specs=pl.BlockSpec((tm, tf), lambda j, i, te, tv, t1: (i, j)),
            scratch_shapes=[pltpu.VMEM((d, tf), BF16), pltpu.VMEM((d, tf), BF16)],
        ),
        out_shape=jax.ShapeDtypeStruct((r, f), BF16),
        compiler_params=_params("arbitrary", "arbitrary"),
        name="moe_glu",
    )(tile_expert, tile_valid, tile_first, a_sorted, wg, wu)
    tn = _pick(d, (512,))
    return pl.pallas_call(
        _moe_down_kernel,
        grid_spec=pltpu.PrefetchScalarGridSpec(
            num_scalar_prefetch=3,
            grid=(d // tn, r // tm),
            in_specs=[pl.BlockSpec((tm, f), lambda j, i, te, tv, t1: (i, 0)),
                      pl.BlockSpec((None, f, tn), lambda j, i, te, tv, t1: (te[i], 0, j))],
            out_specs=pl.BlockSpec((tm, tn), lambda j, i, te, tv, t1: (i, j)),
            scratch_shapes=[pltpu.VMEM((f, tn), BF16)],
        ),
        out_shape=jax.ShapeDtypeStruct((r, d), F32),
        compiler_params=_params("arbitrary", "arbitrary"),
        name="moe_down",
    )(tile_expert, tile_valid, tile_first, act, wd)


def moe_ffn(x, gain, router, wg, wu, wd):
    t, d = x.shape
    tm = MOE_TM
    h, logits = rms_router(x, gain, router)
    top_v, top_i = lax.top_k(logits, TOP_K)
    top_w = jax.nn.softmax(top_v, axis=-1)
    n_asg = t * TOP_K
    n_rows = (-(-n_asg // tm) + N_EXPERTS) * tm
    flat_e = top_i.reshape(-1).astype(jnp.int32)
    order = jnp.argsort(flat_e, stable=True).astype(jnp.int32)
    sorted_e = flat_e[order]
    counts = jnp.sum(flat_e[:, None] == jnp.arange(N_EXPERTS, dtype=jnp.int32)[None, :], axis=0).astype(jnp.int32)
    padded = ((counts + tm - 1) // tm) * tm
    pad_end = jnp.cumsum(padded)
    pad_start = pad_end - padded
    start = jnp.cumsum(counts) - counts
    dest = pad_start[sorted_e] + (jnp.arange(n_asg, dtype=jnp.int32) - start[sorted_e])
    row_src = jnp.zeros((n_rows,), jnp.int32).at[dest].set(order // TOP_K)
    pos = jnp.zeros((n_asg,), jnp.int32).at[order].set(dest)
    tile_start = jnp.arange(n_rows // tm, dtype=jnp.int32) * tm
    tile_valid = (tile_start < pad_end[-1]).astype(jnp.int32)
    tile_expert = jnp.minimum(jnp.searchsorted(pad_end, tile_start, side="right"), N_EXPERTS - 1).astype(jnp.int32)
    last_e = tile_expert[jnp.maximum(pad_end[-1] // tm - 1, 0)]
    tile_expert = jnp.where(tile_valid != 0, tile_expert, last_e)
    tile_first = jnp.concatenate([jnp.ones((1,), jnp.int32),
                                  (tile_expert[1:] != tile_expert[:-1]).astype(jnp.int32)])
    a_sorted = jnp.take(h, row_src, axis=0)
    y = moe_experts(a_sorted, tile_expert, tile_valid, tile_first, wg, wu, wd)
    yk = jnp.take(y, pos, axis=0).reshape(t, TOP_K, d)
    return jnp.sum(yk * top_w[..., None], axis=1)


NEG_BIG = -1e30
NSA_TQ = 128
SEL_CHUNK = 256
CMP_HIDDEN = 256


def _dot_nt(a, b):
    return lax.dot_general(a, b, (((1,), (1,)), ((), ())), preferred_element_type=F32)


def _cmp_mlp_kernel(x_ref, pe_ref, w1_ref, w2_ref, o_ref, hi_scr, *, n_seg):
    lo = jnp.zeros((n_seg, CMP_HIDDEN), F32)
    hi = jnp.zeros((n_seg, CMP_HIDDEN), F32)
    for r in range(CMP_STRIDE):
        xr = x_ref[pl.ds(r, n_seg, stride=CMP_STRIDE), :]
        a_lo = (xr + pe_ref[r:r + 1, :]).astype(BF16)
        a_hi = (xr + pe_ref[CMP_STRIDE + r:CMP_STRIDE + r + 1, :]).astype(BF16)
        lo += jnp.dot(a_lo, w1_ref[r * HEAD_DIM:(r + 1) * HEAD_DIM, :], preferred_element_type=F32)
        hi += jnp.dot(a_hi, w1_ref[(CMP_STRIDE + r) * HEAD_DIM:(CMP_STRIDE + r + 1) * HEAD_DIM, :],
                      preferred_element_type=F32)
    hi_scr[0:n_seg, :] = hi
    hi_scr[n_seg:n_seg + 8, :] = jnp.zeros((8, CMP_HIDDEN), F32)
    pre = lo + hi_scr[pl.ds(1, n_seg), :]
    hid = pre * jax.nn.sigmoid(pre)
    o_ref[...] = jnp.dot(hid.astype(BF16), w2_ref[...], preferred_element_type=F32)


def cmp_mlp_prompt(rows2d, pe, w1b, w2b):
    b, l, _ = rows2d.shape
    n_seg = l // CMP_STRIDE
    return pl.pallas_call(
        functools.partial(_cmp_mlp_kernel, n_seg=n_seg),
        grid=(b, 2, NSA_KV_HEADS),
        in_specs=[pl.BlockSpec((None, l, HEAD_DIM), lambda i, s, k: (i, 0, s * NSA_KV_HEADS + k)),
                  pl.BlockSpec((None, CMP_BLOCK, HEAD_DIM), lambda i, s, k: (s, 0, 0)),
                  pl.BlockSpec((None, CMP_BLOCK * HEAD_DIM, CMP_HIDDEN), lambda i, s, k: (s, 0, 0)),
                  pl.BlockSpec((None, CMP_HIDDEN, HEAD_DIM), lambda i, s, k: (s, 0, 0))],
        out_specs=pl.BlockSpec((None, None, None, n_seg, HEAD_DIM), lambda i, s, k: (i, s, k, 0, 0)),
        out_shape=jax.ShapeDtypeStruct((b, 2, NSA_KV_HEADS, n_seg, HEAD_DIM), F32),
        scratch_shapes=[pltpu.VMEM((n_seg + 8, CMP_HIDDEN), F32)],
        compiler_params=_params("parallel", "parallel", "parallel"),
        name="cmp_mlp_prompt",
    )(rows2d, pe, w1b, w2b)


def _nsa_prompt_kernel(q_ref, kc_ref, vc_ref, ks_ref, vs_ref, kw_ref, vw_ref, g_ref, o_ref,
                       m_scr, l_scr, acc_scr, o_scr, sc_scr, *, tq, n_cmp, n_sel):
    t = pl.program_id(2)
    ng = NSA_GROUP
    rows = ng * tq
    q = q_ref[...].reshape(rows, HEAD_DIM)
    qpos = t * tq + lax.broadcasted_iota(jnp.int32, (tq, 1), 0)
    gate = g_ref[...]

    def tile_rows(x):
        return jnp.concatenate([x] * ng, axis=0)

    kvh = pl.program_id(1)

    def gate_col(branch):
        cols = []
        for h in range(ng):
            c0 = NSA_GATE_LANE + branch * NSA_HEADS + h
            cols.append(jnp.where(kvh == 0, gate[:, c0:c0 + 1], gate[:, c0 + ng:c0 + ng + 1]))
        return jnp.concatenate(cols, axis=0)

    def reset():
        m_scr[...] = jnp.full(m_scr.shape, NEG_BIG, F32)
        l_scr[...] = jnp.zeros(l_scr.shape, F32)
        acc_scr[...] = jnp.zeros(acc_scr.shape, F32)

    def update(k, v, mask):
        s = _dot_nt(q, k)
        mask4 = tile_rows(mask)
        sm = jnp.where(mask4, s, NEG_BIG)
        m_prev = m_scr[:, :1]
        m_new = jnp.maximum(m_prev, jnp.max(sm, axis=1, keepdims=True))
        e = jnp.where(mask4, jnp.exp(sm - m_new), 0.0)
        alpha = jnp.exp(m_prev - m_new)
        l_new = alpha * l_scr[:, :1] + jnp.sum(e, axis=1, keepdims=True)
        m_scr[...] = jnp.broadcast_to(m_new, m_scr.shape)
        l_scr[...] = jnp.broadcast_to(l_new, l_scr.shape)
        acc_scr[...] = alpha * acc_scr[...] + jnp.dot(e.astype(BF16), v, preferred_element_type=F32)

    def finish():
        den = l_scr[:, :1]
        return acc_scr[...] / jnp.where(den > 0, den, 1.0)

    n_cp = kc_ref.shape[0]
    kc = kc_ref[...].astype(BF16)
    vc = vc_ref[...].astype(BF16)
    s = _dot_nt(q, kc)
    cidx = lax.broadcasted_iota(jnp.int32, (1, n_cp), 1)
    mask_c = tile_rows((cidx * CMP_STRIDE + (CMP_BLOCK - 1) <= qpos) & (cidx < n_cmp))
    sm = jnp.where(mask_c, s, NEG_BIG)
    e = jnp.where(mask_c, jnp.exp(sm - jnp.max(sm, axis=1, keepdims=True)), 0.0)
    den = jnp.sum(e, axis=1, keepdims=True)
    p = e / jnp.where(den > 0, den, 1.0)
    o_scr[...] = gate_col(0) * jnp.dot(p.astype(BF16), vc, preferred_element_type=F32)
    p_sum = p[0:tq]
    for h in range(1, ng):
        p_sum = p_sum + p[h * tq:(h + 1) * tq]
    p_hi = p_sum.astype(BF16)
    p_lo = (p_sum - p_hi.astype(F32)).astype(BF16)
    cs = lax.broadcasted_iota(jnp.int32, (n_cp, LANES), 0) * CMP_STRIDE
    ss = lax.broadcasted_iota(jnp.int32, (n_cp, LANES), 1) * SEL_BLOCK
    overlap = ((cs < ss + SEL_BLOCK) & (cs + CMP_BLOCK > ss) & (cs < n_cmp * CMP_STRIDE)).astype(BF16)
    imp = jnp.dot(p_hi, overlap, preferred_element_type=F32) + jnp.dot(p_lo, overlap, preferred_element_type=F32)

    blk = lax.broadcasted_iota(jnp.int32, (tq, LANES), 1)
    cur = qpos // SEL_BLOCK
    forced = (blk == 0) | (blk == cur) | (blk == cur - 1)
    score = jnp.where((blk <= cur) & (blk < n_sel), imp + jnp.where(forced, FORCE_BONUS, 0.0), NEG_BIG)
    n_sp = sc_scr.shape[0]
    s_t = score.T[0:n_sp]
    sc_scr[...] = s_t
    jidx = lax.broadcasted_iota(jnp.int32, (n_sp, tq), 0)
    rank = jnp.zeros((n_sp, tq), jnp.int32)
    for i in range(n_sel):
        si = sc_scr[i:i + 1, :]
        rank += ((si > s_t) | ((si == s_t) & (i < jidx))).astype(jnp.int32)
    sel_t = ((rank < SEL_TOPK) & (s_t > 0.5 * NEG_BIG)).astype(F32)
    if n_sp < LANES:
        sel_t = jnp.concatenate([sel_t, jnp.zeros((LANES - n_sp, tq), F32)], axis=0)
    sel = sel_t.T.astype(BF16)

    reset()

    def sel_step(c, carry):
        start = pl.multiple_of(c * SEL_CHUNK, SEL_CHUNK)
        k = ks_ref[pl.ds(start, SEL_CHUNK), :]
        v = vs_ref[pl.ds(start, SEL_CHUNK), :]
        jrow = lax.broadcasted_iota(jnp.int32, (LANES, SEL_CHUNK), 0)
        kblk = (start + lax.broadcasted_iota(jnp.int32, (LANES, SEL_CHUNK), 1)) // SEL_BLOCK
        expand = (jrow == kblk).astype(BF16)
        chosen = jnp.dot(sel, expand, preferred_element_type=F32) > 0.5
        kpos = start + lax.broadcasted_iota(jnp.int32, (1, SEL_CHUNK), 1)
        update(k, v, chosen & (kpos <= qpos))
        return carry

    lax.fori_loop(0, (t * tq + tq + SEL_CHUNK - 1) // SEL_CHUNK, sel_step, 0)
    o_scr[...] += gate_col(1) * finish()

    reset()
    for i in range(WINDOW // tq + 1):
        cw = t - WINDOW // tq + i
        start = pl.multiple_of(jnp.maximum(cw, 0) * tq, tq)
        kpos = cw * tq + lax.broadcasted_iota(jnp.int32, (1, tq), 1)
        dist = qpos - kpos
        update(kw_ref[pl.ds(start, tq), :], vw_ref[pl.ds(start, tq), :],
               (dist >= 0) & (dist < WINDOW) & (kpos >= 0))
    out = o_scr[...] + gate_col(2) * finish()
    for h in range(ng):
        o_ref[:, h * HEAD_DIM:(h + 1) * HEAD_DIM] = out[h * tq:(h + 1) * tq]


def nsa_prompt(q, cmp2d, kvb, sig, pe, w1b, w2b):
    b, _, l, _ = q.shape
    tq = NSA_TQ
    n_cmp = (l - CMP_BLOCK) // CMP_STRIDE + 1
    n_sel = -(-l // SEL_BLOCK)
    kvc = cmp_mlp_prompt(cmp2d, pe, w1b, w2b)
    n_cp = kvc.shape[3]
    n_sp = -(-n_sel // 8) * 8
    rows = NSA_GROUP * tq
    kv_spec = lambda s: pl.BlockSpec((None, None, None, l, HEAD_DIM), lambda i, k, t: (i, s, k, 0, 0))
    return pl.pallas_call(
        functools.partial(_nsa_prompt_kernel, tq=tq, n_cmp=n_cmp, n_sel=n_sel),
        grid=(b, NSA_KV_HEADS, l // tq),
        in_specs=[pl.BlockSpec((None, NSA_GROUP, tq, HEAD_DIM), lambda i, k, t: (i, k, t, 0)),
                  pl.BlockSpec((None, None, None, n_cp, HEAD_DIM), lambda i, k, t: (i, 0, k, 0, 0)),
                  pl.BlockSpec((None, None, None, n_cp, HEAD_DIM), lambda i, k, t: (i, 1, k, 0, 0)),
                  kv_spec(0), kv_spec(1), kv_spec(2), kv_spec(3),
                  pl.BlockSpec((None, tq, LANES), lambda i, k, t: (i, t, 0))],
        out_specs=pl.BlockSpec((None, tq, NSA_GROUP * HEAD_DIM), lambda i, k, t: (i, t, k)),
        out_shape=jax.ShapeDtypeStruct((b, l, NSA_QW), F32),
        scratch_shapes=[pltpu.VMEM((rows, LANES), F32), pltpu.VMEM((rows, LANES), F32),
                        pltpu.VMEM((rows, HEAD_DIM), F32), pltpu.VMEM((rows, HEAD_DIM), F32),
                        pltpu.VMEM((n_sp, tq), F32)],
        compiler_params=_params("parallel", "parallel", "arbitrary"),
        name="nsa_prompt",
    )(q, kvc, kvc, kvb, kvb, kvb, kvb, sig)


HYB_ORDER = (0, 1, 2, 3, 6, 7, 8, 9, 10, 11, 12, 4, 5, 13)
HYB_QKV = 0
HYB_Z_BLK = 3
HYB_NQ = 2 * GDN_QK + 2 * GDN_VW
HYB_KV = HYB_NQ + NSA_QW
HYB_SMALL = HYB_KV + 6 * NSA_KVW
GDN_BETA_LANE = GDN_HEADS
NSA_GATE_LANE = 2 * GDN_HEADS
PREP_ROWS = 256


def reorder_hyb_cols(w):
    offs = [0]
    for n in HYB_SPLITS:
        offs.append(offs[-1] + n)
    return jnp.concatenate([w[:, offs[i]:offs[i + 1]] for i in HYB_ORDER], axis=1)


def _softplus(x):
    return jnp.maximum(x, 0.0) + jnp.log(1.0 + jnp.exp(-jnp.abs(x)))


def _hyb_prep_kernel(p_ref, cw_ref, alog_ref, dtb_ref, cos_ref, sin_ref, hist_ref,
                     qa_ref, ka_ref, va_ref, g_ref, sig_ref, qn_ref, cmp_ref, sel_ref, win_ref, kvb_ref,
                     ext_scr, *, tr):
    kw = GDN_CONV
    off = 8 - (kw - 1)

    @pl.when(pl.program_id(1) == 0)
    def _():
        ext_scr[0:8, :] = hist_ref[...]

    ext_scr[8:8 + tr, :] = p_ref[:, HYB_QKV:HYB_QKV + GDN_CONV_DIM]
    outs = (qa_ref, ka_ref, va_ref)
    for seg in range(3):
        for h in range(GDN_HEADS):
            lo = seg * GDN_QK + h * GDN_DK
            acc = cw_ref[0:1, lo:lo + GDN_DK] * ext_scr[pl.ds(off, tr), lo:lo + GDN_DK]
            for j in range(1, kw):
                acc = acc + cw_ref[j:j + 1, lo:lo + GDN_DK] * ext_scr[pl.ds(off + j, tr), lo:lo + GDN_DK]
            y = _silu(acc)
            if seg < 2:
                y = y * lax.rsqrt(jnp.sum(y * y, axis=-1, keepdims=True) + 1e-6)
            if seg == 0:
                y = y * GDN_DK ** -0.5
            outs[seg][:, h * GDN_DK:(h + 1) * GDN_DK] = y
    ext_scr[0:8, :] = ext_scr[tr:tr + 8, :]

    small = p_ref[:, HYB_SMALL:HYB_SMALL + LANES]
    g_ref[...] = -jnp.exp(alog_ref[...]) * _softplus(small + dtb_ref[...])
    sig_ref[...] = jax.nn.sigmoid(small)

    cos = cos_ref[...]
    sin = sin_ref[...]

    def rope(x):
        return x * cos + pltpu.roll(x, HEAD_DIM // 2, axis=1) * sin

    for h in range(NSA_HEADS):
        x = p_ref[:, HYB_NQ + h * HEAD_DIM:HYB_NQ + (h + 1) * HEAD_DIM]
        qn_ref[h] = (rope(x) * HEAD_DIM ** -0.5).astype(BF16)
    for i, o_ref in enumerate((cmp_ref, sel_ref, win_ref)):
        for kh in range(NSA_KV_HEADS):
            k0 = HYB_KV + 2 * i * NSA_KVW + kh * HEAD_DIM
            kr = rope(p_ref[:, k0:k0 + HEAD_DIM])
            vr = p_ref[:, k0 + NSA_KVW:k0 + NSA_KVW + HEAD_DIM]
            o_ref[:, kh * HEAD_DIM:(kh + 1) * HEAD_DIM] = kr
            o_ref[:, NSA_KVW + kh * HEAD_DIM:NSA_KVW + (kh + 1) * HEAD_DIM] = vr
            if i > 0:
                kvb_ref[2 * (i - 1), kh] = kr.astype(BF16)
                kvb_ref[2 * (i - 1) + 1, kh] = vr.astype(BF16)


def hyb_prep(proj, conv_w, a_log, dt_bias, pos, hist):
    b, l, _ = proj.shape
    tr = PREP_ROWS
    half = HEAD_DIM // 2
    inv_freq = 1.0 / (ROPE_THETA ** (jnp.arange(half, dtype=F32) / half))
    ang = pos.astype(F32)[:, None] * inv_freq[None, :]
    cos = jnp.concatenate([jnp.cos(ang), jnp.cos(ang)], axis=1)
    sin = jnp.concatenate([-jnp.sin(ang), jnp.sin(ang)], axis=1)
    lane_pad = lambda v: jnp.pad(v.astype(F32).reshape(1, -1), ((0, 0), (0, LANES - v.shape[0])))
    row = lambda w: pl.BlockSpec((None, tr, w), lambda i, n: (i, n, 0))
    f32 = lambda w: jax.ShapeDtypeStruct((b, l, w), F32)
    return pl.pallas_call(
        functools.partial(_hyb_prep_kernel, tr=tr),
        grid=(b, l // tr),
        in_specs=[row(proj.shape[2]),
                  pl.BlockSpec((GDN_CONV, GDN_CONV_DIM), lambda i, n: (0, 0)),
                  pl.BlockSpec((1, LANES), lambda i, n: (0, 0)),
                  pl.BlockSpec((1, LANES), lambda i, n: (0, 0)),
                  pl.BlockSpec((tr, HEAD_DIM), lambda i, n: (n, 0)),
                  pl.BlockSpec((tr, HEAD_DIM), lambda i, n: (n, 0)),
                  pl.BlockSpec((None, 8, GDN_CONV_DIM), lambda i, n: (i, 0, 0))],
        out_specs=[row(GDN_QK), row(GDN_QK), row(GDN_VW), row(LANES), row(LANES),
                   pl.BlockSpec((None, NSA_HEADS, tr, HEAD_DIM), lambda i, n: (i, 0, n, 0)),
                   row(2 * NSA_KVW), row(2 * NSA_KVW), row(2 * NSA_KVW),
                   pl.BlockSpec((None, 4, NSA_KV_HEADS, tr, HEAD_DIM), lambda i, n: (i, 0, 0, n, 0))],
        out_shape=[f32(GDN_QK), f32(GDN_QK), f32(GDN_VW), f32(LANES), f32(LANES),
                   jax.ShapeDtypeStruct((b, NSA_HEADS, l, HEAD_DIM), BF16),
                   f32(2 * NSA_KVW), f32(2 * NSA_KVW), f32(2 * NSA_KVW),
                   jax.ShapeDtypeStruct((b, 4, NSA_KV_HEADS, l, HEAD_DIM), BF16)],
        scratch_shapes=[pltpu.VMEM((tr + 8, GDN_CONV_DIM), F32)],
        compiler_params=_params("parallel", "arbitrary"),
        name="hyb_prep",
    )(proj, conv_w.astype(F32), lane_pad(a_log), lane_pad(dt_bias), cos, sin, hist)


def _dot_exact01(a01, x):
    hi = x.astype(BF16)
    r = x - hi.astype(F32)
    mid = r.astype(BF16)
    lo = (r - mid.astype(F32)).astype(BF16)
    return (jnp.dot(a01, hi, preferred_element_type=F32) + jnp.dot(a01, mid, preferred_element_type=F32)
            + jnp.dot(a01, lo, preferred_element_type=F32))


def _gdn_kernel(q_ref, k_ref, v_ref, g_ref, b_ref, z_ref, nw_ref, s0_ref, o_ref, s_ref, *, c, g_off, b_off):
    @pl.when(pl.program_id(1) == 0)
    def _():
        s_ref[...] = s0_ref[...]

    ii = lax.broadcasted_iota(jnp.int32, (c, c), 0)
    jj = lax.broadcasted_iota(jnp.int32, (c, c), 1)
    incl = ii >= jj
    strict = ii > jj
    t01 = incl.astype(BF16)
    w01 = strict.astype(F32)
    g_all = g_ref[...]
    b_all = b_ref[...]
    nw = nw_ref[...]
    for h in range(GDN_HEADS):
        sl = slice(h * GDN_DV, (h + 1) * GDN_DV)
        q = q_ref[:, sl]
        k = k_ref[:, sl]
        v = v_ref[:, sl]
        g_b = jnp.broadcast_to(g_all[:, g_off + h:g_off + h + 1], (c, GDN_DK))
        b_b = jnp.broadcast_to(b_all[:, b_off + h:b_off + h + 1], (c, GDN_DK))
        gam = _dot_exact01(t01, g_b)
        diff = _dot_exact01(t01, g_b[:, :c] * w01)
        decay = jnp.exp(diff)
        kb = k * b_b
        kbf = k.astype(BF16)
        n_mat = jnp.where(strict, _dot_nt(kb.astype(BF16), kbf) * decay, 0.0)
        e_gam = jnp.exp(gam)
        y = jnp.concatenate([v * b_b, kb * e_gam], axis=1)
        y = y - jnp.dot(n_mat.astype(BF16), y.astype(BF16), preferred_element_type=F32)
        p = n_mat
        span = 2
        while span < c:
            pb = p.astype(BF16)
            p = jnp.dot(pb, pb, preferred_element_type=F32)
            y = y + jnp.dot(p.astype(BF16), y.astype(BF16), preferred_element_type=F32)
            span *= 2
        u = y[:, :GDN_DV]
        w = y[:, GDN_DV:]
        s = s_ref[h]
        sb = s.astype(BF16)
        v_new = u - jnp.dot(w.astype(BF16), sb, preferred_element_type=F32)
        vnb = v_new.astype(BF16)
        attn = jnp.where(incl, _dot_nt(q.astype(BF16), kbf) * decay, 0.0)
        o = (jnp.dot((q * e_gam).astype(BF16), sb, preferred_element_type=F32)
             + jnp.dot(attn.astype(BF16), vnb, preferred_element_type=F32))
        g_end = gam[c - 1:c, :]
        k_dec = (k * jnp.exp(g_end - gam)).astype(BF16)
        s_ref[h] = s * jnp.exp(g_end) + lax.dot_general(k_dec, vnb, (((0,), (0,)), ((), ())),
                                                         preferred_element_type=F32)
        on = o * lax.rsqrt(jnp.mean(o * o, axis=-1, keepdims=True) + RMS_EPS) * nw
        z = z_ref[:, sl]
        o_ref[:, sl] = on * (z * jax.nn.sigmoid(z))


def gdn_gated(q, k, v, g, beta, z, norm_w, s0, g_off=0, b_off=0, z_blk=0):
    b, l, _ = q.shape
    c = GDN_CHUNK
    row = lambda w: pl.BlockSpec((None, c, w), lambda i, n: (i, n, 0))
    st = pl.BlockSpec((None, GDN_HEADS, GDN_DK, GDN_DV), lambda i, n: (i, 0, 0, 0))
    return pl.pallas_call(
        functools.partial(_gdn_kernel, c=c, g_off=g_off, b_off=b_off),
        grid=(b, l // c),
        in_specs=[row(GDN_QK), row(GDN_QK), row(GDN_VW), row(g.shape[2]), row(beta.shape[2]),
                  pl.BlockSpec((None, c, GDN_VW), lambda i, n: (i, n, z_blk)),
                  pl.BlockSpec((1, GDN_DV), lambda i, n: (0, 0)), st],
        out_specs=[row(GDN_VW), st],
        out_shape=[jax.ShapeDtypeStruct((b, l, GDN_VW), F32),
                   jax.ShapeDtypeStruct((b, GDN_HEADS, GDN_DK, GDN_DV), F32)],
        compiler_params=_params("parallel", "arbitrary"),
        name="gdn_chunked",
    )(q, k, v, g, beta, z, norm_w.reshape(1, GDN_DV).astype(F32), s0.astype(F32))


def _split3(x):
    hi = x.astype(BF16)
    r = x - hi.astype(F32)
    mid = r.astype(BF16)
    lo = (r - mid.astype(F32)).astype(BF16)
    return hi, mid, lo


def _dot_exact01_right(x, b01):
    hi, mid, lo = _split3(x)
    return (jnp.dot(hi, b01, preferred_element_type=F32) + jnp.dot(mid, b01, preferred_element_type=F32)
            + jnp.dot(lo, b01, preferred_element_type=F32))


def _silu(x):
    return x * jax.nn.sigmoid(x)


def _ssd_kernel(z_ref, x_ref, b_ref, c_ref, dt_ref, cw_ref, cb_ref, an_ref, dtb_ref, dsk_ref, nw_ref,
                r128_ref, r64_ref, hist_ref, h0_ref, y_ref, h_ref, ex_scr, eb_scr, ec_scr, *, c, n_valid):
    n = pl.program_id(1)
    kw = SSM_CONV
    off = 8 - (kw - 1)
    gw = SSM_HPG * SSM_HEAD_DIM

    @pl.when(n == 0)
    def _():
        h_ref[...] = h0_ref[...]
        ex_scr[0:8, :] = hist_ref[:, 0:SSM_D_INNER]
        eb_scr[0:8, :] = hist_ref[:, SSM_D_INNER:SSM_D_INNER + SSM_BC]
        ec_scr[0:8, :] = hist_ref[:, SSM_D_INNER + SSM_BC:SSM_CONV_DIM]

    ex_scr[8:8 + c, :] = x_ref[...]
    eb_scr[8:8 + c, :] = b_ref[...]
    ec_scr[8:8 + c, :] = c_ref[...]

    def conv(scr, col0, lo, width):
        acc = cb_ref[:, col0 + lo:col0 + lo + width]
        for j in range(kw):
            acc = acc + cw_ref[j:j + 1, col0 + lo:col0 + lo + width] * scr[pl.ds(off + j, c), lo:lo + width]
        return _silu(acc)

    ii = lax.broadcasted_iota(jnp.int32, (c, c), 0)
    jj = lax.broadcasted_iota(jnp.int32, (c, c), 1)
    incl = ii >= jj
    t01 = incl.astype(BF16)
    x_dt = dt_ref[...] + dtb_ref[...]
    dtv = jnp.maximum(x_dt, 0.0) + jnp.log(1.0 + jnp.exp(-jnp.abs(x_dt)))
    if n_valid is not None:
        row = n * c + lax.broadcasted_iota(jnp.int32, dtv.shape, 0)
        dtv = jnp.where(row < n_valid, dtv, 0.0)
    gam = _dot_exact01(t01, dtv * an_ref[...])
    gam_t = gam.T
    for g in range(SSM_GROUPS):
        r64 = r64_ref[:, g * gw:(g + 1) * gw]
        r128 = r128_ref[:, g * SSM_HPG * LANES:(g + 1) * SSM_HPG * LANES]
        xg = conv(ex_scr, 0, g * gw, gw)
        bg = conv(eb_scr, SSM_D_INNER, g * SSM_STATE, SSM_STATE).astype(BF16)
        cg = conv(ec_scr, SSM_D_INNER + SSM_BC, g * SSM_STATE, SSM_STATE).astype(BF16)
        gam64 = _dot_exact01_right(gam, r64)
        gam128 = _dot_exact01_right(gam, r128)
        xdt = xg * _dot_exact01_right(dtv, r64)
        xdt_b = xdt.astype(BF16)
        cb = _dot_nt(cg, bg)
        parts = []
        for hl in range(SSM_HPG):
            h = g * SSM_HPG + hl
            dec = jnp.where(incl, jnp.exp(gam128[:, hl * LANES:(hl + 1) * LANES] - gam_t[h:h + 1, :]), 0.0)
            parts.append(jnp.dot((cb * dec).astype(BF16), xdt_b[:, hl * SSM_HEAD_DIM:(hl + 1) * SSM_HEAD_DIM],
                                 preferred_element_type=F32))
        hs = h_ref[:, g * gw:(g + 1) * gw]
        y = (jnp.concatenate(parts, axis=1)
             + jnp.dot(cg, hs.astype(BF16), preferred_element_type=F32) * jnp.exp(gam64))
        g_end = gam64[c - 1:c, :]
        xdec = (xdt * jnp.exp(g_end - gam64)).astype(BF16)
        h_ref[:, g * gw:(g + 1) * gw] = hs * jnp.exp(g_end) + lax.dot_general(
            bg, xdec, (((0,), (0,)), ((), ())), preferred_element_type=F32)
        y = (y + dsk_ref[:, g * gw:(g + 1) * gw] * xg) * _silu(z_ref[:, g * gw:(g + 1) * gw])
        y = y * lax.rsqrt(jnp.mean(y * y, axis=-1, keepdims=True) + RMS_EPS) * nw_ref[:, g * gw:(g + 1) * gw]
        y_ref[:, g * gw:(g + 1) * gw] = y.astype(y_ref.dtype)

    ex_scr[0:8, :] = ex_scr[c:c + 8, :]
    eb_scr[0:8, :] = eb_scr[c:c + 8, :]
    ec_scr[0:8, :] = ec_scr[c:c + 8, :]


def ssd_fused(proj, n_valid, conv_w, conv_b, a_log, dt_bias, d_skip, norm_w, conv_buf, h0):
    b, lp, _ = proj.shape
    c = SSM_CHUNK
    f = lambda a: a.astype(F32)
    lane_pad = lambda v: jnp.pad(f(v).reshape(1, -1), ((0, 0), (0, LANES - v.shape[0])))
    heads = jnp.arange(LANES)[:, None]
    r128 = (heads == jnp.arange(SSM_HEADS * LANES)[None, :] // LANES).astype(BF16)
    r64 = (heads == jnp.arange(SSM_D_INNER)[None, :] // SSM_HEAD_DIM).astype(BF16)
    hist = jnp.pad(f(conv_buf), ((0, 0), (8 - (SSM_CONV - 1), 0), (0, 0)))
    h0_t = f(h0).reshape(b, SSM_D_INNER, SSM_STATE).transpose(0, 2, 1)
    col = lambda w, blk: pl.BlockSpec((None, c, w), lambda i, n: (i, n, blk))
    full = lambda shape: pl.BlockSpec(shape, lambda i, n: (0,) * len(shape))
    per_seq = lambda shape: pl.BlockSpec((None,) + shape, lambda i, n: (i,) + (0,) * len(shape))
    y, h_t = pl.pallas_call(
        functools.partial(_ssd_kernel, c=c, n_valid=None if n_valid == lp else n_valid),
        grid=(b, lp // c),
        in_specs=[col(SSM_D_INNER, 0), col(SSM_D_INNER, 1),
                  col(SSM_BC, 2 * SSM_D_INNER // SSM_BC), col(SSM_BC, 2 * SSM_D_INNER // SSM_BC + 1),
                  col(LANES, (2 * SSM_D_INNER + 2 * SSM_BC) // LANES),
                  full((SSM_CONV, SSM_CONV_DIM)), full((1, SSM_CONV_DIM)), full((1, LANES)), full((1, LANES)),
                  full((1, SSM_D_INNER)), full((1, SSM_D_INNER)),
                  full((LANES, SSM_HEADS * LANES)), full((LANES, SSM_D_INNER)),
                  per_seq((8, SSM_CONV_DIM)), per_seq((SSM_STATE, SSM_D_INNER))],
        out_specs=[col(SSM_D_INNER, 0), per_seq((SSM_STATE, SSM_D_INNER))],
        out_shape=[jax.ShapeDtypeStruct((b, lp, SSM_D_INNER), BF16),
                   jax.ShapeDtypeStruct((b, SSM_STATE, SSM_D_INNER), F32)],
        scratch_shapes=[pltpu.VMEM((c + 8, SSM_D_INNER), F32), pltpu.VMEM((c + 8, SSM_BC), F32),
                        pltpu.VMEM((c + 8, SSM_BC), F32)],
        compiler_params=_params("parallel", "arbitrary"),
        name="ssd_fused",
    )(proj, proj, proj, proj, proj, f(conv_w), f(conv_b).reshape(1, -1), lane_pad(-jnp.exp(f(a_log))),
      lane_pad(dt_bias), jnp.repeat(f(d_skip), SSM_HEAD_DIM).reshape(1, -1), f(norm_w).reshape(1, -1),
      r128, r64, hist, h0_t)
    return y, h_t.transpose(0, 2, 1).reshape(b, SSM_HEADS, SSM_HEAD_DIM, SSM_STATE)


def rmsnorm(x, w):
    xf = x.astype(F32)
    y = xf * lax.rsqrt(jnp.mean(xf * xf, axis=-1, keepdims=True) + RMS_EPS)
    return (y * w.astype(F32)).astype(x.dtype)


def l2norm(x):
    xf = x.astype(F32)
    return (xf * lax.rsqrt(jnp.sum(xf * xf, axis=-1, keepdims=True) + 1e-6)).astype(x.dtype)


def rope(x, pos):
    half = x.shape[-1] // 2
    inv_freq = 1.0 / (ROPE_THETA ** (jnp.arange(half, dtype=F32) / half))
    ang = pos.astype(F32)[:, None] * inv_freq[None, :]
    cos = jnp.cos(ang)[None, :, None, :]
    sin = jnp.sin(ang)[None, :, None, :]
    xf = x.astype(F32)
    x1, x2 = xf[..., :half], xf[..., half:]
    return jnp.concatenate([x1 * cos - x2 * sin, x2 * cos + x1 * sin], axis=-1).astype(x.dtype)


def split_cols(a, sizes):
    out, s = [], 0
    for n in sizes:
        out.append(a[..., s:s + n])
        s += n
    return out


def causal_dwconv(x_ext, w):
    c = x_ext.shape[-1]
    return lax.conv_general_dilated(x_ext, w[:, None, :], window_strides=(1,), padding='VALID',
                                    dimension_numbers=('NWC', 'WIO', 'NWC'), feature_group_count=c)


def masked_softmax(s, mask):
    s = jnp.where(mask, s, -jnp.inf)
    m = jnp.max(s, axis=-1, keepdims=True)
    m = jnp.where(jnp.isfinite(m), m, 0.0)
    e = jnp.where(mask, jnp.exp(s - m), 0.0)
    den = jnp.sum(e, axis=-1, keepdims=True)
    return e / jnp.where(den > 0, den, 1.0)


def compress_blocks(rows, pe, w1, w2):
    b, lk = rows.shape[:2]
    n_cmp = (lk - CMP_BLOCK) // CMP_STRIDE + 1
    per = CMP_BLOCK // CMP_STRIDE
    segs = rows[:, :(n_cmp + per - 1) * CMP_STRIDE].reshape(
        b, n_cmp + per - 1, CMP_STRIDE, 2, NSA_KV_HEADS, HEAD_DIM)
    blocks = jnp.concatenate([segs[:, i:i + n_cmp] for i in range(per)], axis=2)
    blocks = blocks + jnp.transpose(pe, (1, 0, 2))[:, :, None, :]
    flat = jnp.transpose(blocks, (0, 1, 3, 4, 2, 5)).reshape(b, n_cmp, 2, NSA_KV_HEADS, CMP_BLOCK * HEAD_DIM)
    hid = jax.nn.silu(jnp.einsum('bcskf,sfh->bcskh', flat, w1))
    out = jnp.einsum('bcskh,she->bcske', hid, w2)
    c_end = jnp.arange(n_cmp) * CMP_STRIDE + CMP_BLOCK - 1
    return out[:, :, 0], out[:, :, 1], c_end


def selection_blocks(rows):
    b, lk = rows.shape[:2]
    n_sel = -(-lk // SEL_BLOCK)
    rows = jnp.pad(rows, ((0, 0), (0, n_sel * SEL_BLOCK - lk), (0, 0), (0, 0), (0, 0)))
    blk = rows.reshape(b, n_sel, SEL_BLOCK, 2, NSA_KV_HEADS, HEAD_DIM).transpose(3, 0, 4, 1, 2, 5)
    return blk[0], blk[1]


def cmp_to_sel_map(n_cmp, n_sel):
    cs = jnp.arange(n_cmp)[:, None] * CMP_STRIDE
    ss = jnp.arange(n_sel)[None, :] * SEL_BLOCK
    return ((cs < ss + SEL_BLOCK) & (cs + CMP_BLOCK > ss)).astype(F32)


def nsa_core(q, q_pos, kc, vc, c_end, ks_blk, vs_blk, kw, vw, kw_pos, gates):
    b, lq = q.shape[:2]
    qg = q.astype(F32).reshape(b, lq, NSA_KV_HEADS, NSA_GROUP, HEAD_DIM) * HEAD_DIM ** -0.5
    s_c = jnp.einsum('bqkgd,bckd->bqkgc', qg, kc.astype(F32))
    m_c = c_end[None, :] <= q_pos[:, None]
    p_c = masked_softmax(s_c, m_c[None, :, None, None, :])
    o_c = jnp.einsum('bqkgc,bckd->bqkgd', p_c, vc.astype(F32))
    n_cmp, n_sel = kc.shape[1], ks_blk.shape[2]
    imp = jnp.einsum('bqkgc,cj->bqkj', p_c, cmp_to_sel_map(n_cmp, n_sel))
    blk = jnp.arange(n_sel)[None, :]
    cur = (q_pos // SEL_BLOCK)[:, None]
    forced = (blk == 0) | (blk == cur) | (blk == cur - 1)
    score = jnp.where((blk <= cur)[None, :, None, :],
                      imp + jnp.where(forced, FORCE_BONUS, 0.0)[None, :, None, :], -jnp.inf)
    n_top = min(SEL_TOPK, n_sel)
    top_s, top_i = lax.top_k(score, n_top)
    bi = jnp.arange(b)[:, None, None, None]
    ki = jnp.arange(NSA_KV_HEADS)[None, None, :, None]
    k_s = ks_blk[bi, ki, top_i].astype(F32).reshape(b, lq, NSA_KV_HEADS, n_top * SEL_BLOCK, HEAD_DIM)
    v_s = vs_blk[bi, ki, top_i].astype(F32).reshape(b, lq, NSA_KV_HEADS, n_top * SEL_BLOCK, HEAD_DIM)
    key_pos = top_i[..., None] * SEL_BLOCK + jnp.arange(SEL_BLOCK)
    m_s = jnp.isfinite(top_s)[..., None] & (key_pos <= q_pos[None, :, None, None, None])
    s_s = jnp.einsum('bqkgd,bqkjd->bqkgj', qg, k_s)
    p_s = masked_softmax(s_s, m_s.reshape(b, lq, NSA_KV_HEADS, 1, n_top * SEL_BLOCK))
    o_s = jnp.einsum('bqkgj,bqkjd->bqkgd', p_s, v_s)
    s_w = jnp.einsum('bqkgd,blkd->bqkgl', qg, kw.astype(F32))
    dist = q_pos[:, None] - kw_pos[None, :]
    m_w = (dist >= 0) & (dist < WINDOW) & (kw_pos[None, :] >= 0)
    p_w = masked_softmax(s_w, m_w[None, :, None, None, :])
    o_w = jnp.einsum('bqkgl,blkd->bqkgd', p_w, vw.astype(F32))
    g = gates.astype(F32)[..., None]
    o = g[:, :, 0] * o_c + g[:, :, 1] * o_s + g[:, :, 2] * o_w
    return o.reshape(b, lq, NSA_QW).astype(q.dtype)


def hybrid_prompt(x, norm_w, w_in_p, gdn_conv_w, gdn_a_log, gdn_dt_bias, gdn_norm_w, cmp_pe, cmp_w1, cmp_w2, w_out_b):
    b, l, d = x.shape
    x2 = x.reshape(b * l, d)
    proj = rms_mm(x2, norm_w, w_in_p).reshape(b, l, -1)
    hist = jnp.zeros((b, 8, GDN_CONV_DIM), F32)
    q_a, k_a, v_a, g_log, sig, q_n, cmp2d, sel2d, win2d, kvb = hyb_prep(
        proj, gdn_conv_w, gdn_a_log, gdn_dt_bias, jnp.arange(l), hist)
    s0 = jnp.zeros((b, GDN_HEADS, GDN_DK, GDN_DV), F32)
    gdn_out, s_new = gdn_gated(q_a, k_a, v_a, g_log, sig, proj, gdn_norm_w, s0,
                               g_off=0, b_off=GDN_BETA_LANE, z_blk=HYB_Z_BLK)
    nsa_out = nsa_prompt(q_n, cmp2d, kvb, sig, cmp_pe, cmp_w1.astype(BF16), cmp_w2.astype(BF16))
    mix = jnp.concatenate([gdn_out, nsa_out], axis=-1).reshape(b * l, -1).astype(BF16)
    out = mm_resid(mix, w_out_b, x2).reshape(b, l, d)
    rows5 = lambda r: r.reshape(b, l, 2, NSA_KV_HEADS, HEAD_DIM)
    new_conv = proj[:, -(GDN_CONV - 1):, HYB_QKV:HYB_QKV + GDN_CONV_DIM]
    return out, rows5(cmp2d), rows5(sel2d), rows5(win2d)[:, -min(WINDOW, l):], new_conv, s_new


def hybrid_sample(x, norm_w, pos0, w_in_p, gdn_conv_w, gdn_a_log, gdn_dt_bias, gdn_norm_w, cmp_pe, cmp_w1, cmp_w2,
                  w_out_b, gdn_conv_buf, gdn_s0, past_cmp, past_sel, win_buf):
    b, l, d = x.shape
    x2 = x.reshape(b * l, d)
    proj = rms_mm(x2, norm_w, w_in_p)[:, :HYB_IN].reshape(b, l, HYB_IN)
    (gq, gk, gv, gz, nq, ck, cv, sk, sv, wk, wv, ga, gb, ng) = split_cols(proj, [HYB_SPLITS[i] for i in HYB_ORDER])
    qkv_ext = jnp.concatenate([gdn_conv_buf, jnp.concatenate([gq, gk, gv], axis=-1)], axis=1)
    new_conv = qkv_ext[:, -(GDN_CONV - 1):]
    qkv = jax.nn.silu(causal_dwconv(qkv_ext, gdn_conv_w))
    q_a, k_a, v_a = split_cols(qkv, (GDN_QK, GDN_QK, GDN_VW))
    q_a = l2norm(q_a.reshape(b, l, GDN_HEADS, GDN_DK)) * GDN_DK ** -0.5
    k_a = l2norm(k_a.reshape(b, l, GDN_HEADS, GDN_DK))
    v_a = v_a.reshape(b, l, GDN_HEADS, GDN_DV)
    beta = jax.nn.sigmoid(gb.astype(F32))
    g_log = -jnp.exp(gdn_a_log.astype(F32)) * jax.nn.softplus(ga.astype(F32) + gdn_dt_bias.astype(F32))
    lpad = (-l) % GDN_CHUNK
    flat_pad = lambda a: jnp.pad(a.reshape(b, l, -1), ((0, 0), (0, lpad), (0, 0)))
    gdn_out, s_new = gdn_gated(flat_pad(q_a), flat_pad(k_a), flat_pad(v_a), flat_pad(g_log), flat_pad(beta),
                               flat_pad(gz), gdn_norm_w, gdn_s0)
    gdn_out = gdn_out[:, :l]
    pos = pos0 + jnp.arange(l)
    q_b = rope(nq.reshape(b, l, NSA_HEADS, HEAD_DIM), pos)
    kvr = lambda a: a.reshape(b, l, NSA_KV_HEADS, HEAD_DIM)
    cmp_rows = jnp.stack([rope(kvr(ck), pos), kvr(cv)], axis=2)
    sel_rows = jnp.stack([rope(kvr(sk), pos), kvr(sv)], axis=2)
    win_rows = jnp.stack([rope(kvr(wk), pos), kvr(wv)], axis=2)
    gates = jax.nn.sigmoid(ng.astype(F32)).reshape(b, l, 3, NSA_KV_HEADS, NSA_GROUP)
    cmp_all = jnp.concatenate([past_cmp, cmp_rows], axis=1)
    sel_all = jnp.concatenate([past_sel, sel_rows], axis=1)
    kc, vc, c_end = compress_blocks(cmp_all, cmp_pe, cmp_w1, cmp_w2)
    ks_blk, vs_blk = selection_blocks(sel_all)
    w_buf = win_buf.shape[1]
    win_all = jnp.concatenate([win_buf, win_rows], axis=1)
    kw_pos = pos0 - w_buf + jnp.arange(w_buf + l)
    nsa_out = nsa_core(q_b, pos, kc, vc, c_end, ks_blk, vs_blk,
                       win_all[:, :, 0], win_all[:, :, 1], kw_pos, gates)
    new_win = win_all[:, -w_buf:]
    mix = jnp.concatenate([gdn_out, nsa_out], axis=-1).reshape(b * l, -1).astype(BF16)
    out = mm_resid(mix, w_out_b, x2).reshape(b, l, d)
    return out, cmp_rows, sel_rows, new_win, new_conv, s_new


def ssm_mixer(x, norm_w, w_in_p, conv_w, conv_b, a_log, dt_bias, d_skip, gn_w, w_out_b, conv_buf, h0):
    b, l, d = x.shape
    x2 = x.reshape(b * l, d)
    proj = rms_mm(x2, norm_w, w_in_p).reshape(b, l, -1)
    xbc = proj[:, -(SSM_CONV - 1):, SSM_D_INNER:SSM_D_INNER + SSM_CONV_DIM]
    new_conv = jnp.concatenate([conv_buf, xbc], axis=1)[:, -(SSM_CONV - 1):]
    proj = jnp.pad(proj, ((0, 0), (0, (-l) % SSM_CHUNK), (0, 0)))
    y, h_new = ssd_fused(proj, l, conv_w, conv_b, a_log, dt_bias, d_skip, gn_w, conv_buf, h0)
    y = y[:, :l].reshape(b * l, SSM_D_INNER)
    return mm_resid(y, w_out_b, x2).reshape(b, l, d), new_conv, h_new


def gather_pages(pool, page_table):
    g = pool[page_table]
    return g.reshape((g.shape[0], g.shape[1] * g.shape[2]) + g.shape[3:])


def _pad_cols(w, mult):
    n = w.shape[-1]
    return jnp.pad(w, ((0, 0), (0, (-n) % mult)))


def kernel(x_prompt, x_sample, cache_cmp_kv, cache_sel_kv, cache_win_kv, state_gdn_conv, state_gdn, state_ssm_conv, state_ssm, page_table, hyb_norm_mix, hyb_w_in, hyb_gdn_conv_w, hyb_gdn_a_log, hyb_gdn_dt_bias, hyb_gdn_norm_w, hyb_cmp_pe, hyb_cmp_w1, hyb_cmp_w2, hyb_w_out, hyb_norm_ffn, ffn_w_gate, ffn_w_up, ffn_w_down, ssm_norm_mix, ssm_w_in, ssm_conv_w, ssm_conv_b, ssm_a_log, ssm_dt_bias, ssm_d_skip, ssm_norm_w, ssm_w_out, ssm_norm_ffn, moe_router, moe_w_gate, moe_w_up, moe_w_down, final_norm):
    hp, hs = x_prompt, x_sample
    bp, lp, d = hp.shape
    bs, ls, _ = hs.shape

    w_in_p = _pad_cols(reorder_hyb_cols(hyb_w_in[0]).astype(BF16), 512)
    w_out_b = hyb_w_out[0].astype(BF16)
    hw = (w_in_p, hyb_gdn_conv_w[0], hyb_gdn_a_log[0], hyb_gdn_dt_bias[0], hyb_gdn_norm_w[0],
          hyb_cmp_pe[0], hyb_cmp_w1[0], hyb_cmp_w2[0], w_out_b)
    hp, cmp_p, sel_p, win_p, gconv_p, gst_p = hybrid_prompt(hp, hyb_norm_mix[0], *hw)
    past_cmp = gather_pages(cache_cmp_kv[0], page_table)
    past_sel = gather_pages(cache_sel_kv[0], page_table)
    hs, cmp_s, sel_s, win_s, gconv_s, gst_s = hybrid_sample(
        hs, hyb_norm_mix[0], PAST_LEN, *hw, state_gdn_conv[0], state_gdn[0], past_cmp, past_sel, cache_win_kv[0])

    wg, wu, wd = ffn_w_gate[0].astype(BF16), ffn_w_up[0].astype(BF16), ffn_w_down[0].astype(BF16)

    def dense_ffn(x):
        b, l, _ = x.shape
        x2 = x.reshape(b * l, d)
        return mm_resid(rms_glu(x2, hyb_norm_ffn[0], wg, wu), wd, x2).reshape(b, l, d)

    hp = dense_ffn(hp)
    hs = dense_ffn(hs)

    sw_in_p = _pad_cols(ssm_w_in[0].astype(BF16), 512)
    sw_out_b = ssm_w_out[0].astype(BF16)
    sw = (sw_in_p, ssm_conv_w[0], ssm_conv_b[0], ssm_a_log[0], ssm_dt_bias[0], ssm_d_skip[0], ssm_norm_w[0], sw_out_b)
    zero_conv = jnp.zeros((bp, SSM_CONV - 1, SSM_CONV_DIM), hp.dtype)
    zero_h = jnp.zeros((bp, SSM_HEADS, SSM_HEAD_DIM, SSM_STATE), hp.dtype)
    hp, sconv_p, sst_p = ssm_mixer(hp, ssm_norm_mix[0], *sw, zero_conv, zero_h)
    hs, sconv_s, sst_s = ssm_mixer(hs, ssm_norm_mix[0], *sw, state_ssm_conv[0], state_ssm[0])

    tok = jnp.concatenate([hp.reshape(bp * lp, d), hs.reshape(bs * ls, d)], axis=0)
    n_tok = tok.shape[0]
    tok_pad = jnp.pad(tok, ((0, (-n_tok) % MOE_TM), (0, 0)))
    moe = moe_ffn(tok_pad, ssm_norm_ffn[0], moe_router[0], moe_w_gate[0], moe_w_up[0], moe_w_down[0])
    y = rms(tok_pad + moe, final_norm)
    y_prompt = y[:bp * lp].reshape(bp, lp, d)
    y_sample = y[bp * lp:n_tok].reshape(bs, ls, d)
    st = lambda a: a[None]
    return (y_prompt, y_sample,
            st(cmp_p), st(cmp_s), st(sel_p), st(sel_s), st(win_p), st(win_s),
            st(gconv_p), st(gconv_s), st(gst_p), st(gst_s),
            st(sconv_p), st(sconv_s), st(sst_p), st(sst_s))
```

```python
import functools
import math

import jax
import jax.numpy as jnp
from jax import lax
from jax.experimental import pallas as pl
from jax.experimental.pallas import tpu as pltpu

D_MODEL = 2048
PAST_LEN = 16384
HEAD_DIM = 128
ROPE_THETA = 10000.0
RMS_EPS = 1e-6

GDN_HEADS = D_MODEL // 256
GDN_DK = 128
GDN_DV = 128
GDN_CONV = 4
GDN_CHUNK = 64
GDN_QK = GDN_HEADS * GDN_DK
GDN_VW = GDN_HEADS * GDN_DV
GDN_CONV_DIM = 2 * GDN_QK + GDN_VW

NSA_HEADS = D_MODEL // 256
NSA_KV_HEADS = 2
NSA_GROUP = NSA_HEADS // NSA_KV_HEADS
NSA_QW = NSA_HEADS * HEAD_DIM
NSA_KVW = NSA_KV_HEADS * HEAD_DIM
CMP_BLOCK = 32
CMP_STRIDE = 16
SEL_BLOCK = 64
SEL_TOPK = 16
WINDOW = 512
Q_BLOCK = 128
FORCE_BONUS = 1e4

HYB_SPLITS = (GDN_QK, GDN_QK, GDN_VW, GDN_VW, GDN_HEADS, GDN_HEADS,
              NSA_QW, NSA_KVW, NSA_KVW, NSA_KVW, NSA_KVW, NSA_KVW, NSA_KVW, 3 * NSA_HEADS)
HYB_IN = sum(HYB_SPLITS)

SSM_D_INNER = 2 * D_MODEL
SSM_HEAD_DIM = 64
SSM_HEADS = SSM_D_INNER // SSM_HEAD_DIM
SSM_GROUPS = 8
SSM_HPG = SSM_HEADS // SSM_GROUPS
SSM_STATE = 128
SSM_CONV = 4
SSM_CHUNK = 128
SSM_BC = SSM_GROUPS * SSM_STATE
SSM_CONV_DIM = SSM_D_INNER + 2 * SSM_BC
SSM_IN = SSM_D_INNER + SSM_CONV_DIM + SSM_HEADS

N_EXPERTS = 8
TOP_K = 2

VMEM_LIMIT_BYTES = 56 * 1024 * 1024
LANES = 128

F32 = jnp.float32
BF16 = jnp.bfloat16


def _params(*sem):
    return pltpu.CompilerParams(dimension_semantics=sem, vmem_limit_bytes=VMEM_LIMIT_BYTES)


def _pick(n, prefs):
    for p in prefs:
        if n % p == 0:
            return p
    return n


def _rms_to_bf16(x, g):
    ms = jnp.mean(x * x, axis=-1, keepdims=True)
    return (x * lax.rsqrt(ms + RMS_EPS) * g).astype(BF16)


def _rms_mm_kernel(x_ref, g_ref, w_ref, o_ref, a_scr):
    @pl.when(pl.program_id(1) == 0)
    def _():
        a_scr[...] = _rms_to_bf16(x_ref[...], g_ref[...])

    o_ref[...] = jnp.dot(a_scr[...], w_ref[...], preferred_element_type=F32).astype(o_ref.dtype)


def rms_mm(x, gain, w, out_dtype=F32):
    m, k = x.shape
    n = w.shape[1]
    tm = _pick(m, (1024, 512, 256, 128))
    tn = _pick(n, (1024, 768, 512, 256, 128))
    return pl.pallas_call(
        _rms_mm_kernel,
        grid=(m // tm, n // tn),
        in_specs=[pl.BlockSpec((tm, k), lambda i, j: (i, 0)),
                  pl.BlockSpec((1, k), lambda i, j: (0, 0)),
                  pl.BlockSpec((k, tn), lambda i, j: (0, j))],
        out_specs=pl.BlockSpec((tm, tn), lambda i, j: (i, j)),
        out_shape=jax.ShapeDtypeStruct((m, n), out_dtype),
        scratch_shapes=[pltpu.VMEM((tm, k), BF16)],
        compiler_params=_params("parallel", "arbitrary"),
        name="rms_mm",
    )(x, gain.reshape(1, k).astype(F32), w)


def _rms_glu_kernel(x_ref, g_ref, wg_ref, wu_ref, o_ref, a_scr):
    @pl.when(pl.program_id(1) == 0)
    def _():
        a_scr[...] = _rms_to_bf16(x_ref[...], g_ref[...])

    a = a_scr[...]
    gt = jnp.dot(a, wg_ref[...], preferred_element_type=F32)
    up = jnp.dot(a, wu_ref[...], preferred_element_type=F32)
    o_ref[...] = (gt * jax.nn.sigmoid(gt) * up).astype(o_ref.dtype)


def rms_glu(x, gain, wg, wu):
    m, k = x.shape
    n = wg.shape[1]
    tm = _pick(m, (1024, 512, 256, 128))
    tn = _pick(n, (512, 256, 128))
    return pl.pallas_call(
        _rms_glu_kernel,
        grid=(m // tm, n // tn),
        in_specs=[pl.BlockSpec((tm, k), lambda i, j: (i, 0)),
                  pl.BlockSpec((1, k), lambda i, j: (0, 0)),
                  pl.BlockSpec((k, tn), lambda i, j: (0, j)),
                  pl.BlockSpec((k, tn), lambda i, j: (0, j))],
        out_specs=pl.BlockSpec((tm, tn), lambda i, j: (i, j)),
        out_shape=jax.ShapeDtypeStruct((m, n), BF16),
        scratch_shapes=[pltpu.VMEM((tm, k), BF16)],
        compiler_params=_params("parallel", "arbitrary"),
        name="rms_glu",
    )(x, gain.reshape(1, k).astype(F32), wg, wu)


def _mm_resid_kernel(a_ref, w_ref, r_ref, o_ref, acc_ref, *, nk):
    kk = pl.program_id(2)

    @pl.when(kk == 0)
    def _():
        acc_ref[...] = r_ref[...]

    acc_ref[...] += jnp.dot(a_ref[...], w_ref[...], preferred_element_type=F32)

    @pl.when(kk == nk - 1)
    def _():
        o_ref[...] = acc_ref[...]


def mm_resid(a, w, resid):
    m, k = a.shape
    n = w.shape[1]
    tm = _pick(m, (1024, 512, 256, 128))
    tn = _pick(n, (1024, 512, 256, 128))
    tk = _pick(k, (2048, 1792, 1408, 1024, 512))
    nk = k // tk
    return pl.pallas_call(
        functools.partial(_mm_resid_kernel, nk=nk),
        grid=(m // tm, n // tn, nk),
        in_specs=[pl.BlockSpec((tm, tk), lambda i, j, kk: (i, kk)),
                  pl.BlockSpec((tk, tn), lambda i, j, kk: (kk, j)),
                  pl.BlockSpec((tm, tn), lambda i, j, kk: (i, j))],
        out_specs=pl.BlockSpec((tm, tn), lambda i, j, kk: (i, j)),
        out_shape=jax.ShapeDtypeStruct((m, n), F32),
        scratch_shapes=[pltpu.VMEM((tm, tn), F32)],
        compiler_params=_params("parallel", "parallel", "arbitrary"),
        name="mm_resid",
    )(a, w, resid)


def _rms_kernel(x_ref, g_ref, o_ref):
    x = x_ref[...]
    ms = jnp.mean(x * x, axis=-1, keepdims=True)
    o_ref[...] = (x * lax.rsqrt(ms + RMS_EPS) * g_ref[...]).astype(o_ref.dtype)


def rms(x, gain, out_dtype=F32):
    m, k = x.shape
    tm = _pick(m, (512, 256, 128))
    return pl.pallas_call(
        _rms_kernel,
        grid=(m // tm,),
        in_specs=[pl.BlockSpec((tm, k), lambda i: (i, 0)),
                  pl.BlockSpec((1, k), lambda i: (0, 0))],
        out_specs=pl.BlockSpec((tm, k), lambda i: (i, 0)),
        out_shape=jax.ShapeDtypeStruct((m, k), out_dtype),
        compiler_params=_params("parallel"),
        name="rms",
    )(x, gain.reshape(1, k).astype(F32))


def _rms_router_kernel(x_ref, g_ref, wh_ref, wl_ref, h_ref, lg_ref):
    x = x_ref[...]
    ms = jnp.mean(x * x, axis=-1, keepdims=True)
    h = x * lax.rsqrt(ms + RMS_EPS) * g_ref[...]
    hh = h.astype(BF16)
    hl = (h - hh.astype(F32)).astype(BF16)
    h_ref[...] = hh
    wh = wh_ref[...]
    wl = wl_ref[...]
    lg = jnp.dot(hh, wh, preferred_element_type=F32)
    lg += jnp.dot(hl, wh, preferred_element_type=F32)
    lg += jnp.dot(hh, wl, preferred_element_type=F32)
    lg_ref[...] = lg


def rms_router(x, gain, router):
    m, k = x.shape
    e = router.shape[1]
    rp = jnp.pad(router.astype(F32), ((0, 0), (0, LANES - e)))
    rh = rp.astype(BF16)
    rl = (rp - rh.astype(F32)).astype(BF16)
    tm = _pick(m, (512, 256, 128, 32))
    h, lg = pl.pallas_call(
        _rms_router_kernel,
        grid=(m // tm,),
        in_specs=[pl.BlockSpec((tm, k), lambda i: (i, 0)),
                  pl.BlockSpec((1, k), lambda i: (0, 0)),
                  pl.BlockSpec((k, LANES), lambda i: (0, 0)),
                  pl.BlockSpec((k, LANES), lambda i: (0, 0))],
        out_specs=[pl.BlockSpec((tm, k), lambda i: (i, 0)),
                   pl.BlockSpec((tm, LANES), lambda i: (i, 0))],
        out_shape=[jax.ShapeDtypeStruct((m, k), BF16), jax.ShapeDtypeStruct((m, LANES), F32)],
        compiler_params=_params("parallel"),
        name="rms_router",
    )(x, gain.reshape(1, k).astype(F32), rh, rl)
    return h, lg[:, :e]


MOE_TM = 256


def _moe_glu_kernel(te_ref, tv_ref, tf_ref, a_ref, wg_ref, wu_ref, o_ref, wg_b, wu_b):
    i = pl.program_id(1)

    @pl.when(tf_ref[i] != 0)
    def _():
        wg_b[...] = wg_ref[...].astype(BF16)
        wu_b[...] = wu_ref[...].astype(BF16)

    @pl.when(tv_ref[i] != 0)
    def _():
        a = a_ref[...]
        gt = jnp.dot(a, wg_b[...], preferred_element_type=F32)
        up = jnp.dot(a, wu_b[...], preferred_element_type=F32)
        o_ref[...] = (gt * jax.nn.sigmoid(gt) * up).astype(o_ref.dtype)

    @pl.when(tv_ref[i] == 0)
    def _():
        o_ref[...] = jnp.zeros_like(o_ref)


def _moe_down_kernel(te_ref, tv_ref, tf_ref, a_ref, w_ref, o_ref, w_b):
    i = pl.program_id(1)

    @pl.when(tf_ref[i] != 0)
    def _():
        w_b[...] = w_ref[...].astype(BF16)

    @pl.when(tv_ref[i] != 0)
    def _():
        o_ref[...] = jnp.dot(a_ref[...], w_b[...], preferred_element_type=F32)

    @pl.when(tv_ref[i] == 0)
    def _():
        o_ref[...] = jnp.zeros_like(o_ref)


def moe_experts(a_sorted, tile_expert, tile_valid, tile_first, wg, wu, wd):
    r, d = a_sorted.shape
    f = wg.shape[2]
    tm = MOE_TM
    tf = _pick(f, (1024, 512))
    act = pl.pallas_call(
        _moe_glu_kernel,
        grid_spec=pltpu.PrefetchScalarGridSpec(
            num_scalar_prefetch=3,
            grid=(f // tf, r // tm),
            in_specs=[pl.BlockSpec((tm, d), lambda j, i, te, tv, t1: (i, 0)),
                      pl.BlockSpec((None, d, tf), lambda j, i, te, tv, t1: (te[i], 0, j)),
                      pl.BlockSpec((None, d, tf), lambda j, i, te, tv, t1: (te[i], 0, j))],
            out_specs=pl.BlockSpec((tm, tf), lambda j, i, te, tv, t1: (i, j)),
            scratch_shapes=[pltpu.VMEM((d, tf), BF16), pltpu.VMEM((d, tf), BF16)],
        ),
        out_shape=jax.ShapeDtypeStruct((r, f), BF16),
        compiler_params=_params("arbitrary", "arbitrary"),
        name="moe_glu",
    )(tile_expert, tile_valid, tile_first, a_sorted, wg, wu)
    tn = _pick(d, (512,))
    return pl.pallas_call(
        _moe_down_kernel,
        grid_spec=pltpu.PrefetchScalarGridSpec(
            num_scalar_prefetch=3,
            grid=(d // tn, r // tm),
            in_specs=[pl.BlockSpec((tm, f), lambda j, i, te, tv, t1: (i, 0)),
                      pl.BlockSpec((None, f, tn), lambda j, i, te, tv, t1: (te[i], 0, j))],
            out_specs=pl.BlockSpec((tm, tn), lambda j, i, te, tv, t1: (i, j)),
            scratch_shapes=[pltpu.VMEM((f, tn), BF16)],
        ),
        out_shape=jax.ShapeDtypeStruct((r, d), F32),
        compiler_params=_params("arbitrary", "arbitrary"),
        name="moe_down",
    )(tile_expert, tile_valid, tile_first, act, wd)


def moe_ffn(x, gain, router, wg, wu, wd):
    t, d = x.shape
    tm = MOE_TM
    h, logits = rms_router(x, gain, router)
    top_v, top_i = lax.top_k(logits, TOP_K)
    top_w = jax.nn.softmax(top_v, axis=-1)
    n_asg = t * TOP_K
    n_rows = (-(-n_asg // tm) + N_EXPERTS) * tm
    flat_e = top_i.reshape(-1).astype(jnp.int32)
    order = jnp.argsort(flat_e, stable=True).astype(jnp.int32)
    sorted_e = flat_e[order]
    counts = jnp.sum(flat_e[:, None] == jnp.arange(N_EXPERTS, dtype=jnp.int32)[None, :], axis=0).astype(jnp.int32)
    padded = ((counts + tm - 1) // tm) * tm
    pad_end = jnp.cumsum(padded)
    pad_start = pad_end - padded
    start = jnp.cumsum(counts) - counts
    dest = pad_start[sorted_e] + (jnp.arange(n_asg, dtype=jnp.int32) - start[sorted_e])
    row_src = jnp.zeros((n_rows,), jnp.int32).at[dest].set(order // TOP_K)
    pos = jnp.zeros((n_asg,), jnp.int32).at[order].set(dest)
    tile_start = jnp.arange(n_rows // tm, dtype=jnp.int32) * tm
    tile_valid = (tile_start < pad_end[-1]).astype(jnp.int32)
    tile_expert = jnp.minimum(jnp.searchsorted(pad_end, tile_start, side="right"), N_EXPERTS - 1).astype(jnp.int32)
    last_e = tile_expert[jnp.maximum(pad_end[-1] // tm - 1, 0)]
    tile_expert = jnp.where(tile_valid != 0, tile_expert, last_e)
    tile_first = jnp.concatenate([jnp.ones((1,), jnp.int32),
                                  (tile_expert[1:] != tile_expert[:-1]).astype(jnp.int32)])
    a_sorted = jnp.take(h, row_src, axis=0)
    y = moe_experts(a_sorted, tile_expert, tile_valid, tile_first, wg, wu, wd)
    yk = jnp.take(y, pos, axis=0).reshape(t, TOP_K, d)
    return jnp.sum(yk * top_w[..., None], axis=1)


NEG_BIG = -1e30
NSA_TQ = 128
SEL_CHUNK = 512
CMP_HIDDEN = 256


def _dot_nt(a, b):
    return lax.dot_general(a, b, (((1,), (1,)), ((), ())), preferred_element_type=F32)


def _cmp_mlp_kernel(x_ref, pe_ref, w1_ref, w2_ref, o_ref, hi_scr, *, n_seg):
    lo = jnp.zeros((n_seg, CMP_HIDDEN), F32)
    hi = jnp.zeros((n_seg, CMP_HIDDEN), F32)
    for r in range(CMP_STRIDE):
        xr = x_ref[pl.ds(r, n_seg, stride=CMP_STRIDE), :]
        a_lo = (xr + pe_ref[r:r + 1, :]).astype(BF16)
        a_hi = (xr + pe_ref[CMP_STRIDE + r:CMP_STRIDE + r + 1, :]).astype(BF16)
        lo += jnp.dot(a_lo, w1_ref[r * HEAD_DIM:(r + 1) * HEAD_DIM, :], preferred_element_type=F32)
        hi += jnp.dot(a_hi, w1_ref[(CMP_STRIDE + r) * HEAD_DIM:(CMP_STRIDE + r + 1) * HEAD_DIM, :],
                      preferred_element_type=F32)
    hi_scr[0:n_seg, :] = hi
    hi_scr[n_seg:n_seg + 8, :] = jnp.zeros((8, CMP_HIDDEN), F32)
    pre = lo + hi_scr[pl.ds(1, n_seg), :]
    hid = pre * jax.nn.sigmoid(pre)
    o_ref[...] = jnp.dot(hid.astype(BF16), w2_ref[...], preferred_element_type=F32)


def cmp_mlp_prompt(rows2d, pe, w1b, w2b):
    b, l, _ = rows2d.shape
    n_seg = l // CMP_STRIDE
    return pl.pallas_call(
        functools.partial(_cmp_mlp_kernel, n_seg=n_seg),
        grid=(b, 2, NSA_KV_HEADS),
        in_specs=[pl.BlockSpec((None, l, HEAD_DIM), lambda i, s, k: (i, 0, s * NSA_KV_HEADS + k)),
                  pl.BlockSpec((None, CMP_BLOCK, HEAD_DIM), lambda i, s, k: (s, 0, 0)),
                  pl.BlockSpec((None, CMP_BLOCK * HEAD_DIM, CMP_HIDDEN), lambda i, s, k: (s, 0, 0)),
                  pl.BlockSpec((None, CMP_HIDDEN, HEAD_DIM), lambda i, s, k: (s, 0, 0))],
        out_specs=pl.BlockSpec((None, None, None, n_seg, HEAD_DIM), lambda i, s, k: (i, s, k, 0, 0)),
        out_shape=jax.ShapeDtypeStruct((b, 2, NSA_KV_HEADS, n_seg, HEAD_DIM), F32),
        scratch_shapes=[pltpu.VMEM((n_seg + 8, CMP_HIDDEN), F32)],
        compiler_params=_params("parallel", "parallel", "parallel"),
        name="cmp_mlp_prompt",
    )(rows2d, pe, w1b, w2b)


def _nsa_prompt_kernel(q_ref, kc_ref, vc_ref, ks_ref, vs_ref, kw_ref, vw_ref, g_ref, o_ref,
                       m_scr, acc_scr, o_scr, sc_scr, *, tq, n_cmp, n_sel):
    t = pl.program_id(2)
    kvh = pl.program_id(1)
    ng = NSA_GROUP
    rows = ng * tq
    q = q_ref[...].reshape(rows, HEAD_DIM)
    qpos = t * tq + lax.broadcasted_iota(jnp.int32, (tq, 1), 0)

    def tile_rows(x):
        return jnp.concatenate([x] * ng, axis=0)

    lane = lax.broadcasted_iota(jnp.int32, (LANES, 3 * ng * LANES), 0)
    slot = lax.broadcasted_iota(jnp.int32, (LANES, 3 * ng * LANES), 1) // LANES
    pick = (lane == NSA_GATE_LANE + (slot // ng) * NSA_HEADS + kvh * ng + slot % ng).astype(BF16)
    gate_all = _dot_exact01_right(g_ref[...], pick)

    def gate_rows(branch):
        return jnp.concatenate([gate_all[:, (branch * ng + h) * LANES:(branch * ng + h + 1) * LANES]
                                for h in range(ng)], axis=0)

    def reset():
        m_scr[...] = jnp.full(m_scr.shape, NEG_BIG, F32)
        acc_scr[...] = jnp.zeros(acc_scr.shape, F32)

    def update(k, v, mask):
        s = _dot_nt(q, k)
        mask4 = tile_rows(mask)
        sm = jnp.where(mask4, s, NEG_BIG)
        m_prev = m_scr[:, :1]
        m_new = jnp.maximum(m_prev, jnp.max(sm, axis=1, keepdims=True))
        e = jnp.where(mask4, jnp.exp(sm - m_new), 0.0)
        alpha = jnp.exp(m_prev - m_new)
        m_scr[...] = jnp.broadcast_to(m_new, m_scr.shape)
        v1 = jnp.concatenate([v, jnp.ones_like(v)], axis=1)
        acc_scr[...] = alpha * acc_scr[...] + jnp.dot(e.astype(BF16), v1, preferred_element_type=F32)

    def finish():
        den = acc_scr[:, HEAD_DIM:]
        return acc_scr[:, :HEAD_DIM] / jnp.where(den > 0, den, 1.0)

    n_cp = kc_ref.shape[0]
    kc = kc_ref[...].astype(BF16)
    vc = vc_ref[...].astype(BF16)
    s = _dot_nt(q, kc)
    cidx = lax.broadcasted_iota(jnp.int32, (1, n_cp), 1)
    mask_c = tile_rows((cidx * CMP_STRIDE + (CMP_BLOCK - 1) <= qpos) & (cidx < n_cmp))
    sm = jnp.where(mask_c, s, NEG_BIG)
    e = jnp.where(mask_c, jnp.exp(sm - jnp.max(sm, axis=1, keepdims=True)), 0.0)
    den = jnp.sum(e, axis=1, keepdims=True)
    p = e / jnp.where(den > 0, den, 1.0)
    o_scr[...] = gate_rows(0) * jnp.dot(p.astype(BF16), vc, preferred_element_type=F32)
    p_sum = p[0:tq]
    for h in range(1, ng):
        p_sum = p_sum + p[h * tq:(h + 1) * tq]
    p_hi = p_sum.astype(BF16)
    p_lo = (p_sum - p_hi.astype(F32)).astype(BF16)
    cs = lax.broadcasted_iota(jnp.int32, (n_cp, LANES), 0) * CMP_STRIDE
    ss = lax.broadcasted_iota(jnp.int32, (n_cp, LANES), 1) * SEL_BLOCK
    overlap = ((cs < ss + SEL_BLOCK) & (cs + CMP_BLOCK > ss) & (cs < n_cmp * CMP_STRIDE)).astype(BF16)
    imp = jnp.dot(p_hi, overlap, preferred_element_type=F32) + jnp.dot(p_lo, overlap, preferred_element_type=F32)

    blk = lax.broadcasted_iota(jnp.int32, (tq, LANES), 1)
    cur = qpos // SEL_BLOCK
    forced = (blk == 0) | (blk == cur) | (blk == cur - 1)
    score = jnp.where((blk <= cur) & (blk < n_sel), imp + jnp.where(forced, FORCE_BONUS, 0.0), NEG_BIG)
    n_sp = sc_scr.shape[0]
    s_t = score.T[0:n_sp]
    sc_scr[...] = s_t
    jidx = lax.broadcasted_iota(jnp.int32, (n_sp, tq), 0)
    rank = jnp.zeros((n_sp, tq), jnp.int32)
    for i in range(n_sel):
        si = sc_scr[i:i + 1, :]
        rank += ((si > s_t) | ((si == s_t) & (i < jidx))).astype(jnp.int32)
    sel_t = ((rank < SEL_TOPK) & (s_t > 0.5 * NEG_BIG)).astype(F32)
    if n_sp < LANES:
        sel_t = jnp.concatenate([sel_t, jnp.zeros((LANES - n_sp, tq), F32)], axis=0)
    sel = sel_t.T.astype(BF16)

    reset()

    def sel_step(c, carry):
        start = pl.multiple_of(c * SEL_CHUNK, SEL_CHUNK)
        jrow = lax.broadcasted_iota(jnp.int32, (LANES, SEL_CHUNK), 0)
        kblk = (start + lax.broadcasted_iota(jnp.int32, (LANES, SEL_CHUNK), 1)) // SEL_BLOCK
        expand = (jrow == kblk).astype(BF16)
        chosen = jnp.dot(sel, expand, preferred_element_type=F32) > 0.5
        kpos = start + lax.broadcasted_iota(jnp.int32, (1, SEL_CHUNK), 1)
        update(ks_ref[pl.ds(start, SEL_CHUNK), :], vs_ref[pl.ds(start, SEL_CHUNK), :], chosen & (kpos <= qpos))
        return carry

    lax.fori_loop(0, (t * tq + tq + SEL_CHUNK - 1) // SEL_CHUNK, sel_step, 0)
    o_scr[...] += gate_rows(1) * finish()

    reset()
    span = WINDOW + tq
    start = pl.multiple_of(jnp.maximum(t - WINDOW // tq, 0) * tq, tq)
    dist = qpos - (start + lax.broadcasted_iota(jnp.int32, (1, span), 1))
    update(kw_ref[pl.ds(start, span), :], vw_ref[pl.ds(start, span), :], (dist >= 0) & (dist < WINDOW))
    out = o_scr[...] + gate_rows(2) * finish()
    for h in range(ng):
        o_ref[:, h * HEAD_DIM:(h + 1) * HEAD_DIM] = out[h * tq:(h + 1) * tq]


def nsa_prompt(q, cmp2d, kvb, sig, pe, w1b, w2b):
    b, _, l, _ = q.shape
    tq = NSA_TQ
    n_cmp = (l - CMP_BLOCK) // CMP_STRIDE + 1
    n_sel = -(-l // SEL_BLOCK)
    kvc = cmp_mlp_prompt(cmp2d, pe, w1b, w2b)
    n_cp = kvc.shape[3]
    n_sp = -(-n_sel // 8) * 8
    rows = NSA_GROUP * tq
    kv_spec = lambda s: pl.BlockSpec((None, None, None, l, HEAD_DIM), lambda i, k, t: (i, s, k, 0, 0))
    return pl.pallas_call(
        functools.partial(_nsa_prompt_kernel, tq=tq, n_cmp=n_cmp, n_sel=n_sel),
        grid=(b, NSA_KV_HEADS, l // tq),
        in_specs=[pl.BlockSpec((None, NSA_GROUP, tq, HEAD_DIM), lambda i, k, t: (i, k, t, 0)),
                  pl.BlockSpec((None, None, None, n_cp, HEAD_DIM), lambda i, k, t: (i, 0, k, 0, 0)),
                  pl.BlockSpec((None, None, None, n_cp, HEAD_DIM), lambda i, k, t: (i, 1, k, 0, 0)),
                  kv_spec(0), kv_spec(1), kv_spec(2), kv_spec(3),
                  pl.BlockSpec((None, tq, LANES), lambda i, k, t: (i, t, 0))],
        out_specs=pl.BlockSpec((None, tq, NSA_GROUP * HEAD_DIM), lambda i, k, t: (i, t, k)),
        out_shape=jax.ShapeDtypeStruct((b, l, NSA_QW), F32),
        scratch_shapes=[pltpu.VMEM((rows, LANES), F32), pltpu.VMEM((rows, 2 * HEAD_DIM), F32),
                        pltpu.VMEM((rows, HEAD_DIM), F32), pltpu.VMEM((n_sp, tq), F32)],
        compiler_params=_params("parallel", "parallel", "arbitrary"),
        name="nsa_prompt",
    )(q, kvc, kvc, kvb, kvb, kvb, kvb, sig)


HYB_ORDER = (0, 1, 2, 3, 6, 7, 8, 9, 10, 11, 12, 4, 5, 13)
HYB_QKV = 0
HYB_Z_BLK = 3
HYB_NQ = 2 * GDN_QK + 2 * GDN_VW
HYB_KV = HYB_NQ + NSA_QW
HYB_SMALL = HYB_KV + 6 * NSA_KVW
GDN_BETA_LANE = GDN_HEADS
NSA_GATE_LANE = 2 * GDN_HEADS
PREP_ROWS = 256


def reorder_hyb_cols(w):
    offs = [0]
    for n in HYB_SPLITS:
        offs.append(offs[-1] + n)
    return jnp.concatenate([w[:, offs[i]:offs[i + 1]] for i in HYB_ORDER], axis=1)


def _softplus(x):
    return jnp.maximum(x, 0.0) + jnp.log(1.0 + jnp.exp(-jnp.abs(x)))


def _hyb_prep_kernel(p_ref, cw_ref, alog_ref, dtb_ref, cos_ref, sin_ref, hist_ref,
                     qa_ref, ka_ref, va_ref, g_ref, sig_ref, qn_ref, cmp_ref, sel_ref, win_ref, kvb_ref,
                     ext_scr, *, tr):
    kw = GDN_CONV
    off = 8 - (kw - 1)

    @pl.when(pl.program_id(1) == 0)
    def _():
        ext_scr[0:8, :] = hist_ref[...]

    ext_scr[8:8 + tr, :] = p_ref[:, HYB_QKV:HYB_QKV + GDN_CONV_DIM]
    outs = (qa_ref, ka_ref, va_ref)
    for seg in range(3):
        for h in range(GDN_HEADS):
            lo = seg * GDN_QK + h * GDN_DK
            acc = cw_ref[0:1, lo:lo + GDN_DK] * ext_scr[pl.ds(off, tr), lo:lo + GDN_DK]
            for j in range(1, kw):
                acc = acc + cw_ref[j:j + 1, lo:lo + GDN_DK] * ext_scr[pl.ds(off + j, tr), lo:lo + GDN_DK]
            y = _silu(acc)
            if seg < 2:
                y = y * lax.rsqrt(jnp.sum(y * y, axis=-1, keepdims=True) + 1e-6)
            if seg == 0:
                y = y * GDN_DK ** -0.5
            outs[seg][:, h * GDN_DK:(h + 1) * GDN_DK] = y
    ext_scr[0:8, :] = ext_scr[tr:tr + 8, :]

    small = p_ref[:, HYB_SMALL:HYB_SMALL + LANES]
    g_ref[...] = -jnp.exp(alog_ref[...]) * _softplus(small + dtb_ref[...])
    sig_ref[...] = jax.nn.sigmoid(small)

    cos = cos_ref[...]
    sin = sin_ref[...]

    def rope(x):
        return x * cos + pltpu.roll(x, HEAD_DIM // 2, axis=1) * sin

    for h in range(NSA_HEADS):
        x = p_ref[:, HYB_NQ + h * HEAD_DIM:HYB_NQ + (h + 1) * HEAD_DIM]
        qn_ref[h] = (rope(x) * HEAD_DIM ** -0.5).astype(BF16)
    for i, o_ref in enumerate((cmp_ref, sel_ref, win_ref)):
        for kh in range(NSA_KV_HEADS):
            k0 = HYB_KV + 2 * i * NSA_KVW + kh * HEAD_DIM
            kr = rope(p_ref[:, k0:k0 + HEAD_DIM])
            vr = p_ref[:, k0 + NSA_KVW:k0 + NSA_KVW + HEAD_DIM]
            o_ref[:, kh * HEAD_DIM:(kh + 1) * HEAD_DIM] = kr
            o_ref[:, NSA_KVW + kh * HEAD_DIM:NSA_KVW + (kh + 1) * HEAD_DIM] = vr
            if i > 0:
                kvb_ref[2 * (i - 1), kh] = kr.astype(BF16)
                kvb_ref[2 * (i - 1) + 1, kh] = vr.astype(BF16)


def hyb_prep(proj, conv_w, a_log, dt_bias, pos, hist):
    b, l, _ = proj.shape
    tr = PREP_ROWS
    half = HEAD_DIM // 2
    inv_freq = 1.0 / (ROPE_THETA ** (jnp.arange(half, dtype=F32) / half))
    ang = pos.astype(F32)[:, None] * inv_freq[None, :]
    cos = jnp.concatenate([jnp.cos(ang), jnp.cos(ang)], axis=1)
    sin = jnp.concatenate([-jnp.sin(ang), jnp.sin(ang)], axis=1)
    lane_pad = lambda v: jnp.pad(v.astype(F32).reshape(1, -1), ((0, 0), (0, LANES - v.shape[0])))
    row = lambda w: pl.BlockSpec((None, tr, w), lambda i, n: (i, n, 0))
    f32 = lambda w: jax.ShapeDtypeStruct((b, l, w), F32)
    return pl.pallas_call(
        functools.partial(_hyb_prep_kernel, tr=tr),
        grid=(b, l // tr),
        in_specs=[row(proj.shape[2]),
                  pl.BlockSpec((GDN_CONV, GDN_CONV_DIM), lambda i, n: (0, 0)),
                  pl.BlockSpec((1, LANES), lambda i, n: (0, 0)),
                  pl.BlockSpec((1, LANES), lambda i, n: (0, 0)),
                  pl.BlockSpec((tr, HEAD_DIM), lambda i, n: (n, 0)),
                  pl.BlockSpec((tr, HEAD_DIM), lambda i, n: (n, 0)),
                  pl.BlockSpec((None, 8, GDN_CONV_DIM), lambda i, n: (i, 0, 0))],
        out_specs=[row(GDN_QK), row(GDN_QK), row(GDN_VW), row(LANES), row(LANES),
                   pl.BlockSpec((None, NSA_HEADS, tr, HEAD_DIM), lambda i, n: (i, 0, n, 0)),
                   row(2 * NSA_KVW), row(2 * NSA_KVW), row(2 * NSA_KVW),
                   pl.BlockSpec((None, 4, NSA_KV_HEADS, tr, HEAD_DIM), lambda i, n: (i, 0, 0, n, 0))],
        out_shape=[f32(GDN_QK), f32(GDN_QK), f32(GDN_VW), f32(LANES), f32(LANES),
                   jax.ShapeDtypeStruct((b, NSA_HEADS, l, HEAD_DIM), BF16),
                   f32(2 * NSA_KVW), f32(2 * NSA_KVW), f32(2 * NSA_KVW),
                   jax.ShapeDtypeStruct((b, 4, NSA_KV_HEADS, l, HEAD_DIM), BF16)],
        scratch_shapes=[pltpu.VMEM((tr + 8, GDN_CONV_DIM), F32)],
        compiler_params=_params("parallel", "arbitrary"),
        name="hyb_prep",
    )(proj, conv_w.astype(F32), lane_pad(a_log), lane_pad(dt_bias), cos, sin, hist)


def _dot_exact01(a01, x):
    hi = x.astype(BF16)
    r = x - hi.astype(F32)
    mid = r.astype(BF16)
    lo = (r - mid.astype(F32)).astype(BF16)
    return (jnp.dot(a01, hi, preferred_element_type=F32) + jnp.dot(a01, mid, preferred_element_type=F32)
            + jnp.dot(a01, lo, preferred_element_type=F32))


def _gdn_kernel(q_ref, k_ref, v_ref, g_ref, b_ref, z_ref, nw_ref, s0_ref, o_ref, s_ref, *, c, g_off, b_off):
    @pl.when(pl.program_id(1) == 0)
    def _():
        s_ref[...] = s0_ref[...]

    ii = lax.broadcasted_iota(jnp.int32, (c, c), 0)
    jj = lax.broadcasted_iota(jnp.int32, (c, c), 1)
    incl = ii >= jj
    strict = ii > jj
    t01 = incl.astype(BF16)
    w01 = strict.astype(F32)
    g_all = g_ref[...]
    b_all = b_ref[...]
    nw = nw_ref[...]
    hs = range(GDN_HEADS)
    sl = [slice(h * GDN_DV, (h + 1) * GDN_DV) for h in hs]
    dot = functools.partial(jnp.dot, preferred_element_type=F32)
    bf = lambda xs: [x.astype(BF16) for x in xs]
    q = [q_ref[:, x] for x in sl]
    k = [k_ref[:, x] for x in sl]
    v = [v_ref[:, x] for x in sl]
    g_b = [jnp.broadcast_to(g_all[:, g_off + h:g_off + h + 1], (c, GDN_DK)) for h in hs]
    b_b = [jnp.broadcast_to(b_all[:, b_off + h:b_off + h + 1], (c, GDN_DK)) for h in hs]
    gam = [_dot_exact01(t01, x) for x in g_b]
    decay = [jnp.exp(_dot_exact01(t01, x[:, :c] * w01)) for x in g_b]
    kb = [x * y for x, y in zip(k, b_b)]
    kbf = bf(k)
    n_mat = [jnp.where(strict, _dot_nt(x, y) * d, 0.0) for x, y, d in zip(bf(kb), kbf, decay)]
    e_gam = [jnp.exp(x) for x in gam]
    y = [jnp.concatenate([vv * bb, kk * ee], axis=1) for vv, bb, kk, ee in zip(v, b_b, kb, e_gam)]
    y = [yy - dot(nn, yb) for yy, nn, yb in zip(y, bf(n_mat), bf(y))]
    p = n_mat
    span = 2
    while span < c:
        pb = bf(p)
        p = [dot(x, x) for x in pb]
        y = [yy + dot(pp, yb) for yy, pp, yb in zip(y, bf(p), bf(y))]
        span *= 2
    s = [s_ref[h] for h in hs]
    sb = bf(s)
    v_new = [yy[:, :GDN_DV] - dot(yy[:, GDN_DV:].astype(BF16), ss) for yy, ss in zip(y, sb)]
    vnb = bf(v_new)
    attn = [jnp.where(incl, _dot_nt(x, y) * d, 0.0) for x, y, d in zip(bf(q), kbf, decay)]
    o = [dot((qq * ee).astype(BF16), ss) + dot(aa, vv)
         for qq, ee, ss, aa, vv in zip(q, e_gam, sb, bf(attn), vnb)]
    g_end = [x[c - 1:c, :] for x in gam]
    k_dec = [(kk * jnp.exp(ge - gg)).astype(BF16) for kk, ge, gg in zip(k, g_end, gam)]
    for h in hs:
        s_ref[h] = s[h] * jnp.exp(g_end[h]) + lax.dot_general(
            k_dec[h], vnb[h], (((0,), (0,)), ((), ())), preferred_element_type=F32)
        on = o[h] * lax.rsqrt(jnp.mean(o[h] * o[h], axis=-1, keepdims=True) + RMS_EPS) * nw
        o_ref[:, sl[h]] = on * _silu(z_ref[:, sl[h]])


def gdn_gated(q, k, v, g, beta, z, norm_w, s0, g_off=0, b_off=0, z_blk=0):
    b, l, _ = q.shape
    c = GDN_CHUNK
    row = lambda w: pl.BlockSpec((None, c, w), lambda i, n: (i, n, 0))
    st = pl.BlockSpec((None, GDN_HEADS, GDN_DK, GDN_DV), lambda i, n: (i, 0, 0, 0))
    return pl.pallas_call(
        functools.partial(_gdn_kernel, c=c, g_off=g_off, b_off=b_off),
        grid=(b, l // c),
        in_specs=[row(GDN_QK), row(GDN_QK), row(GDN_VW), row(g.shape[2]), row(beta.shape[2]),
                  pl.BlockSpec((None, c, GDN_VW), lambda i, n: (i, n, z_blk)),
                  pl.BlockSpec((1, GDN_DV), lambda i, n: (0, 0)), st],
        out_specs=[row(GDN_VW), st],
        out_shape=[jax.ShapeDtypeStruct((b, l, GDN_VW), F32),
                   jax.ShapeDtypeStruct((b, GDN_HEADS, GDN_DK, GDN_DV), F32)],
        compiler_params=_params("parallel", "arbitrary"),
        name="gdn_chunked",
    )(q, k, v, g, beta, z, norm_w.reshape(1, GDN_DV).astype(F32), s0.astype(F32))


def _split3(x):
    hi = x.astype(BF16)
    r = x - hi.astype(F32)
    mid = r.astype(BF16)
    lo = (r - mid.astype(F32)).astype(BF16)
    return hi, mid, lo


def _dot_exact01_right(x, b01):
    hi, mid, lo = _split3(x)
    return (jnp.dot(hi, b01, preferred_element_type=F32) + jnp.dot(mid, b01, preferred_element_type=F32)
            + jnp.dot(lo, b01, preferred_element_type=F32))


def _silu(x):
    return x * jax.nn.sigmoid(x)


def _ssd_kernel(z_ref, x_ref, b_ref, c_ref, dt_ref, cw_ref, cb_ref, an_ref, dtb_ref, dsk_ref, nw_ref,
                r128_ref, r64_ref, hist_ref, h0_ref, y_ref, h_ref, ex_scr, eb_scr, ec_scr, *, c, n_valid):
    n = pl.program_id(1)
    kw = SSM_CONV
    off = 8 - (kw - 1)
    gw = SSM_HPG * SSM_HEAD_DIM

    @pl.when(n == 0)
    def _():
        h_ref[...] = h0_ref[...]
        ex_scr[0:8, :] = hist_ref[:, 0:SSM_D_INNER]
        eb_scr[0:8, :] = hist_ref[:, SSM_D_INNER:SSM_D_INNER + SSM_BC]
        ec_scr[0:8, :] = hist_ref[:, SSM_D_INNER + SSM_BC:SSM_CONV_DIM]

    ex_scr[8:8 + c, :] = x_ref[...]
    eb_scr[8:8 + c, :] = b_ref[...]
    ec_scr[8:8 + c, :] = c_ref[...]

    def conv(scr, col0, lo, width):
        acc = cb_ref[:, col0 + lo:col0 + lo + width]
        for j in range(kw):
            acc = acc + cw_ref[j:j + 1, col0 + lo:col0 + lo + width] * scr[pl.ds(off + j, c), lo:lo + width]
        return _silu(acc)

    ii = lax.broadcasted_iota(jnp.int32, (c, c), 0)
    jj = lax.broadcasted_iota(jnp.int32, (c, c), 1)
    incl = ii >= jj
    t01 = incl.astype(BF16)
    x_dt = dt_ref[...] + dtb_ref[...]
    dtv = jnp.maximum(x_dt, 0.0) + jnp.log(1.0 + jnp.exp(-jnp.abs(x_dt)))
    if n_valid is not None:
        row = n * c + lax.broadcasted_iota(jnp.int32, dtv.shape, 0)
        dtv = jnp.where(row < n_valid, dtv, 0.0)
    gam = _dot_exact01(t01, dtv * an_ref[...])
    gam_t = gam.T
    for g in range(SSM_GROUPS):
        r64 = r64_ref[:, g * gw:(g + 1) * gw]
        r128 = r128_ref[:, g * SSM_HPG * LANES:(g + 1) * SSM_HPG * LANES]
        xg = conv(ex_scr, 0, g * gw, gw)
        bg = conv(eb_scr, SSM_D_INNER, g * SSM_STATE, SSM_STATE).astype(BF16)
        cg = conv(ec_scr, SSM_D_INNER + SSM_BC, g * SSM_STATE, SSM_STATE).astype(BF16)
        gam64 = _dot_exact01_right(gam, r64)
        gam128 = _dot_exact01_right(gam, r128)
        xdt = xg * _dot_exact01_right(dtv, r64)
        xdt_b = xdt.astype(BF16)
        cb = _dot_nt(cg, bg)
        parts = []
        for hl in range(SSM_HPG):
            h = g * SSM_HPG + hl
            dec = jnp.where(incl, jnp.exp(gam128[:, hl * LANES:(hl + 1) * LANES] - gam_t[h:h + 1, :]), 0.0)
            parts.append(jnp.dot((cb * dec).astype(BF16), xdt_b[:, hl * SSM_HEAD_DIM:(hl + 1) * SSM_HEAD_DIM],
                                 preferred_element_type=F32))
        hs = h_ref[:, g * gw:(g + 1) * gw]
        y = (jnp.concatenate(parts, axis=1)
             + jnp.dot(cg, hs.astype(BF16), preferred_element_type=F32) * jnp.exp(gam64))
        g_end = gam64[c - 1:c, :]
        xdec = (xdt * jnp.exp(g_end - gam64)).astype(BF16)
        h_ref[:, g * gw:(g + 1) * gw] = hs * jnp.exp(g_end) + lax.dot_general(
            bg, xdec, (((0,), (0,)), ((), ())), preferred_element_type=F32)
        y = (y + dsk_ref[:, g * gw:(g + 1) * gw] * xg) * _silu(z_ref[:, g * gw:(g + 1) * gw])
        y = y * lax.rsqrt(jnp.mean(y * y, axis=-1, keepdims=True) + RMS_EPS) * nw_ref[:, g * gw:(g + 1) * gw]
        y_ref[:, g * gw:(g + 1) * gw] = y.astype(y_ref.dtype)

    ex_scr[0:8, :] = ex_scr[c:c + 8, :]
    eb_scr[0:8, :] = eb_scr[c:c + 8, :]
    ec_scr[0:8, :] = ec_scr[c:c + 8, :]


def ssd_fused(proj, n_valid, conv_w, conv_b, a_log, dt_bias, d_skip, norm_w, conv_buf, h0):
    b, lp, _ = proj.shape
    c = SSM_CHUNK
    f = lambda a: a.astype(F32)
    lane_pad = lambda v: jnp.pad(f(v).reshape(1, -1), ((0, 0), (0, LANES - v.shape[0])))
    heads = jnp.arange(LANES)[:, None]
    r128 = (heads == jnp.arange(SSM_HEADS * LANES)[None, :] // LANES).astype(BF16)
    r64 = (heads == jnp.arange(SSM_D_INNER)[None, :] // SSM_HEAD_DIM).astype(BF16)
    hist = jnp.pad(f(conv_buf), ((0, 0), (8 - (SSM_CONV - 1), 0), (0, 0)))
    h0_t = f(h0).reshape(b, SSM_D_INNER, SSM_STATE).transpose(0, 2, 1)
    col = lambda w, blk: pl.BlockSpec((None, c, w), lambda i, n: (i, n, blk))
    full = lambda shape: pl.BlockSpec(shape, lambda i, n: (0,) * len(shape))
    per_seq = lambda shape: pl.BlockSpec((None,) + shape, lambda i, n: (i,) + (0,) * len(shape))
    y, h_t = pl.pallas_call(
        functools.partial(_ssd_kernel, c=c, n_valid=None if n_valid == lp else n_valid),
        grid=(b, lp // c),
        in_specs=[col(SSM_D_INNER, 0), col(SSM_D_INNER, 1),
                  col(SSM_BC, 2 * SSM_D_INNER // SSM_BC), col(SSM_BC, 2 * SSM_D_INNER // SSM_BC + 1),
                  col(LANES, (2 * SSM_D_INNER + 2 * SSM_BC) // LANES),
                  full((SSM_CONV, SSM_CONV_DIM)), full((1, SSM_CONV_DIM)), full((1, LANES)), full((1, LANES)),
                  full((1, SSM_D_INNER)), full((1, SSM_D_INNER)),
                  full((LANES, SSM_HEADS * LANES)), full((LANES, SSM_D_INNER)),
                  per_seq((8, SSM_CONV_DIM)), per_seq((SSM_STATE, SSM_D_INNER))],
        out_specs=[col(SSM_D_INNER, 0), per_seq((SSM_STATE, SSM_D_INNER))],
        out_shape=[jax.ShapeDtypeStruct((b, lp, SSM_D_INNER), BF16),
                   jax.ShapeDtypeStruct((b, SSM_STATE, SSM_D_INNER), F32)],
        scratch_shapes=[pltpu.VMEM((c + 8, SSM_D_INNER), F32), pltpu.VMEM((c + 8, SSM_BC), F32),
                        pltpu.VMEM((c + 8, SSM_BC), F32)],
        compiler_params=_params("parallel", "arbitrary"),
        name="ssd_fused",
    )(proj, proj, proj, proj, proj, f(conv_w), f(conv_b).reshape(1, -1), lane_pad(-jnp.exp(f(a_log))),
      lane_pad(dt_bias), jnp.repeat(f(d_skip), SSM_HEAD_DIM).reshape(1, -1), f(norm_w).reshape(1, -1),
      r128, r64, hist, h0_t)
    return y, h_t.transpose(0, 2, 1).reshape(b, SSM_HEADS, SSM_HEAD_DIM, SSM_STATE)


def rmsnorm(x, w):
    xf = x.astype(F32)
    y = xf * lax.rsqrt(jnp.mean(xf * xf, axis=-1, keepdims=True) + RMS_EPS)
    return (y * w.astype(F32)).astype(x.dtype)


def l2norm(x):
    xf = x.astype(F32)
    return (xf * lax.rsqrt(jnp.sum(xf * xf, axis=-1, keepdims=True) + 1e-6)).astype(x.dtype)


def rope(x, pos):
    half = x.shape[-1] // 2
    inv_freq = 1.0 / (ROPE_THETA ** (jnp.arange(half, dtype=F32) / half))
    ang = pos.astype(F32)[:, None] * inv_freq[None, :]
    cos = jnp.cos(ang)[None, :, None, :]
    sin = jnp.sin(ang)[None, :, None, :]
    xf = x.astype(F32)
    x1, x2 = xf[..., :half], xf[..., half:]
    return jnp.concatenate([x1 * cos - x2 * sin, x2 * cos + x1 * sin], axis=-1).astype(x.dtype)


def split_cols(a, sizes):
    out, s = [], 0
    for n in sizes:
        out.append(a[..., s:s + n])
        s += n
    return out


def causal_dwconv(x_ext, w):
    c = x_ext.shape[-1]
    return lax.conv_general_dilated(x_ext, w[:, None, :], window_strides=(1,), padding='VALID',
                                    dimension_numbers=('NWC', 'WIO', 'NWC'), feature_group_count=c)


def masked_softmax(s, mask):
    s = jnp.where(mask, s, -jnp.inf)
    m = jnp.max(s, axis=-1, keepdims=True)
    m = jnp.where(jnp.isfinite(m), m, 0.0)
    e = jnp.where(mask, jnp.exp(s - m), 0.0)
    den = jnp.sum(e, axis=-1, keepdims=True)
    return e / jnp.where(den > 0, den, 1.0)


def compress_blocks(rows, pe, w1, w2):
    b, lk = rows.shape[:2]
    n_cmp = (lk - CMP_BLOCK) // CMP_STRIDE + 1
    per = CMP_BLOCK // CMP_STRIDE
    segs = rows[:, :(n_cmp + per - 1) * CMP_STRIDE].reshape(
        b, n_cmp + per - 1, CMP_STRIDE, 2, NSA_KV_HEADS, HEAD_DIM)
    blocks = jnp.concatenate([segs[:, i:i + n_cmp] for i in range(per)], axis=2)
    blocks = blocks + jnp.transpose(pe, (1, 0, 2))[:, :, None, :]
    flat = jnp.transpose(blocks, (0, 1, 3, 4, 2, 5)).reshape(b, n_cmp, 2, NSA_KV_HEADS, CMP_BLOCK * HEAD_DIM)
    hid = jax.nn.silu(jnp.einsum('bcskf,sfh->bcskh', flat, w1))
    out = jnp.einsum('bcskh,she->bcske', hid, w2)
    c_end = jnp.arange(n_cmp) * CMP_STRIDE + CMP_BLOCK - 1
    return out[:, :, 0], out[:, :, 1], c_end


def selection_blocks(rows):
    b, lk = rows.shape[:2]
    n_sel = -(-lk // SEL_BLOCK)
    rows = jnp.pad(rows, ((0, 0), (0, n_sel * SEL_BLOCK - lk), (0, 0), (0, 0), (0, 0)))
    blk = rows.reshape(b, n_sel, SEL_BLOCK, 2, NSA_KV_HEADS, HEAD_DIM).transpose(3, 0, 4, 1, 2, 5)
    return blk[0], blk[1]


def cmp_to_sel_map(n_cmp, n_sel):
    cs = jnp.arange(n_cmp)[:, None] * CMP_STRIDE
    ss = jnp.arange(n_sel)[None, :] * SEL_BLOCK
    return ((cs < ss + SEL_BLOCK) & (cs + CMP_BLOCK > ss)).astype(F32)


def nsa_core(q, q_pos, kc, vc, c_end, ks_blk, vs_blk, kw, vw, kw_pos, gates):
    b, lq = q.shape[:2]
    qg = q.astype(F32).reshape(b, lq, NSA_KV_HEADS, NSA_GROUP, HEAD_DIM) * HEAD_DIM ** -0.5
    s_c = jnp.einsum('bqkgd,bckd->bqkgc', qg, kc.astype(F32))
    m_c = c_end[None, :] <= q_pos[:, None]
    p_c = masked_softmax(s_c, m_c[None, :, None, None, :])
    o_c = jnp.einsum('bqkgc,bckd->bqkgd', p_c, vc.astype(F32))
    n_cmp, n_sel = kc.shape[1], ks_blk.shape[2]
    imp = jnp.einsum('bqkgc,cj->bqkj', p_c, cmp_to_sel_map(n_cmp, n_sel))
    blk = jnp.arange(n_sel)[None, :]
    cur = (q_pos // SEL_BLOCK)[:, None]
    forced = (blk == 0) | (blk == cur) | (blk == cur - 1)
    score = jnp.where((blk <= cur)[None, :, None, :],
                      imp + jnp.where(forced, FORCE_BONUS, 0.0)[None, :, None, :], -jnp.inf)
    n_top = min(SEL_TOPK, n_sel)
    top_s, top_i = lax.top_k(score, n_top)
    bi = jnp.arange(b)[:, None, None, None]
    ki = jnp.arange(NSA_KV_HEADS)[None, None, :, None]
    k_s = ks_blk[bi, ki, top_i].astype(F32).reshape(b, lq, NSA_KV_HEADS, n_top * SEL_BLOCK, HEAD_DIM)
    v_s = vs_blk[bi, ki, top_i].astype(F32).reshape(b, lq, NSA_KV_HEADS, n_top * SEL_BLOCK, HEAD_DIM)
    key_pos = top_i[..., None] * SEL_BLOCK + jnp.arange(SEL_BLOCK)
    m_s = jnp.isfinite(top_s)[..., None] & (key_pos <= q_pos[None, :, None, None, None])
    s_s = jnp.einsum('bqkgd,bqkjd->bqkgj', qg, k_s)
    p_s = masked_softmax(s_s, m_s.reshape(b, lq, NSA_KV_HEADS, 1, n_top * SEL_BLOCK))
    o_s = jnp.einsum('bqkgj,bqkjd->bqkgd', p_s, v_s)
    s_w = jnp.einsum('bqkgd,blkd->bqkgl', qg, kw.astype(F32))
    dist = q_pos[:, None] - kw_pos[None, :]
    m_w = (dist >= 0) & (dist < WINDOW) & (kw_pos[None, :] >= 0)
    p_w = masked_softmax(s_w, m_w[None, :, None, None, :])
    o_w = jnp.einsum('bqkgl,blkd->bqkgd', p_w, vw.astype(F32))
    g = gates.astype(F32)[..., None]
    o = g[:, :, 0] * o_c + g[:, :, 1] * o_s + g[:, :, 2] * o_w
    return o.reshape(b, lq, NSA_QW).astype(q.dtype)


def hybrid_prompt(x, norm_w, w_in_p, gdn_conv_w, gdn_a_log, gdn_dt_bias, gdn_norm_w, cmp_pe, cmp_w1, cmp_w2, w_out_b):
    b, l, d = x.shape
    x2 = x.reshape(b * l, d)
    proj = rms_mm(x2, norm_w, w_in_p).reshape(b, l, -1)
    hist = jnp.zeros((b, 8, GDN_CONV_DIM), F32)
    q_a, k_a, v_a, g_log, sig, q_n, cmp2d, sel2d, win2d, kvb = hyb_prep(
        proj, gdn_conv_w, gdn_a_log, gdn_dt_bias, jnp.arange(l), hist)
    s0 = jnp.zeros((b, GDN_HEADS, GDN_DK, GDN_DV), F32)
    gdn_out, s_new = gdn_gated(q_a, k_a, v_a, g_log, sig, proj, gdn_norm_w, s0,
                               g_off=0, b_off=GDN_BETA_LANE, z_blk=HYB_Z_BLK)
    nsa_out = nsa_prompt(q_n, cmp2d, kvb, sig, cmp_pe, cmp_w1.astype(BF16), cmp_w2.astype(BF16))
    mix = jnp.concatenate([gdn_out, nsa_out], axis=-1).reshape(b * l, -1).astype(BF16)
    out = mm_resid(mix, w_out_b, x2).reshape(b, l, d)
    rows5 = lambda r: r.reshape(b, l, 2, NSA_KV_HEADS, HEAD_DIM)
    new_conv = proj[:, -(GDN_CONV - 1):, HYB_QKV:HYB_QKV + GDN_CONV_DIM]
    return out, rows5(cmp2d), rows5(sel2d), rows5(win2d)[:, -min(WINDOW, l):], new_conv, s_new


def hybrid_sample(x, norm_w, pos0, w_in_p, gdn_conv_w, gdn_a_log, gdn_dt_bias, gdn_norm_w, cmp_pe, cmp_w1, cmp_w2,
                  w_out_b, gdn_conv_buf, gdn_s0, past_cmp, past_sel, win_buf):
    b, l, d = x.shape
    x2 = x.reshape(b * l, d)
    proj = rms_mm(x2, norm_w, w_in_p)[:, :HYB_IN].reshape(b, l, HYB_IN)
    (gq, gk, gv, gz, nq, ck, cv, sk, sv, wk, wv, ga, gb, ng) = split_cols(proj, [HYB_SPLITS[i] for i in HYB_ORDER])
    qkv_ext = jnp.concatenate([gdn_conv_buf, jnp.concatenate([gq, gk, gv], axis=-1)], axis=1)
    new_conv = qkv_ext[:, -(GDN_CONV - 1):]
    qkv = jax.nn.silu(causal_dwconv(qkv_ext, gdn_conv_w))
    q_a, k_a, v_a = split_cols(qkv, (GDN_QK, GDN_QK, GDN_VW))
    q_a = l2norm(q_a.reshape(b, l, GDN_HEADS, GDN_DK)) * GDN_DK ** -0.5
    k_a = l2norm(k_a.reshape(b, l, GDN_HEADS, GDN_DK))
    v_a = v_a.reshape(b, l, GDN_HEADS, GDN_DV)
    beta = jax.nn.sigmoid(gb.astype(F32))
    g_log = -jnp.exp(gdn_a_log.astype(F32)) * jax.nn.softplus(ga.astype(F32) + gdn_dt_bias.astype(F32))
    lpad = (-l) % GDN_CHUNK
    flat_pad = lambda a: jnp.pad(a.reshape(b, l, -1), ((0, 0), (0, lpad), (0, 0)))
    gdn_out, s_new = gdn_gated(flat_pad(q_a), flat_pad(k_a), flat_pad(v_a), flat_pad(g_log), flat_pad(beta),
                               flat_pad(gz), gdn_norm_w, gdn_s0)
    gdn_out = gdn_out[:, :l]
    pos = pos0 + jnp.arange(l)
    q_b = rope(nq.reshape(b, l, NSA_HEADS, HEAD_DIM), pos)
    kvr = lambda a: a.reshape(b, l, NSA_KV_HEADS, HEAD_DIM)
    cmp_rows = jnp.stack([rope(kvr(ck), pos), kvr(cv)], axis=2)
    sel_rows = jnp.stack([rope(kvr(sk), pos), kvr(sv)], axis=2)
    win_rows = jnp.stack([rope(kvr(wk), pos), kvr(wv)], axis=2)
    gates = jax.nn.sigmoid(ng.astype(F32)).reshape(b, l, 3, NSA_KV_HEADS, NSA_GROUP)
    cmp_all = jnp.concatenate([past_cmp, cmp_rows], axis=1)
    sel_all = jnp.concatenate([past_sel, sel_rows], axis=1)
    kc, vc, c_end = compress_blocks(cmp_all, cmp_pe, cmp_w1, cmp_w2)
    ks_blk, vs_blk = selection_blocks(sel_all)
    w_buf = win_buf.shape[1]
    win_all = jnp.concatenate([win_buf, win_rows], axis=1)
    kw_pos = pos0 - w_buf + jnp.arange(w_buf + l)
    nsa_out = nsa_core(q_b, pos, kc, vc, c_end, ks_blk, vs_blk,
                       win_all[:, :, 0], win_all[:, :, 1], kw_pos, gates)
    new_win = win_all[:, -w_buf:]
    mix = jnp.concatenate([gdn_out, nsa_out], axis=-1).reshape(b * l, -1).astype(BF16)
    out = mm_resid(mix, w_out_b, x2).reshape(b, l, d)
    return out, cmp_rows, sel_rows, new_win, new_conv, s_new


def ssm_mixer(x, norm_w, w_in_p, conv_w, conv_b, a_log, dt_bias, d_skip, gn_w, w_out_b, conv_buf, h0):
    b, l, d = x.shape
    x2 = x.reshape(b * l, d)
    proj = rms_mm(x2, norm_w, w_in_p).reshape(b, l, -1)
    xbc = proj[:, -(SSM_CONV - 1):, SSM_D_INNER:SSM_D_INNER + SSM_CONV_DIM]
    new_conv = jnp.concatenate([conv_buf, xbc], axis=1)[:, -(SSM_CONV - 1):]
    proj = jnp.pad(proj, ((0, 0), (0, (-l) % SSM_CHUNK), (0, 0)))
    y, h_new = ssd_fused(proj, l, conv_w, conv_b, a_log, dt_bias, d_skip, gn_w, conv_buf, h0)
    y = y[:, :l].reshape(b * l, SSM_D_INNER)
    return mm_resid(y, w_out_b, x2).reshape(b, l, d), new_conv, h_new


def gather_pages(pool, page_table):
    g = pool[page_table]
    return g.reshape((g.shape[0], g.shape[1] * g.shape[2]) + g.shape[3:])


def _pad_cols(w, mult):
    n = w.shape[-1]
    return jnp.pad(w, ((0, 0), (0, (-n) % mult)))


def kernel(x_prompt, x_sample, cache_cmp_kv, cache_sel_kv, cache_win_kv, state_gdn_conv, state_gdn, state_ssm_conv, state_ssm, page_table, hyb_norm_mix, hyb_w_in, hyb_gdn_conv_w, hyb_gdn_a_log, hyb_gdn_dt_bias, hyb_gdn_norm_w, hyb_cmp_pe, hyb_cmp_w1, hyb_cmp_w2, hyb_w_out, hyb_norm_ffn, ffn_w_gate, ffn_w_up, ffn_w_down, ssm_norm_mix, ssm_w_in, ssm_conv_w, ssm_conv_b, ssm_a_log, ssm_dt_bias, ssm_d_skip, ssm_norm_w, ssm_w_out, ssm_norm_ffn, moe_router, moe_w_gate, moe_w_up, moe_w_down, final_norm):
    hp, hs = x_prompt, x_sample
    bp, lp, d = hp.shape
    bs, ls, _ = hs.shape

    w_in_p = _pad_cols(reorder_hyb_cols(hyb_w_in[0]).astype(BF16), 512)
    w_out_b = hyb_w_out[0].astype(BF16)
    hw = (w_in_p, hyb_gdn_conv_w[0], hyb_gdn_a_log[0], hyb_gdn_dt_bias[0], hyb_gdn_norm_w[0],
          hyb_cmp_pe[0], hyb_cmp_w1[0], hyb_cmp_w2[0], w_out_b)
    hp, cmp_p, sel_p, win_p, gconv_p, gst_p = hybrid_prompt(hp, hyb_norm_mix[0], *hw)
    past_cmp = gather_pages(cache_cmp_kv[0], page_table)
    past_sel = gather_pages(cache_sel_kv[0], page_table)
    hs, cmp_s, sel_s, win_s, gconv_s, gst_s = hybrid_sample(
        hs, hyb_norm_mix[0], PAST_LEN, *hw, state_gdn_conv[0], state_gdn[0], past_cmp, past_sel, cache_win_kv[0])

    wg, wu, wd = ffn_w_gate[0].astype(BF16), ffn_w_up[0].astype(BF16), ffn_w_down[0].astype(BF16)

    def dense_ffn(x):
        b, l, _ = x.shape
        x2 = x.reshape(b * l, d)
        return mm_resid(rms_glu(x2, hyb_norm_ffn[0], wg, wu), wd, x2).reshape(b, l, d)

    hp = dense_ffn(hp)
    hs = dense_ffn(hs)

    sw_in_p = _pad_cols(ssm_w_in[0].astype(BF16), 512)
    sw_out_b = ssm_w_out[0].astype(BF16)
    sw = (sw_in_p, ssm_conv_w[0], ssm_conv_b[0], ssm_a_log[0], ssm_dt_bias[0], ssm_d_skip[0], ssm_norm_w[0], sw_out_b)
    zero_conv = jnp.zeros((bp, SSM_CONV - 1, SSM_CONV_DIM), hp.dtype)
    zero_h = jnp.zeros((bp, SSM_HEADS, SSM_HEAD_DIM, SSM_STATE), hp.dtype)
    hp, sconv_p, sst_p = ssm_mixer(hp, ssm_norm_mix[0], *sw, zero_conv, zero_h)
    hs, sconv_s, sst_s = ssm_mixer(hs, ssm_norm_mix[0], *sw, state_ssm_conv[0], state_ssm[0])

    tok = jnp.concatenate([hp.reshape(bp * lp, d), hs.reshape(bs * ls, d)], axis=0)
    n_tok = tok.shape[0]
    tok_pad = jnp.pad(tok, ((0, (-n_tok) % MOE_TM), (0, 0)))
    moe = moe_ffn(tok_pad, ssm_norm_ffn[0], moe_router[0], moe_w_gate[0], moe_w_up[0], moe_w_down[0])
    y = rms(tok_pad + moe, final_norm)
    y_prompt = y[:bp * lp].reshape(bp, lp, d)
    y_sample = y[bp * lp:n_tok].reshape(bs, ls, d)
    st = lambda a: a[None]
    return (y_prompt, y_sample,
            st(cmp_p), st(cmp_s), st(sel_p), st(sel_s), st(win_p), st(win_s),
            st(gconv_p), st(gconv_s), st(gst_p), st(gst_s),
            st(sconv_p), st(sconv_s), st(sst_p), st(sst_s))
```

```python
import functools
import math

import jax
import jax.numpy as jnp
from jax import lax
from jax.experimental import pallas as pl
from jax.experimental.pallas import tpu as pltpu

D_MODEL = 2048
PAST_LEN = 16384
HEAD_DIM = 128
ROPE_THETA = 10000.0
RMS_EPS = 1e-6

GDN_HEADS = D_MODEL // 256
GDN_DK = 128
GDN_DV = 128
GDN_CONV = 4
GDN_CHUNK = 64
GDN_QK = GDN_HEADS * GDN_DK
GDN_VW = GDN_HEADS * GDN_DV
GDN_CONV_DIM = 2 * GDN_QK + GDN_VW

NSA_HEADS = D_MODEL // 256
NSA_KV_HEADS = 2
NSA_GROUP = NSA_HEADS // NSA_KV_HEADS
NSA_QW = NSA_HEADS * HEAD_DIM
NSA_KVW = NSA_KV_HEADS * HEAD_DIM
CMP_BLOCK = 32
CMP_STRIDE = 16
SEL_BLOCK = 64
SEL_TOPK = 16
WINDOW = 512
Q_BLOCK = 128
FORCE_BONUS = 1e4

HYB_SPLITS = (GDN_QK, GDN_QK, GDN_VW, GDN_VW, GDN_HEADS, GDN_HEADS,
              NSA_QW, NSA_KVW, NSA_KVW, NSA_KVW, NSA_KVW, NSA_KVW, NSA_KVW, 3 * NSA_HEADS)
HYB_IN = sum(HYB_SPLITS)

SSM_D_INNER = 2 * D_MODEL
SSM_HEAD_DIM = 64
SSM_HEADS = SSM_D_INNER // SSM_HEAD_DIM
SSM_GROUPS = 8
SSM_HPG = SSM_HEADS // SSM_GROUPS
SSM_STATE = 128
SSM_CONV = 4
SSM_CHUNK = 128
SSM_BC = SSM_GROUPS * SSM_STATE
SSM_CONV_DIM = SSM_D_INNER + 2 * SSM_BC
SSM_IN = SSM_D_INNER + SSM_CONV_DIM + SSM_HEADS

N_EXPERTS = 8
TOP_K = 2

VMEM_LIMIT_BYTES = 56 * 1024 * 1024
LANES = 128

F32 = jnp.float32
BF16 = jnp.bfloat16


def _params(*sem):
    return pltpu.CompilerParams(dimension_semantics=sem, vmem_limit_bytes=VMEM_LIMIT_BYTES)


def _pick(n, prefs):
    for p in prefs:
        if n % p == 0:
            return p
    return n


def _rms_to_bf16(x, g):
    ms = jnp.mean(x * x, axis=-1, keepdims=True)
    return (x * lax.rsqrt(ms + RMS_EPS) * g).astype(BF16)


def _rms_mm_kernel(x_ref, g_ref, w_ref, o_ref, a_scr):
    @pl.when(pl.program_id(1) == 0)
    def _():
        a_scr[...] = _rms_to_bf16(x_ref[...], g_ref[...])

    o_ref[...] = jnp.dot(a_scr[...], w_ref[...], preferred_element_type=F32).astype(o_ref.dtype)


def rms_mm(x, gain, w, out_dtype=F32):
    m, k = x.shape
    n = w.shape[1]
    tm = _pick(m, (1024, 512, 256, 128))
    tn = _pick(n, (1024, 768, 512, 256, 128))
    return pl.pallas_call(
        _rms_mm_kernel,
        grid=(m // tm, n // tn),
        in_specs=[pl.BlockSpec((tm, k), lambda i, j: (i, 0)),
                  pl.BlockSpec((1, k), lambda i, j: (0, 0)),
                  pl.BlockSpec((k, tn), lambda i, j: (0, j))],
        out_specs=pl.BlockSpec((tm, tn), lambda i, j: (i, j)),
        out_shape=jax.ShapeDtypeStruct((m, n), out_dtype),
        scratch_shapes=[pltpu.VMEM((tm, k), BF16)],
        compiler_params=_params("parallel", "arbitrary"),
        name="rms_mm",
    )(x, gain.reshape(1, k).astype(F32), w)


def _rms_glu_kernel(x_ref, g_ref, wg_ref, wu_ref, o_ref, a_scr):
    @pl.when(pl.program_id(1) == 0)
    def _():
        a_scr[...] = _rms_to_bf16(x_ref[...], g_ref[...])

    a = a_scr[...]
    gt = jnp.dot(a, wg_ref[...], preferred_element_type=F32)
    up = jnp.dot(a, wu_ref[...], preferred_element_type=F32)
    o_ref[...] = (gt * jax.nn.sigmoid(gt) * up).astype(o_ref.dtype)


def rms_glu(x, gain, wg, wu):
    m, k = x.shape
    n = wg.shape[1]
    tm = _pick(m, (1024, 512, 256, 128))
    tn = _pick(n, (512, 256, 128))
    return pl.pallas_call(
        _rms_glu_kernel,
        grid=(m // tm, n // tn),
        in_specs=[pl.BlockSpec((tm, k), lambda i, j: (i, 0)),
                  pl.BlockSpec((1, k), lambda i, j: (0, 0)),
                  pl.BlockSpec((k, tn), lambda i, j: (0, j)),
                  pl.BlockSpec((k, tn), lambda i, j: (0, j))],
        out_specs=pl.BlockSpec((tm, tn), lambda i, j: (i, j)),
        out_shape=jax.ShapeDtypeStruct((m, n), BF16),
        scratch_shapes=[pltpu.VMEM((tm, k), BF16)],
        compiler_params=_params("parallel", "arbitrary"),
        name="rms_glu",
    )(x, gain.reshape(1, k).astype(F32), wg, wu)


def _mm_resid_kernel(a_ref, w_ref, r_ref, o_ref, acc_ref, *, nk):
    kk = pl.program_id(2)

    @pl.when(kk == 0)
    def _():
        acc_ref[...] = r_ref[...]

    acc_ref[...] += jnp.dot(a_ref[...], w_ref[...], preferred_element_type=F32)

    @pl.when(kk == nk - 1)
    def _():
        o_ref[...] = acc_ref[...]


def mm_resid(a, w, resid):
    m, k = a.shape
    n = w.shape[1]
    tm = _pick(m, (1024, 512, 256, 128))
    tn = _pick(n, (1024, 512, 256, 128))
    tk = _pick(k, (2048, 1792, 1408, 1024, 512))
    nk = k // tk
    return pl.pallas_call(
        functools.partial(_mm_resid_kernel, nk=nk),
        grid=(m // tm, n // tn, nk),
        in_specs=[pl.BlockSpec((tm, tk), lambda i, j, kk: (i, kk)),
                  pl.BlockSpec((tk, tn), lambda i, j, kk: (kk, j)),
                  pl.BlockSpec((tm, tn), lambda i, j, kk: (i, j))],
        out_specs=pl.BlockSpec((tm, tn), lambda i, j, kk: (i, j)),
        out_shape=jax.ShapeDtypeStruct((m, n), F32),
        scratch_shapes=[pltpu.VMEM((tm, tn), F32)],
        compiler_params=_params("parallel", "parallel", "arbitrary"),
        name="mm_resid",
    )(a, w, resid)


def _rms_kernel(x_ref, g_ref, o_ref):
    x = x_ref[...]
    ms = jnp.mean(x * x, axis=-1, keepdims=True)
    o_ref[...] = (x * lax.rsqrt(ms + RMS_EPS) * g_ref[...]).astype(o_ref.dtype)


def rms(x, gain, out_dtype=F32):
    m, k = x.shape
    tm = _pick(m, (512, 256, 128))
    return pl.pallas_call(
        _rms_kernel,
        grid=(m // tm,),
        in_specs=[pl.BlockSpec((tm, k), lambda i: (i, 0)),
                  pl.BlockSpec((1, k), lambda i: (0, 0))],
        out_specs=pl.BlockSpec((tm, k), lambda i: (i, 0)),
        out_shape=jax.ShapeDtypeStruct((m, k), out_dtype),
        compiler_params=_params("parallel"),
        name="rms",
    )(x, gain.reshape(1, k).astype(F32))


def _rms_router_kernel(x_ref, g_ref, wh_ref, wl_ref, h_ref, lg_ref):
    x = x_ref[...]
    ms = jnp.mean(x * x, axis=-1, keepdims=True)
    h = x * lax.rsqrt(ms + RMS_EPS) * g_ref[...]
    hh = h.astype(BF16)
    hl = (h - hh.astype(F32)).astype(BF16)
    h_ref[...] = hh
    wh = wh_ref[...]
    wl = wl_ref[...]
    lg = jnp.dot(hh, wh, preferred_element_type=F32)
    lg += jnp.dot(hl, wh, preferred_element_type=F32)
    lg += jnp.dot(hh, wl, preferred_element_type=F32)
    lg_ref[...] = lg


def rms_router(x, gain, router):
    m, k = x.shape
    e = router.shape[1]
    rp = jnp.pad(router.astype(F32), ((0, 0), (0, LANES - e)))
    rh = rp.astype(BF16)
    rl = (rp - rh.astype(F32)).astype(BF16)
    tm = _pick(m, (512, 256, 128, 32))
    h, lg = pl.pallas_call(
        _rms_router_kernel,
        grid=(m // tm,),
        in_specs=[pl.BlockSpec((tm, k), lambda i: (i, 0)),
                  pl.BlockSpec((1, k), lambda i: (0, 0)),
                  pl.BlockSpec((k, LANES), lambda i: (0, 0)),
                  pl.BlockSpec((k, LANES), lambda i: (0, 0))],
        out_specs=[pl.BlockSpec((tm, k), lambda i: (i, 0)),
                   pl.BlockSpec((tm, LANES), lambda i: (i, 0))],
        out_shape=[jax.ShapeDtypeStruct((m, k), BF16), jax.ShapeDtypeStruct((m, LANES), F32)],
        compiler_params=_params("parallel"),
        name="rms_router",
    )(x, gain.reshape(1, k).astype(F32), rh, rl)
    return h, lg[:, :e]


MOE_TM = 256


def _moe_glu_kernel(te_ref, tv_ref, tf_ref, a_ref, wg_ref, wu_ref, o_ref, wg_b, wu_b):
    i = pl.program_id(1)

    @pl.when(tf_ref[i] != 0)
    def _():
        wg_b[...] = wg_ref[...].astype(BF16)
        wu_b[...] = wu_ref[...].astype(BF16)

    @pl.when(tv_ref[i] != 0)
    def _():
        a = a_ref[...]
        gt = jnp.dot(a, wg_b[...], preferred_element_type=F32)
        up = jnp.dot(a, wu_b[...], preferred_element_type=F32)
        o_ref[...] = (gt * jax.nn.sigmoid(gt) * up).astype(o_ref.dtype)

    @pl.when(tv_ref[i] == 0)
    def _():
        o_ref[...] = jnp.zeros_like(o_ref)


def _moe_down_kernel(te_ref, tv_ref, tf_ref, a_ref, w_ref, o_ref, w_b):
    i = pl.program_id(1)

    @pl.when(tf_ref[i] != 0)
    def _():
        w_b[...] = w_ref[...].astype(BF16)

    @pl.when(tv_ref[i] != 0)
    def _():
        o_ref[...] = jnp.dot(a_ref[...], w_b[...], preferred_element_type=F32)

    @pl.when(tv_ref[i] == 0)
    def _():
        o_ref[...] = jnp.zeros_like(o_ref)


def moe_experts(a_sorted, tile_expert, tile_valid, tile_first, wg, wu, wd):
    r, d = a_sorted.shape
    f = wg.shape[2]
    tm = MOE_TM
    tf = _pick(f, (1024, 512))
    act = pl.pallas_call(
        _moe_glu_kernel,
        grid_spec=pltpu.PrefetchScalarGridSpec(
            num_scalar_prefetch=3,
            grid=(f // tf, r // tm),
            in_specs=[pl.BlockSpec((tm, d), lambda j, i, te, tv, t1: (i, 0)),
                      pl.BlockSpec((None, d, tf), lambda j, i, te, tv, t1: (te[i], 0, j)),
                      pl.BlockSpec((None, d, tf), lambda j, i, te, tv, t1: (te[i], 0, j))],
            out_specs=pl.BlockSpec((tm, tf), lambda j, i, te, tv, t1: (i, j)),
            scratch_shapes=[pltpu.VMEM((d, tf), BF16), pltpu.VMEM((d, tf), BF16)],
        ),
        out_shape=jax.ShapeDtypeStruct((r, f), BF16),
        compiler_params=_params("arbitrary", "arbitrary"),
        name="moe_glu",
    )(tile_expert, tile_valid, tile_first, a_sorted, wg, wu)
    tn = _pick(d, (512,))
    return pl.pallas_call(
        _moe_down_kernel,
        grid_spec=pltpu.PrefetchScalarGridSpec(
            num_scalar_prefetch=3,
            grid=(d // tn, r // tm),
            in_specs=[pl.BlockSpec((tm, f), lambda j, i, te, tv, t1: (i, 0)),
                      pl.BlockSpec((None, f, tn), lambda j, i, te, tv, t1: (te[i], 0, j))],
            out_specs=pl.BlockSpec((tm, tn), lambda j, i, te, tv, t1: (i, j)),
            scratch_shapes=[pltpu.VMEM((f, tn), BF16)],
        ),
        out_shape=jax.ShapeDtypeStruct((r, d), F32),
        compiler_params=_params("arbitrary", "arbitrary"),
        name="moe_down",
    )(tile_expert, tile_valid, tile_first, act, wd)


def moe_ffn(x, gain, router, wg, wu, wd):
    t, d = x.shape
    tm = MOE_TM
    h, logits = rms_router(x, gain, router)
    top_v, top_i = lax.top_k(logits, TOP_K)
    top_w = jax.nn.softmax(top_v, axis=-1)
    n_asg = t * TOP_K
    n_rows = (-(-n_asg // tm) + N_EXPERTS) * tm
    flat_e = top_i.reshape(-1).astype(jnp.int32)
    order = jnp.argsort(flat_e, stable=True).astype(jnp.int32)
    sorted_e = flat_e[order]
    counts = jnp.sum(flat_e[:, None] == jnp.arange(N_EXPERTS, dtype=jnp.int32)[None, :], axis=0).astype(jnp.int32)
    padded = ((counts + tm - 1) // tm) * tm
    pad_end = jnp.cumsum(padded)
    pad_start = pad_end - padded
    start = jnp.cumsum(counts) - counts
    dest = pad_start[sorted_e] + (jnp.arange(n_asg, dtype=jnp.int32) - start[sorted_e])
    row_src = jnp.zeros((n_rows,), jnp.int32).at[dest].set(order // TOP_K)
    pos = jnp.zeros((n_asg,), jnp.int32).at[order].set(dest)
    tile_start = jnp.arange(n_rows // tm, dtype=jnp.int32) * tm
    tile_valid = (tile_start < pad_end[-1]).astype(jnp.int32)
    tile_expert = jnp.minimum(jnp.searchsorted(pad_end, tile_start, side="right"), N_EXPERTS - 1).astype(jnp.int32)
    last_e = tile_expert[jnp.maximum(pad_end[-1] // tm - 1, 0)]
    tile_expert = jnp.where(tile_valid != 0, tile_expert, last_e)
    tile_first = jnp.concatenate([jnp.ones((1,), jnp.int32),
                                  (tile_expert[1:] != tile_expert[:-1]).astype(jnp.int32)])
    a_sorted = jnp.take(h, row_src, axis=0)
    y = moe_experts(a_sorted, tile_expert, tile_valid, tile_first, wg, wu, wd)
    yk = jnp.take(y, pos, axis=0).reshape(t, TOP_K, d)
    return jnp.sum(yk * top_w[..., None], axis=1)


NEG_BIG = -1e30
NSA_TQ = 128
SEL_CHUNK = 512
CMP_HIDDEN = 256


def _dot_nt(a, b):
    return lax.dot_general(a, b, (((1,), (1,)), ((), ())), preferred_element_type=F32)


def _cmp_mlp_kernel(x_ref, pe_ref, w1_ref, w2_ref, o_ref, hi_scr, *, n_seg):
    lo = jnp.zeros((n_seg, CMP_HIDDEN), F32)
    hi = jnp.zeros((n_seg, CMP_HIDDEN), F32)
    for r in range(CMP_STRIDE):
        xr = x_ref[pl.ds(r, n_seg, stride=CMP_STRIDE), :]
        a_lo = (xr + pe_ref[r:r + 1, :]).astype(BF16)
        a_hi = (xr + pe_ref[CMP_STRIDE + r:CMP_STRIDE + r + 1, :]).astype(BF16)
        lo += jnp.dot(a_lo, w1_ref[r * HEAD_DIM:(r + 1) * HEAD_DIM, :], preferred_element_type=F32)
        hi += jnp.dot(a_hi, w1_ref[(CMP_STRIDE + r) * HEAD_DIM:(CMP_STRIDE + r + 1) * HEAD_DIM, :],
                      preferred_element_type=F32)
    hi_scr[0:n_seg, :] = hi
    hi_scr[n_seg:n_seg + 8, :] = jnp.zeros((8, CMP_HIDDEN), F32)
    pre = lo + hi_scr[pl.ds(1, n_seg), :]
    hid = pre * jax.nn.sigmoid(pre)
    o_ref[...] = jnp.dot(hid.astype(BF16), w2_ref[...], preferred_element_type=F32)


def cmp_mlp_prompt(rows2d, pe, w1b, w2b):
    b, l, _ = rows2d.shape
    n_seg = l // CMP_STRIDE
    return pl.pallas_call(
        functools.partial(_cmp_mlp_kernel, n_seg=n_seg),
        grid=(b, 2, NSA_KV_HEADS),
        in_specs=[pl.BlockSpec((None, l, HEAD_DIM), lambda i, s, k: (i, 0, s * NSA_KV_HEADS + k)),
                  pl.BlockSpec((None, CMP_BLOCK, HEAD_DIM), lambda i, s, k: (s, 0, 0)),
                  pl.BlockSpec((None, CMP_BLOCK * HEAD_DIM, CMP_HIDDEN), lambda i, s, k: (s, 0, 0)),
                  pl.BlockSpec((None, CMP_HIDDEN, HEAD_DIM), lambda i, s, k: (s, 0, 0))],
        out_specs=pl.BlockSpec((None, None, None, n_seg, HEAD_DIM), lambda i, s, k: (i, s, k, 0, 0)),
        out_shape=jax.ShapeDtypeStruct((b, 2, NSA_KV_HEADS, n_seg, HEAD_DIM), F32),
        scratch_shapes=[pltpu.VMEM((n_seg + 8, CMP_HIDDEN), F32)],
        compiler_params=_params("parallel", "parallel", "parallel"),
        name="cmp_mlp_prompt",
    )(rows2d, pe, w1b, w2b)


def _nsa_prompt_kernel(q_ref, kc_ref, vc_ref, ks_ref, vs_ref, kw_ref, vw_ref, g_ref, o_ref,
                       m_scr, acc_scr, o_scr, sc_scr, *, tq, n_cmp, n_sel):
    t = pl.program_id(2)
    kvh = pl.program_id(1)
    ng = NSA_GROUP
    rows = ng * tq
    q = q_ref[...].reshape(rows, HEAD_DIM)
    qpos = t * tq + lax.broadcasted_iota(jnp.int32, (tq, 1), 0)

    def tile_rows(x):
        return jnp.concatenate([x] * ng, axis=0)

    lane = lax.broadcasted_iota(jnp.int32, (LANES, 3 * ng * LANES), 0)
    slot = lax.broadcasted_iota(jnp.int32, (LANES, 3 * ng * LANES), 1) // LANES
    pick = (lane == NSA_GATE_LANE + (slot // ng) * NSA_HEADS + kvh * ng + slot % ng).astype(BF16)
    gate_all = _dot_exact01_right(g_ref[...], pick)

    def gate_rows(branch):
        return jnp.concatenate([gate_all[:, (branch * ng + h) * LANES:(branch * ng + h + 1) * LANES]
                                for h in range(ng)], axis=0)

    def reset():
        m_scr[...] = jnp.full(m_scr.shape, NEG_BIG, F32)
        acc_scr[...] = jnp.zeros(acc_scr.shape, F32)

    def update(k, v, mask):
        s = _dot_nt(q, k)
        mask4 = tile_rows(mask)
        sm = jnp.where(mask4, s, NEG_BIG)
        m_prev = m_scr[:, :1]
        m_new = jnp.maximum(m_prev, jnp.max(sm, axis=1, keepdims=True))
        e = jnp.where(mask4, jnp.exp(sm - m_new), 0.0)
        alpha = jnp.exp(m_prev - m_new)
        m_scr[...] = jnp.broadcast_to(m_new, m_scr.shape)
        v1 = jnp.concatenate([v, jnp.ones_like(v)], axis=1)
        acc_scr[...] = alpha * acc_scr[...] + jnp.dot(e.astype(BF16), v1, preferred_element_type=F32)

    def finish():
        den = acc_scr[:, HEAD_DIM:]
        return acc_scr[:, :HEAD_DIM] / jnp.where(den > 0, den, 1.0)

    n_cp = kc_ref.shape[0]
    kc = kc_ref[...].astype(BF16)
    vc = vc_ref[...].astype(BF16)
    s = _dot_nt(q, kc)
    cidx = lax.broadcasted_iota(jnp.int32, (1, n_cp), 1)
    mask_c = tile_rows((cidx * CMP_STRIDE + (CMP_BLOCK - 1) <= qpos) & (cidx < n_cmp))
    sm = jnp.where(mask_c, s, NEG_BIG)
    e = jnp.where(mask_c, jnp.exp(sm - jnp.max(sm, axis=1, keepdims=True)), 0.0)
    den = jnp.sum(e, axis=1, keepdims=True)
    p = e / jnp.where(den > 0, den, 1.0)
    o_scr[...] = gate_rows(0) * jnp.dot(p.astype(BF16), vc, preferred_element_type=F32)
    p_sum = p[0:tq]
    for h in range(1, ng):
        p_sum = p_sum + p[h * tq:(h + 1) * tq]
    p_hi = p_sum.astype(BF16)
    p_lo = (p_sum - p_hi.astype(F32)).astype(BF16)
    cs = lax.broadcasted_iota(jnp.int32, (n_cp, LANES), 0) * CMP_STRIDE
    ss = lax.broadcasted_iota(jnp.int32, (n_cp, LANES), 1) * SEL_BLOCK
    overlap = ((cs < ss + SEL_BLOCK) & (cs + CMP_BLOCK > ss) & (cs < n_cmp * CMP_STRIDE)).astype(BF16)
    imp = jnp.dot(p_hi, overlap, preferred_element_type=F32) + jnp.dot(p_lo, overlap, preferred_element_type=F32)

    blk = lax.broadcasted_iota(jnp.int32, (tq, LANES), 1)
    cur = qpos // SEL_BLOCK
    forced = (blk == 0) | (blk == cur) | (blk == cur - 1)
    score = jnp.where((blk <= cur) & (blk < n_sel), imp + jnp.where(forced, FORCE_BONUS, 0.0), NEG_BIG)
    n_sp = sc_scr.shape[0]
    s_t = score.T[0:n_sp]
    sc_scr[...] = s_t
    jidx = lax.broadcasted_iota(jnp.int32, (n_sp, tq), 0)
    rank = jnp.zeros((n_sp, tq), jnp.int32)
    for i in range(n_sel):
        si = sc_scr[i:i + 1, :]
        rank += ((si > s_t) | ((si == s_t) & (i < jidx))).astype(jnp.int32)
    sel_t = ((rank < SEL_TOPK) & (s_t > 0.5 * NEG_BIG)).astype(F32)
    if n_sp < LANES:
        sel_t = jnp.concatenate([sel_t, jnp.zeros((LANES - n_sp, tq), F32)], axis=0)
    sel = sel_t.T.astype(BF16)

    reset()

    def sel_step(c, carry):
        start = pl.multiple_of(c * SEL_CHUNK, SEL_CHUNK)
        jrow = lax.broadcasted_iota(jnp.int32, (LANES, SEL_CHUNK), 0)
        kblk = (start + lax.broadcasted_iota(jnp.int32, (LANES, SEL_CHUNK), 1)) // SEL_BLOCK
        expand = (jrow == kblk).astype(BF16)
        chosen = jnp.dot(sel, expand, preferred_element_type=F32) > 0.5
        kpos = start + lax.broadcasted_iota(jnp.int32, (1, SEL_CHUNK), 1)
        update(ks_ref[pl.ds(start, SEL_CHUNK), :], vs_ref[pl.ds(start, SEL_CHUNK), :], chosen & (kpos <= qpos))
        return carry

    lax.fori_loop(0, (t * tq + tq + SEL_CHUNK - 1) // SEL_CHUNK, sel_step, 0)
    o_scr[...] += gate_rows(1) * finish()

    reset()
    span = WINDOW + tq
    start = pl.multiple_of(jnp.maximum(t - WINDOW // tq, 0) * tq, tq)
    dist = qpos - (start + lax.broadcasted_iota(jnp.int32, (1, span), 1))
    update(kw_ref[pl.ds(start, span), :], vw_ref[pl.ds(start, span), :], (dist >= 0) & (dist < WINDOW))
    out = o_scr[...] + gate_rows(2) * finish()
    for h in range(ng):
        o_ref[:, h * HEAD_DIM:(h + 1) * HEAD_DIM] = out[h * tq:(h + 1) * tq]


def nsa_prompt(q, cmp2d, kvb, sig, pe, w1b, w2b):
    b, _, l, _ = q.shape
    tq = NSA_TQ
    n_cmp = (l - CMP_BLOCK) // CMP_STRIDE + 1
    n_sel = -(-l // SEL_BLOCK)
    kvc = cmp_mlp_prompt(cmp2d, pe, w1b, w2b)
    n_cp = kvc.shape[3]
    n_sp = -(-n_sel // 8) * 8
    rows = NSA_GROUP * tq
    kv_spec = lambda s: pl.BlockSpec((None, None, None, l, HEAD_DIM), lambda i, k, t: (i, s, k, 0, 0))
    return pl.pallas_call(
        functools.partial(_nsa_prompt_kernel, tq=tq, n_cmp=n_cmp, n_sel=n_sel),
        grid=(b, NSA_KV_HEADS, l // tq),
        in_specs=[pl.BlockSpec((None, NSA_GROUP, tq, HEAD_DIM), lambda i, k, t: (i, k, t, 0)),
                  pl.BlockSpec((None, None, None, n_cp, HEAD_DIM), lambda i, k, t: (i, 0, k, 0, 0)),
                  pl.BlockSpec((None, None, None, n_cp, HEAD_DIM), lambda i, k, t: (i, 1, k, 0, 0)),
                  kv_spec(0), kv_spec(1), kv_spec(2), kv_spec(3),
                  pl.BlockSpec((None, tq, LANES), lambda i, k, t: (i, t, 0))],
        out_specs=pl.BlockSpec((None, tq, NSA_GROUP * HEAD_DIM), lambda i, k, t: (i, t, k)),
        out_shape=jax.ShapeDtypeStruct((b, l, NSA_QW), F32),
        scratch_shapes=[pltpu.VMEM((rows, LANES), F32), pltpu.VMEM((rows, 2 * HEAD_DIM), F32),
                        pltpu.VMEM((rows, HEAD_DIM), F32), pltpu.VMEM((n_sp, tq), F32)],
        compiler_params=_params("parallel", "parallel", "arbitrary"),
        name="nsa_prompt",
    )(q, kvc, kvc, kvb, kvb, kvb, kvb, sig)


PAGES_PER_STEP = 8
NSA_SAMPLE_TQ = 8
NEW_ROWS = 128


def _page_gather_kernel(pt_ref, *refs):
    o_ref = refs[-1]
    page = refs[0].shape[0]
    for i, r in enumerate(refs[:-1]):
        o_ref[i * page:(i + 1) * page, :] = r[...].astype(o_ref.dtype)


def page_gather(pool, page_table, out_dtype):
    b, n_pages = page_table.shape
    _, page, w = pool.shape
    g = PAGES_PER_STEP

    def page_spec(k):
        return pl.BlockSpec((None, page, w), lambda i, j, pt: (pt[i, j * g + k], 0, 0))

    return pl.pallas_call(
        _page_gather_kernel,
        grid_spec=pltpu.PrefetchScalarGridSpec(
            num_scalar_prefetch=1,
            grid=(b, n_pages // g),
            in_specs=[page_spec(k) for k in range(g)],
            out_specs=pl.BlockSpec((None, g * page, w), lambda i, j, pt: (i, j, 0)),
        ),
        out_shape=jax.ShapeDtypeStruct((b, n_pages * page, w), out_dtype),
        compiler_params=_params("parallel", "arbitrary"),
        name="page_gather",
    )(page_table, *([pool] * g))


def _nsa_sample_kernel(q_ref, kc_ref, vc_ref, ks_ref, vs_ref, kn_ref, vn_ref, kw_ref, vw_ref, g_ref, o_ref,
                       m_scr, acc_scr, o_scr, *, tq, pos0, n_new, n_cmp, n_sel, past, w_buf):
    kvh = pl.program_id(1)
    ng = NSA_GROUP
    rows = ng * tq
    nb = -(-n_sel // LANES) * LANES
    q = q_ref[...].reshape(rows, HEAD_DIM)
    qpos = pos0 + lax.broadcasted_iota(jnp.int32, (tq, 1), 0)

    def tile_rows(x):
        return jnp.concatenate([x] * ng, axis=0)

    lane = lax.broadcasted_iota(jnp.int32, (LANES, 3 * ng * LANES), 0)
    slot = lax.broadcasted_iota(jnp.int32, (LANES, 3 * ng * LANES), 1) // LANES
    pick = (lane == NSA_GATE_LANE + (slot // ng) * NSA_HEADS + kvh * ng + slot % ng).astype(BF16)
    gate_all = _dot_exact01_right(g_ref[...], pick)

    def gate_rows(branch):
        return jnp.concatenate([gate_all[:, (branch * ng + h) * LANES:(branch * ng + h + 1) * LANES]
                                for h in range(ng)], axis=0)

    def reset():
        m_scr[...] = jnp.full(m_scr.shape, NEG_BIG, F32)
        acc_scr[...] = jnp.zeros(acc_scr.shape, F32)

    def update(k, v, mask):
        s = _dot_nt(q, k)
        mask4 = tile_rows(mask)
        sm = jnp.where(mask4, s, NEG_BIG)
        m_prev = m_scr[:, :1]
        m_new = jnp.maximum(m_prev, jnp.max(sm, axis=1, keepdims=True))
        e = jnp.where(mask4, jnp.exp(sm - m_new), 0.0)
        alpha = jnp.exp(m_prev - m_new)
        m_scr[...] = jnp.broadcast_to(m_new, m_scr.shape)
        v1 = jnp.concatenate([v, jnp.ones_like(v)], axis=1)
        acc_scr[...] = alpha * acc_scr[...] + jnp.dot(e.astype(BF16), v1, preferred_element_type=F32)

    def finish():
        den = acc_scr[:, HEAD_DIM:]
        return acc_scr[:, :HEAD_DIM] / jnp.where(den > 0, den, 1.0)

    n_cp = kc_ref.shape[0]
    kc = kc_ref[...].astype(BF16)
    vc = vc_ref[...].astype(BF16)
    s = _dot_nt(q, kc)
    cidx = lax.broadcasted_iota(jnp.int32, (1, n_cp), 1)
    mask_c = tile_rows((cidx * CMP_STRIDE + (CMP_BLOCK - 1) <= qpos) & (cidx < n_cmp))
    sm = jnp.where(mask_c, s, NEG_BIG)
    e = jnp.where(mask_c, jnp.exp(sm - jnp.max(sm, axis=1, keepdims=True)), 0.0)
    den = jnp.sum(e, axis=1, keepdims=True)
    p = e / jnp.where(den > 0, den, 1.0)
    o_scr[...] = gate_rows(0) * jnp.dot(p.astype(BF16), vc, preferred_element_type=F32)
    p_sum = p[0:tq]
    for h in range(1, ng):
        p_sum = p_sum + p[h * tq:(h + 1) * tq]
    p_hi = p_sum.astype(BF16)
    p_lo = (p_sum - p_hi.astype(F32)).astype(BF16)
    cs = lax.broadcasted_iota(jnp.int32, (n_cp, nb), 0) * CMP_STRIDE
    ss = lax.broadcasted_iota(jnp.int32, (n_cp, nb), 1) * SEL_BLOCK
    overlap = ((cs < ss + SEL_BLOCK) & (cs + CMP_BLOCK > ss) & (cs < n_cmp * CMP_STRIDE)).astype(BF16)
    imp = jnp.dot(p_hi, overlap, preferred_element_type=F32) + jnp.dot(p_lo, overlap, preferred_element_type=F32)

    blk = lax.broadcasted_iota(jnp.int32, (tq, nb), 1)
    cur = qpos // SEL_BLOCK
    forced = (blk == 0) | (blk == cur) | (blk == cur - 1)
    score = jnp.where((blk <= cur) & (blk < n_sel), imp + jnp.where(forced, FORCE_BONUS, 0.0), NEG_BIG)
    rank = jnp.zeros((tq, nb), jnp.int32)
    for i in range(n_sel):
        si = score[:, i:i + 1]
        rank += ((si > score) | ((si == score) & (i < blk))).astype(jnp.int32)
    sel_f = ((rank < SEL_TOPK) & (score > 0.5 * NEG_BIG)).astype(F32)
    sel = sel_f.astype(BF16)

    reset()

    def sel_step(c, carry):
        start = pl.multiple_of(c * SEL_CHUNK, SEL_CHUNK)
        jrow = lax.broadcasted_iota(jnp.int32, (nb, SEL_CHUNK), 0)
        kblk = (start + lax.broadcasted_iota(jnp.int32, (nb, SEL_CHUNK), 1)) // SEL_BLOCK
        expand = (jrow == kblk).astype(BF16)
        chosen = jnp.dot(sel, expand, preferred_element_type=F32) > 0.5
        kpos = start + lax.broadcasted_iota(jnp.int32, (1, SEL_CHUNK), 1)
        update(ks_ref[pl.ds(start, SEL_CHUNK), :].astype(BF16), vs_ref[pl.ds(start, SEL_CHUNK), :].astype(BF16),
               chosen & (kpos <= qpos))
        return carry

    lax.fori_loop(0, past // SEL_CHUNK, sel_step, 0)
    new_blk = past // SEL_BLOCK
    idx = lax.broadcasted_iota(jnp.int32, (1, NEW_ROWS), 1)
    update(kn_ref[...], vn_ref[...],
           (sel_f[:, new_blk:new_blk + 1] > 0.5) & (past + idx <= qpos) & (idx < n_new))
    o_scr[...] += gate_rows(1) * finish()

    reset()
    widx = lax.broadcasted_iota(jnp.int32, (1, kw_ref.shape[0]), 1)
    kpos = pos0 - w_buf + widx
    dist = qpos - kpos
    update(kw_ref[...], vw_ref[...], (dist >= 0) & (dist < WINDOW) & (kpos >= 0) & (widx < w_buf + n_new))
    out = o_scr[...] + gate_rows(2) * finish()
    for h in range(ng):
        o_ref[:, h * HEAD_DIM:(h + 1) * HEAD_DIM] = out[h * tq:(h + 1) * tq]


def nsa_sample(q, kvc, past_sel, new_kv, win_kv, sig, pos0, n_new, w_buf):
    b, _, tq, _ = q.shape
    past = past_sel.shape[1]
    n_cp = kvc.shape[3]
    n_cmp = (past + n_new - CMP_BLOCK) // CMP_STRIDE + 1
    n_sel = -(-(past + n_new) // SEL_BLOCK)
    wp = win_kv.shape[3]
    rows = NSA_GROUP * tq
    kv5 = lambda s, n: pl.BlockSpec((None, None, None, n, HEAD_DIM), lambda i, k: (i, s, k, 0, 0))
    return pl.pallas_call(
        functools.partial(_nsa_sample_kernel, tq=tq, pos0=pos0, n_new=n_new, n_cmp=n_cmp, n_sel=n_sel,
                          past=past, w_buf=w_buf),
        grid=(b, NSA_KV_HEADS),
        in_specs=[pl.BlockSpec((None, NSA_GROUP, tq, HEAD_DIM), lambda i, k: (i, k, 0, 0)),
                  kv5(0, n_cp), kv5(1, n_cp),
                  pl.BlockSpec((None, past, HEAD_DIM), lambda i, k: (i, 0, k)),
                  pl.BlockSpec((None, past, HEAD_DIM), lambda i, k: (i, 0, NSA_KV_HEADS + k)),
                  kv5(0, NEW_ROWS), kv5(1, NEW_ROWS), kv5(0, wp), kv5(1, wp),
                  pl.BlockSpec((None, tq, LANES), lambda i, k: (i, 0, 0))],
        out_specs=pl.BlockSpec((None, tq, NSA_GROUP * HEAD_DIM), lambda i, k: (i, 0, k)),
        out_shape=jax.ShapeDtypeStruct((b, tq, NSA_QW), F32),
        scratch_shapes=[pltpu.VMEM((rows, LANES), F32), pltpu.VMEM((rows, 2 * HEAD_DIM), F32),
                        pltpu.VMEM((rows, HEAD_DIM), F32)],
        compiler_params=_params("parallel", "arbitrary"),
        name="nsa_sample",
    )(q, kvc, kvc, past_sel, past_sel, new_kv, new_kv, win_kv, win_kv, sig)


def sample_nsa(q_b, past_cmp, past_sel, sel_rows, win_all, gates, pe, w1, w2, pos0, w_buf):
    b, l = q_b.shape[:2]
    tq = NSA_SAMPLE_TQ
    kvc = cmp_mlp_prompt(past_cmp, pe, w1.astype(BF16), w2.astype(BF16))
    q = jnp.pad((q_b * HEAD_DIM ** -0.5).astype(BF16).transpose(0, 2, 1, 3), ((0, 0), (0, 0), (0, tq - l), (0, 0)))

    def heads_first(r, n):
        return jnp.pad(r.transpose(0, 2, 3, 1, 4).astype(BF16), ((0, 0), (0, 0), (0, 0), (0, n - r.shape[1]), (0, 0)))

    new_kv = heads_first(sel_rows, NEW_ROWS)
    win_kv = heads_first(win_all, -(-win_all.shape[1] // LANES) * LANES)
    sig = jnp.zeros((b, tq, LANES), F32).at[:, :l, NSA_GATE_LANE:NSA_GATE_LANE + 3 * NSA_HEADS].set(
        gates.reshape(b, l, 3 * NSA_HEADS))
    return nsa_sample(q, kvc, past_sel, new_kv, win_kv, sig, pos0, l, w_buf)[:, :l]


HYB_ORDER = (0, 1, 2, 3, 6, 7, 8, 9, 10, 11, 12, 4, 5, 13)
HYB_QKV = 0
HYB_Z_BLK = 3
HYB_NQ = 2 * GDN_QK + 2 * GDN_VW
HYB_KV = HYB_NQ + NSA_QW
HYB_SMALL = HYB_KV + 6 * NSA_KVW
GDN_BETA_LANE = GDN_HEADS
NSA_GATE_LANE = 2 * GDN_HEADS
PREP_ROWS = 256


def reorder_hyb_cols(w):
    offs = [0]
    for n in HYB_SPLITS:
        offs.append(offs[-1] + n)
    return jnp.concatenate([w[:, offs[i]:offs[i + 1]] for i in HYB_ORDER], axis=1)


def _softplus(x):
    return jnp.maximum(x, 0.0) + jnp.log(1.0 + jnp.exp(-jnp.abs(x)))


def _hyb_prep_kernel(p_ref, cw_ref, alog_ref, dtb_ref, cos_ref, sin_ref, hist_ref,
                     qa_ref, ka_ref, va_ref, g_ref, sig_ref, qn_ref, cmp_ref, sel_ref, win_ref, kvb_ref,
                     ext_scr, *, tr):
    kw = GDN_CONV
    off = 8 - (kw - 1)

    @pl.when(pl.program_id(1) == 0)
    def _():
        ext_scr[0:8, :] = hist_ref[...]

    ext_scr[8:8 + tr, :] = p_ref[:, HYB_QKV:HYB_QKV + GDN_CONV_DIM]
    outs = (qa_ref, ka_ref, va_ref)
    for seg in range(3):
        for h in range(GDN_HEADS):
            lo = seg * GDN_QK + h * GDN_DK
            acc = cw_ref[0:1, lo:lo + GDN_DK] * ext_scr[pl.ds(off, tr), lo:lo + GDN_DK]
            for j in range(1, kw):
                acc = acc + cw_ref[j:j + 1, lo:lo + GDN_DK] * ext_scr[pl.ds(off + j, tr), lo:lo + GDN_DK]
            y = _silu(acc)
            if seg < 2:
                y = y * lax.rsqrt(jnp.sum(y * y, axis=-1, keepdims=True) + 1e-6)
            if seg == 0:
                y = y * GDN_DK ** -0.5
            outs[seg][:, h * GDN_DK:(h + 1) * GDN_DK] = y
    ext_scr[0:8, :] = ext_scr[tr:tr + 8, :]

    small = p_ref[:, HYB_SMALL:HYB_SMALL + LANES]
    g_ref[...] = -jnp.exp(alog_ref[...]) * _softplus(small + dtb_ref[...])
    sig_ref[...] = jax.nn.sigmoid(small)

    cos = cos_ref[...]
    sin = sin_ref[...]

    def rope(x):
        return x * cos + pltpu.roll(x, HEAD_DIM // 2, axis=1) * sin

    for h in range(NSA_HEADS):
        x = p_ref[:, HYB_NQ + h * HEAD_DIM:HYB_NQ + (h + 1) * HEAD_DIM]
        qn_ref[h] = (rope(x) * HEAD_DIM ** -0.5).astype(BF16)
    for i, o_ref in enumerate((cmp_ref, sel_ref, win_ref)):
        for kh in range(NSA_KV_HEADS):
            k0 = HYB_KV + 2 * i * NSA_KVW + kh * HEAD_DIM
            kr = rope(p_ref[:, k0:k0 + HEAD_DIM])
            vr = p_ref[:, k0 + NSA_KVW:k0 + NSA_KVW + HEAD_DIM]
            o_ref[:, kh * HEAD_DIM:(kh + 1) * HEAD_DIM] = kr
            o_ref[:, NSA_KVW + kh * HEAD_DIM:NSA_KVW + (kh + 1) * HEAD_DIM] = vr
            if i > 0:
                kvb_ref[2 * (i - 1), kh] = kr.astype(BF16)
                kvb_ref[2 * (i - 1) + 1, kh] = vr.astype(BF16)


def hyb_prep(proj, conv_w, a_log, dt_bias, pos, hist):
    b, l, _ = proj.shape
    tr = PREP_ROWS
    half = HEAD_DIM // 2
    inv_freq = 1.0 / (ROPE_THETA ** (jnp.arange(half, dtype=F32) / half))
    ang = pos.astype(F32)[:, None] * inv_freq[None, :]
    cos = jnp.concatenate([jnp.cos(ang), jnp.cos(ang)], axis=1)
    sin = jnp.concatenate([-jnp.sin(ang), jnp.sin(ang)], axis=1)
    lane_pad = lambda v: jnp.pad(v.astype(F32).reshape(1, -1), ((0, 0), (0, LANES - v.shape[0])))
    row = lambda w: pl.BlockSpec((None, tr, w), lambda i, n: (i, n, 0))
    f32 = lambda w: jax.ShapeDtypeStruct((b, l, w), F32)
    return pl.pallas_call(
        functools.partial(_hyb_prep_kernel, tr=tr),
        grid=(b, l // tr),
        in_specs=[row(proj.shape[2]),
                  pl.BlockSpec((GDN_CONV, GDN_CONV_DIM), lambda i, n: (0, 0)),
                  pl.BlockSpec((1, LANES), lambda i, n: (0, 0)),
                  pl.BlockSpec((1, LANES), lambda i, n: (0, 0)),
                  pl.BlockSpec((tr, HEAD_DIM), lambda i, n: (n, 0)),
                  pl.BlockSpec((tr, HEAD_DIM), lambda i, n: (n, 0)),
                  pl.BlockSpec((None, 8, GDN_CONV_DIM), lambda i, n: (i, 0, 0))],
        out_specs=[row(GDN_QK), row(GDN_QK), row(GDN_VW), row(LANES), row(LANES),
                   pl.BlockSpec((None, NSA_HEADS, tr, HEAD_DIM), lambda i, n: (i, 0, n, 0)),
                   row(2 * NSA_KVW), row(2 * NSA_KVW), row(2 * NSA_KVW),
                   pl.BlockSpec((None, 4, NSA_KV_HEADS, tr, HEAD_DIM), lambda i, n: (i, 0, 0, n, 0))],
        out_shape=[f32(GDN_QK), f32(GDN_QK), f32(GDN_VW), f32(LANES), f32(LANES),
                   jax.ShapeDtypeStruct((b, NSA_HEADS, l, HEAD_DIM), BF16),
                   f32(2 * NSA_KVW), f32(2 * NSA_KVW), f32(2 * NSA_KVW),
                   jax.ShapeDtypeStruct((b, 4, NSA_KV_HEADS, l, HEAD_DIM), BF16)],
        scratch_shapes=[pltpu.VMEM((tr + 8, GDN_CONV_DIM), F32)],
        compiler_params=_params("parallel", "arbitrary"),
        name="hyb_prep",
    )(proj, conv_w.astype(F32), lane_pad(a_log), lane_pad(dt_bias), cos, sin, hist)


def _dot_exact01(a01, x):
    hi = x.astype(BF16)
    r = x - hi.astype(F32)
    mid = r.astype(BF16)
    lo = (r - mid.astype(F32)).astype(BF16)
    return (jnp.dot(a01, hi, preferred_element_type=F32) + jnp.dot(a01, mid, preferred_element_type=F32)
            + jnp.dot(a01, lo, preferred_element_type=F32))


def _gdn_kernel(q_ref, k_ref, v_ref, g_ref, b_ref, z_ref, nw_ref, s0_ref, o_ref, s_ref, *, c, g_off, b_off):
    @pl.when(pl.program_id(1) == 0)
    def _():
        s_ref[...] = s0_ref[...]

    ii = lax.broadcasted_iota(jnp.int32, (c, c), 0)
    jj = lax.broadcasted_iota(jnp.int32, (c, c), 1)
    incl = ii >= jj
    strict = ii > jj
    t01 = incl.astype(BF16)
    w01 = strict.astype(F32)
    g_all = g_ref[...]
    b_all = b_ref[...]
    nw = nw_ref[...]
    hs = range(GDN_HEADS)
    sl = [slice(h * GDN_DV, (h + 1) * GDN_DV) for h in hs]
    dot = functools.partial(jnp.dot, preferred_element_type=F32)
    bf = lambda xs: [x.astype(BF16) for x in xs]
    q = [q_ref[:, x] for x in sl]
    k = [k_ref[:, x] for x in sl]
    v = [v_ref[:, x] for x in sl]
    g_b = [jnp.broadcast_to(g_all[:, g_off + h:g_off + h + 1], (c, GDN_DK)) for h in hs]
    b_b = [jnp.broadcast_to(b_all[:, b_off + h:b_off + h + 1], (c, GDN_DK)) for h in hs]
    gam = [_dot_exact01(t01, x) for x in g_b]
    decay = [jnp.exp(_dot_exact01(t01, x[:, :c] * w01)) for x in g_b]
    kb = [x * y for x, y in zip(k, b_b)]
    kbf = bf(k)
    n_mat = [jnp.where(strict, _dot_nt(x, y) * d, 0.0) for x, y, d in zip(bf(kb), kbf, decay)]
    e_gam = [jnp.exp(x) for x in gam]
    y = [jnp.concatenate([vv * bb, kk * ee], axis=1) for vv, bb, kk, ee in zip(v, b_b, kb, e_gam)]
    y = [yy - dot(nn, yb) for yy, nn, yb in zip(y, bf(n_mat), bf(y))]
    p = n_mat
    span = 2
    while span < c:
        pb = bf(p)
        p = [dot(x, x) for x in pb]
        y = [yy + dot(pp, yb) for yy, pp, yb in zip(y, bf(p), bf(y))]
        span *= 2
    s = [s_ref[h] for h in hs]
    sb = bf(s)
    v_new = [yy[:, :GDN_DV] - dot(yy[:, GDN_DV:].astype(BF16), ss) for yy, ss in zip(y, sb)]
    vnb = bf(v_new)
    attn = [jnp.where(incl, _dot_nt(x, y) * d, 0.0) for x, y, d in zip(bf(q), kbf, decay)]
    o = [dot((qq * ee).astype(BF16), ss) + dot(aa, vv)
         for qq, ee, ss, aa, vv in zip(q, e_gam, sb, bf(attn), vnb)]
    g_end = [x[c - 1:c, :] for x in gam]
    k_dec = [(kk * jnp.exp(ge - gg)).astype(BF16) for kk, ge, gg in zip(k, g_end, gam)]
    for h in hs:
        s_ref[h] = s[h] * jnp.exp(g_end[h]) + lax.dot_general(
            k_dec[h], vnb[h], (((0,), (0,)), ((), ())), preferred_element_type=F32)
        on = o[h] * lax.rsqrt(jnp.mean(o[h] * o[h], axis=-1, keepdims=True) + RMS_EPS) * nw
        o_ref[:, sl[h]] = on * _silu(z_ref[:, sl[h]])


def gdn_gated(q, k, v, g, beta, z, norm_w, s0, g_off=0, b_off=0, z_blk=0):
    b, l, _ = q.shape
    c = GDN_CHUNK
    row = lambda w: pl.BlockSpec((None, c, w), lambda i, n: (i, n, 0))
    st = pl.BlockSpec((None, GDN_HEADS, GDN_DK, GDN_DV), lambda i, n: (i, 0, 0, 0))
    return pl.pallas_call(
        functools.partial(_gdn_kernel, c=c, g_off=g_off, b_off=b_off),
        grid=(b, l // c),
        in_specs=[row(GDN_QK), row(GDN_QK), row(GDN_VW), row(g.shape[2]), row(beta.shape[2]),
                  pl.BlockSpec((None, c, GDN_VW), lambda i, n: (i, n, z_blk)),
                  pl.BlockSpec((1, GDN_DV), lambda i, n: (0, 0)), st],
        out_specs=[row(GDN_VW), st],
        out_shape=[jax.ShapeDtypeStruct((b, l, GDN_VW), F32),
                   jax.ShapeDtypeStruct((b, GDN_HEADS, GDN_DK, GDN_DV), F32)],
        compiler_params=_params("parallel", "arbitrary"),
        name="gdn_chunked",
    )(q, k, v, g, beta, z, norm_w.reshape(1, GDN_DV).astype(F32), s0.astype(F32))


def _split3(x):
    hi = x.astype(BF16)
    r = x - hi.astype(F32)
    mid = r.astype(BF16)
    lo = (r - mid.astype(F32)).astype(BF16)
    return hi, mid, lo


def _dot_exact01_right(x, b01):
    hi, mid, lo = _split3(x)
    return (jnp.dot(hi, b01, preferred_element_type=F32) + jnp.dot(mid, b01, preferred_element_type=F32)
            + jnp.dot(lo, b01, preferred_element_type=F32))


def _silu(x):
    return x * jax.nn.sigmoid(x)


def _ssd_kernel(z_ref, x_ref, b_ref, c_ref, dt_ref, cw_ref, cb_ref, an_ref, dtb_ref, dsk_ref, nw_ref,
                r128_ref, r64_ref, hist_ref, h0_ref, y_ref, h_ref, ex_scr, eb_scr, ec_scr, *, c, n_valid):
    n = pl.program_id(1)
    kw = SSM_CONV
    off = 8 - (kw - 1)
    gw = SSM_HPG * SSM_HEAD_DIM

    @pl.when(n == 0)
    def _():
        h_ref[...] = h0_ref[...]
        ex_scr[0:8, :] = hist_ref[:, 0:SSM_D_INNER]
        eb_scr[0:8, :] = hist_ref[:, SSM_D_INNER:SSM_D_INNER + SSM_BC]
        ec_scr[0:8, :] = hist_ref[:, SSM_D_INNER + SSM_BC:SSM_CONV_DIM]

    ex_scr[8:8 + c, :] = x_ref[...]
    eb_scr[8:8 + c, :] = b_ref[...]
    ec_scr[8:8 + c, :] = c_ref[...]

    def conv(scr, col0, lo, width):
        acc = cb_ref[:, col0 + lo:col0 + lo + width]
        for j in range(kw):
            acc = acc + cw_ref[j:j + 1, col0 + lo:col0 + lo + width] * scr[pl.ds(off + j, c), lo:lo + width]
        return _silu(acc)

    ii = lax.broadcasted_iota(jnp.int32, (c, c), 0)
    jj = lax.broadcasted_iota(jnp.int32, (c, c), 1)
    incl = ii >= jj
    t01 = incl.astype(BF16)
    x_dt = dt_ref[...] + dtb_ref[...]
    dtv = jnp.maximum(x_dt, 0.0) + jnp.log(1.0 + jnp.exp(-jnp.abs(x_dt)))
    if n_valid is not None:
        row = n * c + lax.broadcasted_iota(jnp.int32, dtv.shape, 0)
        dtv = jnp.where(row < n_valid, dtv, 0.0)
    gam = _dot_exact01(t01, dtv * an_ref[...])
    gam_t = gam.T
    for g in range(SSM_GROUPS):
        r64 = r64_ref[:, g * gw:(g + 1) * gw]
        r128 = r128_ref[:, g * SSM_HPG * LANES:(g + 1) * SSM_HPG * LANES]
        xg = conv(ex_scr, 0, g * gw, gw)
        bg = conv(eb_scr, SSM_D_INNER, g * SSM_STATE, SSM_STATE).astype(BF16)
        cg = conv(ec_scr, SSM_D_INNER + SSM_BC, g * SSM_STATE, SSM_STATE).astype(BF16)
        gam64 = _dot_exact01_right(gam, r64)
        gam128 = _dot_exact01_right(gam, r128)
        xdt = xg * _dot_exact01_right(dtv, r64)
        xdt_b = xdt.astype(BF16)
        cb = _dot_nt(cg, bg)
        parts = []
        for hl in range(SSM_HPG):
            h = g * SSM_HPG + hl
            dec = jnp.where(incl, jnp.exp(gam128[:, hl * LANES:(hl + 1) * LANES] - gam_t[h:h + 1, :]), 0.0)
            parts.append(jnp.dot((cb * dec).astype(BF16), xdt_b[:, hl * SSM_HEAD_DIM:(hl + 1) * SSM_HEAD_DIM],
                                 preferred_element_type=F32))
        hs = h_ref[:, g * gw:(g + 1) * gw]
        y = (jnp.concatenate(parts, axis=1)
             + jnp.dot(cg, hs.astype(BF16), preferred_element_type=F32) * jnp.exp(gam64))
        g_end = gam64[c - 1:c, :]
        xdec = (xdt * jnp.exp(g_end - gam64)).astype(BF16)
        h_ref[:, g * gw:(g + 1) * gw] = hs * jnp.exp(g_end) + lax.dot_general(
            bg, xdec, (((0,), (0,)), ((), ())), preferred_element_type=F32)
        y = (y + dsk_ref[:, g * gw:(g + 1) * gw] * xg) * _silu(z_ref[:, g * gw:(g + 1) * gw])
        y = y * lax.rsqrt(jnp.mean(y * y, axis=-1, keepdims=True) + RMS_EPS) * nw_ref[:, g * gw:(g + 1) * gw]
        y_ref[:, g * gw:(g + 1) * gw] = y.astype(y_ref.dtype)

    ex_scr[0:8, :] = ex_scr[c:c + 8, :]
    eb_scr[0:8, :] = eb_scr[c:c + 8, :]
    ec_scr[0:8, :] = ec_scr[c:c + 8, :]


def ssd_fused(proj, n_valid, conv_w, conv_b, a_log, dt_bias, d_skip, norm_w, conv_buf, h0):
    b, lp, _ = proj.shape
    c = SSM_CHUNK
    f = lambda a: a.astype(F32)
    lane_pad = lambda v: jnp.pad(f(v).reshape(1, -1), ((0, 0), (0, LANES - v.shape[0])))
    heads = jnp.arange(LANES)[:, None]
    r128 = (heads == jnp.arange(SSM_HEADS * LANES)[None, :] // LANES).astype(BF16)
    r64 = (heads == jnp.arange(SSM_D_INNER)[None, :] // SSM_HEAD_DIM).astype(BF16)
    hist = jnp.pad(f(conv_buf), ((0, 0), (8 - (SSM_CONV - 1), 0), (0, 0)))
    h0_t = f(h0).reshape(b, SSM_D_INNER, SSM_STATE).transpose(0, 2, 1)
    col = lambda w, blk: pl.BlockSpec((None, c, w), lambda i, n: (i, n, blk))
    full = lambda shape: pl.BlockSpec(shape, lambda i, n: (0,) * len(shape))
    per_seq = lambda shape: pl.BlockSpec((None,) + shape, lambda i, n: (i,) + (0,) * len(shape))
    y, h_t = pl.pallas_call(
        functools.partial(_ssd_kernel, c=c, n_valid=None if n_valid == lp else n_valid),
        grid=(b, lp // c),
        in_specs=[col(SSM_D_INNER, 0), col(SSM_D_INNER, 1),
                  col(SSM_BC, 2 * SSM_D_INNER // SSM_BC), col(SSM_BC, 2 * SSM_D_INNER // SSM_BC + 1),
                  col(LANES, (2 * SSM_D_INNER + 2 * SSM_BC) // LANES),
                  full((SSM_CONV, SSM_CONV_DIM)), full((1, SSM_CONV_DIM)), full((1, LANES)), full((1, LANES)),
                  full((1, SSM_D_INNER)), full((1, SSM_D_INNER)),
                  full((LANES, SSM_HEADS * LANES)), full((LANES, SSM_D_INNER)),
                  per_seq((8, SSM_CONV_DIM)), per_seq((SSM_STATE, SSM_D_INNER))],
        out_specs=[col(SSM_D_INNER, 0), per_seq((SSM_STATE, SSM_D_INNER))],
        out_shape=[jax.ShapeDtypeStruct((b, lp, SSM_D_INNER), BF16),
                   jax.ShapeDtypeStruct((b, SSM_STATE, SSM_D_INNER), F32)],
        scratch_shapes=[pltpu.VMEM((c + 8, SSM_D_INNER), F32), pltpu.VMEM((c + 8, SSM_BC), F32),
                        pltpu.VMEM((c + 8, SSM_BC), F32)],
        compiler_params=_params("parallel", "arbitrary"),
        name="ssd_fused",
    )(proj, proj, proj, proj, proj, f(conv_w), f(conv_b).reshape(1, -1), lane_pad(-jnp.exp(f(a_log))),
      lane_pad(dt_bias), jnp.repeat(f(d_skip), SSM_HEAD_DIM).reshape(1, -1), f(norm_w).reshape(1, -1),
      r128, r64, hist, h0_t)
    return y, h_t.transpose(0, 2, 1).reshape(b, SSM_HEADS, SSM_HEAD_DIM, SSM_STATE)


def rmsnorm(x, w):
    xf = x.astype(F32)
    y = xf * lax.rsqrt(jnp.mean(xf * xf, axis=-1, keepdims=True) + RMS_EPS)
    return (y * w.astype(F32)).astype(x.dtype)


def l2norm(x):
    xf = x.astype(F32)
    return (xf * lax.rsqrt(jnp.sum(xf * xf, axis=-1, keepdims=True) + 1e-6)).astype(x.dtype)


def rope(x, pos):
    half = x.shape[-1] // 2
    inv_freq = 1.0 / (ROPE_THETA ** (jnp.arange(half, dtype=F32) / half))
    ang = pos.astype(F32)[:, None] * inv_freq[None, :]
    cos = jnp.cos(ang)[None, :, None, :]
    sin = jnp.sin(ang)[None, :, None, :]
    xf = x.astype(F32)
    x1, x2 = xf[..., :half], xf[..., half:]
    return jnp.concatenate([x1 * cos - x2 * sin, x2 * cos + x1 * sin], axis=-1).astype(x.dtype)


def split_cols(a, sizes):
    out, s = [], 0
    for n in sizes:
        out.append(a[..., s:s + n])
        s += n
    return out


def causal_dwconv(x_ext, w):
    c = x_ext.shape[-1]
    return lax.conv_general_dilated(x_ext, w[:, None, :], window_strides=(1,), padding='VALID',
                                    dimension_numbers=('NWC', 'WIO', 'NWC'), feature_group_count=c)


def masked_softmax(s, mask):
    s = jnp.where(mask, s, -jnp.inf)
    m = jnp.max(s, axis=-1, keepdims=True)
    m = jnp.where(jnp.isfinite(m), m, 0.0)
    e = jnp.where(mask, jnp.exp(s - m), 0.0)
    den = jnp.sum(e, axis=-1, keepdims=True)
    return e / jnp.where(den > 0, den, 1.0)


def compress_blocks(rows, pe, w1, w2):
    b, lk = rows.shape[:2]
    n_cmp = (lk - CMP_BLOCK) // CMP_STRIDE + 1
    per = CMP_BLOCK // CMP_STRIDE
    segs = rows[:, :(n_cmp + per - 1) * CMP_STRIDE].reshape(
        b, n_cmp + per - 1, CMP_STRIDE, 2, NSA_KV_HEADS, HEAD_DIM)
    blocks = jnp.concatenate([segs[:, i:i + n_cmp] for i in range(per)], axis=2)
    blocks = blocks + jnp.transpose(pe, (1, 0, 2))[:, :, None, :]
    flat = jnp.transpose(blocks, (0, 1, 3, 4, 2, 5)).reshape(b, n_cmp, 2, NSA_KV_HEADS, CMP_BLOCK * HEAD_DIM)
    hid = jax.nn.silu(jnp.einsum('bcskf,sfh->bcskh', flat, w1))
    out = jnp.einsum('bcskh,she->bcske', hid, w2)
    c_end = jnp.arange(n_cmp) * CMP_STRIDE + CMP_BLOCK - 1
    return out[:, :, 0], out[:, :, 1], c_end


def selection_blocks(rows):
    b, lk = rows.shape[:2]
    n_sel = -(-lk // SEL_BLOCK)
    rows = jnp.pad(rows, ((0, 0), (0, n_sel * SEL_BLOCK - lk), (0, 0), (0, 0), (0, 0)))
    blk = rows.reshape(b, n_sel, SEL_BLOCK, 2, NSA_KV_HEADS, HEAD_DIM).transpose(3, 0, 4, 1, 2, 5)
    return blk[0], blk[1]


def cmp_to_sel_map(n_cmp, n_sel):
    cs = jnp.arange(n_cmp)[:, None] * CMP_STRIDE
    ss = jnp.arange(n_sel)[None, :] * SEL_BLOCK
    return ((cs < ss + SEL_BLOCK) & (cs + CMP_BLOCK > ss)).astype(F32)


def nsa_core(q, q_pos, kc, vc, c_end, ks_blk, vs_blk, kw, vw, kw_pos, gates):
    b, lq = q.shape[:2]
    qg = q.astype(F32).reshape(b, lq, NSA_KV_HEADS, NSA_GROUP, HEAD_DIM) * HEAD_DIM ** -0.5
    s_c = jnp.einsum('bqkgd,bckd->bqkgc', qg, kc.astype(F32))
    m_c = c_end[None, :] <= q_pos[:, None]
    p_c = masked_softmax(s_c, m_c[None, :, None, None, :])
    o_c = jnp.einsum('bqkgc,bckd->bqkgd', p_c, vc.astype(F32))
    n_cmp, n_sel = kc.shape[1], ks_blk.shape[2]
    imp = jnp.einsum('bqkgc,cj->bqkj', p_c, cmp_to_sel_map(n_cmp, n_sel))
    blk = jnp.arange(n_sel)[None, :]
    cur = (q_pos // SEL_BLOCK)[:, None]
    forced = (blk == 0) | (blk == cur) | (blk == cur - 1)
    score = jnp.where((blk <= cur)[None, :, None, :],
                      imp + jnp.where(forced, FORCE_BONUS, 0.0)[None, :, None, :], -jnp.inf)
    n_top = min(SEL_TOPK, n_sel)
    top_s, top_i = lax.top_k(score, n_top)
    bi = jnp.arange(b)[:, None, None, None]
    ki = jnp.arange(NSA_KV_HEADS)[None, None, :, None]
    k_s = ks_blk[bi, ki, top_i].astype(F32).reshape(b, lq, NSA_KV_HEADS, n_top * SEL_BLOCK, HEAD_DIM)
    v_s = vs_blk[bi, ki, top_i].astype(F32).reshape(b, lq, NSA_KV_HEADS, n_top * SEL_BLOCK, HEAD_DIM)
    key_pos = top_i[..., None] * SEL_BLOCK + jnp.arange(SEL_BLOCK)
    m_s = jnp.isfinite(top_s)[..., None] & (key_pos <= q_pos[None, :, None, None, None])
    s_s = jnp.einsum('bqkgd,bqkjd->bqkgj', qg, k_s)
    p_s = masked_softmax(s_s, m_s.reshape(b, lq, NSA_KV_HEADS, 1, n_top * SEL_BLOCK))
    o_s = jnp.einsum('bqkgj,bqkjd->bqkgd', p_s, v_s)
    s_w = jnp.einsum('bqkgd,blkd->bqkgl', qg, kw.astype(F32))
    dist = q_pos[:, None] - kw_pos[None, :]
    m_w = (dist >= 0) & (dist < WINDOW) & (kw_pos[None, :] >= 0)
    p_w = masked_softmax(s_w, m_w[None, :, None, None, :])
    o_w = jnp.einsum('bqkgl,blkd->bqkgd', p_w, vw.astype(F32))
    g = gates.astype(F32)[..., None]
    o = g[:, :, 0] * o_c + g[:, :, 1] * o_s + g[:, :, 2] * o_w
    return o.reshape(b, lq, NSA_QW).astype(q.dtype)


def hybrid_prompt(x, norm_w, w_in_p, gdn_conv_w, gdn_a_log, gdn_dt_bias, gdn_norm_w, cmp_pe, cmp_w1, cmp_w2, w_out_b):
    b, l, d = x.shape
    x2 = x.reshape(b * l, d)
    proj = rms_mm(x2, norm_w, w_in_p).reshape(b, l, -1)
    hist = jnp.zeros((b, 8, GDN_CONV_DIM), F32)
    q_a, k_a, v_a, g_log, sig, q_n, cmp2d, sel2d, win2d, kvb = hyb_prep(
        proj, gdn_conv_w, gdn_a_log, gdn_dt_bias, jnp.arange(l), hist)
    s0 = jnp.zeros((b, GDN_HEADS, GDN_DK, GDN_DV), F32)
    gdn_out, s_new = gdn_gated(q_a, k_a, v_a, g_log, sig, proj, gdn_norm_w, s0,
                               g_off=0, b_off=GDN_BETA_LANE, z_blk=HYB_Z_BLK)
    nsa_out = nsa_prompt(q_n, cmp2d, kvb, sig, cmp_pe, cmp_w1.astype(BF16), cmp_w2.astype(BF16))
    mix = jnp.concatenate([gdn_out, nsa_out], axis=-1).reshape(b * l, -1).astype(BF16)
    out = mm_resid(mix, w_out_b, x2).reshape(b, l, d)
    rows5 = lambda r: r.reshape(b, l, 2, NSA_KV_HEADS, HEAD_DIM)
    new_conv = proj[:, -(GDN_CONV - 1):, HYB_QKV:HYB_QKV + GDN_CONV_DIM]
    return out, rows5(cmp2d), rows5(sel2d), rows5(win2d)[:, -min(WINDOW, l):], new_conv, s_new


def hybrid_sample(x, norm_w, pos0, w_in_p, gdn_conv_w, gdn_a_log, gdn_dt_bias, gdn_norm_w, cmp_pe, cmp_w1, cmp_w2,
                  w_out_b, gdn_conv_buf, gdn_s0, past_cmp, past_sel, win_buf):
    b, l, d = x.shape
    x2 = x.reshape(b * l, d)
    proj = rms_mm(x2, norm_w, w_in_p)[:, :HYB_IN].reshape(b, l, HYB_IN)
    (gq, gk, gv, gz, nq, ck, cv, sk, sv, wk, wv, ga, gb, ng) = split_cols(proj, [HYB_SPLITS[i] for i in HYB_ORDER])
    qkv_ext = jnp.concatenate([gdn_conv_buf, jnp.concatenate([gq, gk, gv], axis=-1)], axis=1)
    new_conv = qkv_ext[:, -(GDN_CONV - 1):]
    qkv = jax.nn.silu(causal_dwconv(qkv_ext, gdn_conv_w))
    q_a, k_a, v_a = split_cols(qkv, (GDN_QK, GDN_QK, GDN_VW))
    q_a = l2norm(q_a.reshape(b, l, GDN_HEADS, GDN_DK)) * GDN_DK ** -0.5
    k_a = l2norm(k_a.reshape(b, l, GDN_HEADS, GDN_DK))
    v_a = v_a.reshape(b, l, GDN_HEADS, GDN_DV)
    beta = jax.nn.sigmoid(gb.astype(F32))
    g_log = -jnp.exp(gdn_a_log.astype(F32)) * jax.nn.softplus(ga.astype(F32) + gdn_dt_bias.astype(F32))
    lpad = (-l) % GDN_CHUNK
    flat_pad = lambda a: jnp.pad(a.reshape(b, l, -1), ((0, 0), (0, lpad), (0, 0)))
    gdn_out, s_new = gdn_gated(flat_pad(q_a), flat_pad(k_a), flat_pad(v_a), flat_pad(g_log), flat_pad(beta),
                               flat_pad(gz), gdn_norm_w, gdn_s0)
    gdn_out = gdn_out[:, :l]
    pos = pos0 + jnp.arange(l)
    q_b = rope(nq.reshape(b, l, NSA_HEADS, HEAD_DIM), pos)
    kvr = lambda a: a.reshape(b, l, NSA_KV_HEADS, HEAD_DIM)
    cmp_rows = jnp.stack([rope(kvr(ck), pos), kvr(cv)], axis=2)
    sel_rows = jnp.stack([rope(kvr(sk), pos), kvr(sv)], axis=2)
    win_rows = jnp.stack([rope(kvr(wk), pos), kvr(wv)], axis=2)
    gates = jax.nn.sigmoid(ng.astype(F32)).reshape(b, l, 3, NSA_KV_HEADS, NSA_GROUP)
    w_buf = win_buf.shape[1]
    win_all = jnp.concatenate([win_buf, win_rows], axis=1)
    new_win = win_all[:, -w_buf:]
    nsa_out = sample_nsa(q_b, past_cmp, past_sel, sel_rows, win_all, gates, cmp_pe, cmp_w1, cmp_w2, pos0, w_buf)
    mix = jnp.concatenate([gdn_out, nsa_out], axis=-1).reshape(b * l, -1).astype(BF16)
    out = mm_resid(mix, w_out_b, x2).reshape(b, l, d)
    return out, cmp_rows, sel_rows, new_win, new_conv, s_new


def ssm_mixer(x, norm_w, w_in_p, conv_w, conv_b, a_log, dt_bias, d_skip, gn_w, w_out_b, conv_buf, h0):
    b, l, d = x.shape
    x2 = x.reshape(b * l, d)
    proj = rms_mm(x2, norm_w, w_in_p).reshape(b, l, -1)
    xbc = proj[:, -(SSM_CONV - 1):, SSM_D_INNER:SSM_D_INNER + SSM_CONV_DIM]
    new_conv = jnp.concatenate([conv_buf, xbc], axis=1)[:, -(SSM_CONV - 1):]
    proj = jnp.pad(proj, ((0, 0), (0, (-l) % SSM_CHUNK), (0, 0)))
    y, h_new = ssd_fused(proj, l, conv_w, conv_b, a_log, dt_bias, d_skip, gn_w, conv_buf, h0)
    y = y[:, :l].reshape(b * l, SSM_D_INNER)
    return mm_resid(y, w_out_b, x2).reshape(b, l, d), new_conv, h_new


def gather_pages(pool, page_table):
    g = pool[page_table]
    return g.reshape((g.shape[0], g.shape[1] * g.shape[2]) + g.shape[3:])


def _pad_cols(w, mult):
    n = w.shape[-1]
    return jnp.pad(w, ((0, 0), (0, (-n) % mult)))


def kernel(x_prompt, x_sample, cache_cmp_kv, cache_sel_kv, cache_win_kv, state_gdn_conv, state_gdn, state_ssm_conv, state_ssm, page_table, hyb_norm_mix, hyb_w_in, hyb_gdn_conv_w, hyb_gdn_a_log, hyb_gdn_dt_bias, hyb_gdn_norm_w, hyb_cmp_pe, hyb_cmp_w1, hyb_cmp_w2, hyb_w_out, hyb_norm_ffn, ffn_w_gate, ffn_w_up, ffn_w_down, ssm_norm_mix, ssm_w_in, ssm_conv_w, ssm_conv_b, ssm_a_log, ssm_dt_bias, ssm_d_skip, ssm_norm_w, ssm_w_out, ssm_norm_ffn, moe_router, moe_w_gate, moe_w_up, moe_w_down, final_norm):
    hp, hs = x_prompt, x_sample
    bp, lp, d = hp.shape
    bs, ls, _ = hs.shape

    w_in_p = _pad_cols(reorder_hyb_cols(hyb_w_in[0]).astype(BF16), 512)
    w_out_b = hyb_w_out[0].astype(BF16)
    hw = (w_in_p, hyb_gdn_conv_w[0], hyb_gdn_a_log[0], hyb_gdn_dt_bias[0], hyb_gdn_norm_w[0],
          hyb_cmp_pe[0], hyb_cmp_w1[0], hyb_cmp_w2[0], w_out_b)
    hp, cmp_p, sel_p, win_p, gconv_p, gst_p = hybrid_prompt(hp, hyb_norm_mix[0], *hw)
    pages2d = lambda c: c.reshape(c.shape[0], c.shape[1], 2 * NSA_KVW)
    past_cmp = page_gather(pages2d(cache_cmp_kv[0]), page_table, F32)
    past_sel = page_gather(pages2d(cache_sel_kv[0]), page_table, BF16)
    hs, cmp_s, sel_s, win_s, gconv_s, gst_s = hybrid_sample(
        hs, hyb_norm_mix[0], PAST_LEN, *hw, state_gdn_conv[0], state_gdn[0], past_cmp, past_sel, cache_win_kv[0])

    wg, wu, wd = ffn_w_gate[0].astype(BF16), ffn_w_up[0].astype(BF16), ffn_w_down[0].astype(BF16)

    def dense_ffn(x):
        b, l, _ = x.shape
        x2 = x.reshape(b * l, d)
        return mm_resid(rms_glu(x2, hyb_norm_ffn[0], wg, wu), wd, x2).reshape(b, l, d)

    hp = dense_ffn(hp)
    hs = dense_ffn(hs)

    sw_in_p = _pad_cols(ssm_w_in[0].astype(BF16), 512)
    sw_out_b = ssm_w_out[0].astype(BF16)
    sw = (sw_in_p, ssm_conv_w[0], ssm_conv_b[0], ssm_a_log[0], ssm_dt_bias[0], ssm_d_skip[0], ssm_norm_w[0], sw_out_b)
    zero_conv = jnp.zeros((bp, SSM_CONV - 1, SSM_CONV_DIM), hp.dtype)
    zero_h = jnp.zeros((bp, SSM_HEADS, SSM_HEAD_DIM, SSM_STATE), hp.dtype)
    hp, sconv_p, sst_p = ssm_mixer(hp, ssm_norm_mix[0], *sw, zero_conv, zero_h)
    hs, sconv_s, sst_s = ssm_mixer(hs, ssm_norm_mix[0], *sw, state_ssm_conv[0], state_ssm[0])

    tok = jnp.concatenate([hp.reshape(bp * lp, d), hs.reshape(bs * ls, d)], axis=0)
    n_tok = tok.shape[0]
    tok_pad = jnp.pad(tok, ((0, (-n_tok) % MOE_TM), (0, 0)))
    moe = moe_ffn(tok_pad, ssm_norm_ffn[0], moe_router[0], moe_w_gate[0], moe_w_up[0], moe_w_down[0])
    y = rms(tok_pad + moe, final_norm)
    y_prompt = y[:bp * lp].reshape(bp, lp, d)
    y_sample = y[bp * lp:n_tok].reshape(bs, ls, d)
    st = lambda a: a[None]
    return (y_prompt, y_sample,
            st(cmp_p), st(cmp_s), st(sel_p), st(sel_s), st(win_p), st(win_s),
            st(gconv_p), st(gconv_s), st(gst_p), st(gst_s),
            st(sconv_p), st(sconv_s), st(sst_p), st(sst_s))
```

```python
import functools
import math

import jax
import jax.numpy as jnp
from jax import lax
from jax.experimental import pallas as pl
from jax.experimental.pallas import tpu as pltpu

D_MODEL = 2048
PAST_LEN = 16384
HEAD_DIM = 128
ROPE_THETA = 10000.0
RMS_EPS = 1e-6

GDN_HEADS = D_MODEL // 256
GDN_DK = 128
GDN_DV = 128
GDN_CONV = 4
GDN_CHUNK = 64
GDN_QK = GDN_HEADS * GDN_DK
GDN_VW = GDN_HEADS * GDN_DV
GDN_CONV_DIM = 2 * GDN_QK + GDN_VW

NSA_HEADS = D_MODEL // 256
NSA_KV_HEADS = 2
NSA_GROUP = NSA_HEADS // NSA_KV_HEADS
NSA_QW = NSA_HEADS * HEAD_DIM
NSA_KVW = NSA_KV_HEADS * HEAD_DIM
CMP_BLOCK = 32
CMP_STRIDE = 16
SEL_BLOCK = 64
SEL_TOPK = 16
WINDOW = 512
Q_BLOCK = 128
FORCE_BONUS = 1e4

HYB_SPLITS = (GDN_QK, GDN_QK, GDN_VW, GDN_VW, GDN_HEADS, GDN_HEADS,
              NSA_QW, NSA_KVW, NSA_KVW, NSA_KVW, NSA_KVW, NSA_KVW, NSA_KVW, 3 * NSA_HEADS)
HYB_IN = sum(HYB_SPLITS)

SSM_D_INNER = 2 * D_MODEL
SSM_HEAD_DIM = 64
SSM_HEADS = SSM_D_INNER // SSM_HEAD_DIM
SSM_GROUPS = 8
SSM_HPG = SSM_HEADS // SSM_GROUPS
SSM_STATE = 128
SSM_CONV = 4
SSM_CHUNK = 128
SSM_BC = SSM_GROUPS * SSM_STATE
SSM_CONV_DIM = SSM_D_INNER + 2 * SSM_BC
SSM_IN = SSM_D_INNER + SSM_CONV_DIM + SSM_HEADS

N_EXPERTS = 8
TOP_K = 2

VMEM_LIMIT_BYTES = 56 * 1024 * 1024
LANES = 128

F32 = jnp.float32
BF16 = jnp.bfloat16


def _params(*sem):
    return pltpu.CompilerParams(dimension_semantics=sem, vmem_limit_bytes=VMEM_LIMIT_BYTES)


def _pick(n, prefs):
    for p in prefs:
        if n % p == 0:
            return p
    return n


def _rms_to_bf16(x, g):
    ms = jnp.mean(x * x, axis=-1, keepdims=True)
    return (x * lax.rsqrt(ms + RMS_EPS) * g).astype(BF16)


def _rms_mm_kernel(x_ref, g_ref, w_ref, o_ref, a_scr):
    @pl.when(pl.program_id(1) == 0)
    def _():
        a_scr[...] = _rms_to_bf16(x_ref[...], g_ref[...])

    o_ref[...] = jnp.dot(a_scr[...], w_ref[...], preferred_element_type=F32).astype(o_ref.dtype)


def rms_mm(x, gain, w, out_dtype=F32):
    m, k = x.shape
    n = w.shape[1]
    tm = _pick(m, (1024, 512, 256, 128))
    tn = _pick(n, (1024, 768, 512, 256, 128))
    return pl.pallas_call(
        _rms_mm_kernel,
        grid=(m // tm, n // tn),
        in_specs=[pl.BlockSpec((tm, k), lambda i, j: (i, 0)),
                  pl.BlockSpec((1, k), lambda i, j: (0, 0)),
                  pl.BlockSpec((k, tn), lambda i, j: (0, j))],
        out_specs=pl.BlockSpec((tm, tn), lambda i, j: (i, j)),
        out_shape=jax.ShapeDtypeStruct((m, n), out_dtype),
        scratch_shapes=[pltpu.VMEM((tm, k), BF16)],
        compiler_params=_params("parallel", "arbitrary"),
        name="rms_mm",
    )(x, gain.reshape(1, k).astype(F32), w)


def _rms_glu_kernel(x_ref, g_ref, wg_ref, wu_ref, o_ref, a_scr):
    @pl.when(pl.program_id(1) == 0)
    def _():
        a_scr[...] = _rms_to_bf16(x_ref[...], g_ref[...])

    a = a_scr[...]
    gt = jnp.dot(a, wg_ref[...], preferred_element_type=F32)
    up = jnp.dot(a, wu_ref[...], preferred_element_type=F32)
    o_ref[...] = (gt * jax.nn.sigmoid(gt) * up).astype(o_ref.dtype)


def rms_glu(x, gain, wg, wu):
    m, k = x.shape
    n = wg.shape[1]
    tm = _pick(m, (1024, 512, 256, 128))
    tn = _pick(n, (512, 256, 128))
    return pl.pallas_call(
        _rms_glu_kernel,
        grid=(m // tm, n // tn),
        in_specs=[pl.BlockSpec((tm, k), lambda i, j: (i, 0)),
                  pl.BlockSpec((1, k), lambda i, j: (0, 0)),
                  pl.BlockSpec((k, tn), lambda i, j: (0, j)),
                  pl.BlockSpec((k, tn), lambda i, j: (0, j))],
        out_specs=pl.BlockSpec((tm, tn), lambda i, j: (i, j)),
        out_shape=jax.ShapeDtypeStruct((m, n), BF16),
        scratch_shapes=[pltpu.VMEM((tm, k), BF16)],
        compiler_params=_params("parallel", "arbitrary"),
        name="rms_glu",
    )(x, gain.reshape(1, k).astype(F32), wg, wu)


def _mm_resid_kernel(a_ref, w_ref, r_ref, o_ref, acc_ref, *, nk):
    kk = pl.program_id(2)

    @pl.when(kk == 0)
    def _():
        acc_ref[...] = r_ref[...]

    acc_ref[...] += jnp.dot(a_ref[...], w_ref[...], preferred_element_type=F32)

    @pl.when(kk == nk - 1)
    def _():
        o_ref[...] = acc_ref[...]


def mm_resid(a, w, resid):
    m, k = a.shape
    n = w.shape[1]
    tm = _pick(m, (1024, 512, 256, 128))
    tn = _pick(n, (1024, 512, 256, 128))
    tk = _pick(k, (2048, 1792, 1408, 1024, 512))
    nk = k // tk
    return pl.pallas_call(
        functools.partial(_mm_resid_kernel, nk=nk),
        grid=(m // tm, n // tn, nk),
        in_specs=[pl.BlockSpec((tm, tk), lambda i, j, kk: (i, kk)),
                  pl.BlockSpec((tk, tn), lambda i, j, kk: (kk, j)),
                  pl.BlockSpec((tm, tn), lambda i, j, kk: (i, j))],
        out_specs=pl.BlockSpec((tm, tn), lambda i, j, kk: (i, j)),
        out_shape=jax.ShapeDtypeStruct((m, n), F32),
        scratch_shapes=[pltpu.VMEM((tm, tn), F32)],
        compiler_params=_params("parallel", "parallel", "arbitrary"),
        name="mm_resid",
    )(a, w, resid)


def _combine_rms_kernel(x_ref, y0_ref, y1_ref, w_ref, g_ref, o_ref):
    w = w_ref[...]
    x = x_ref[...] + (y0_ref[...] * w[:, 0:1] + y1_ref[...] * w[:, 1:2])
    ms = jnp.mean(x * x, axis=-1, keepdims=True)
    o_ref[...] = x * lax.rsqrt(ms + RMS_EPS) * g_ref[...]


def combine_rms(x, y0, y1, w, gain):
    m, k = x.shape
    tm = _pick(m, (256, 128))
    row = pl.BlockSpec((tm, k), lambda i: (i, 0))
    return pl.pallas_call(
        _combine_rms_kernel,
        grid=(m // tm,),
        in_specs=[row, row, row, pl.BlockSpec((tm, TOP_K), lambda i: (i, 0)),
                  pl.BlockSpec((1, k), lambda i: (0, 0))],
        out_specs=row,
        out_shape=jax.ShapeDtypeStruct((m, k), F32),
        compiler_params=_params("parallel"),
        name="combine_rms",
    )(x, y0, y1, w, gain.reshape(1, k).astype(F32))


def _rms_router_kernel(x_ref, g_ref, wh_ref, wl_ref, h_ref, lg_ref):
    x = x_ref[...]
    ms = jnp.mean(x * x, axis=-1, keepdims=True)
    h = x * lax.rsqrt(ms + RMS_EPS) * g_ref[...]
    hh = h.astype(BF16)
    hl = (h - hh.astype(F32)).astype(BF16)
    h_ref[...] = hh
    wh = wh_ref[...]
    wl = wl_ref[...]
    lg = jnp.dot(hh, wh, preferred_element_type=F32)
    lg += jnp.dot(hl, wh, preferred_element_type=F32)
    lg += jnp.dot(hh, wl, preferred_element_type=F32)
    lg_ref[...] = lg


def rms_router(x, gain, router):
    m, k = x.shape
    e = router.shape[1]
    rp = jnp.pad(router.astype(F32), ((0, 0), (0, LANES - e)))
    rh = rp.astype(BF16)
    rl = (rp - rh.astype(F32)).astype(BF16)
    tm = _pick(m, (512, 256, 128, 32))
    h, lg = pl.pallas_call(
        _rms_router_kernel,
        grid=(m // tm,),
        in_specs=[pl.BlockSpec((tm, k), lambda i: (i, 0)),
                  pl.BlockSpec((1, k), lambda i: (0, 0)),
                  pl.BlockSpec((k, LANES), lambda i: (0, 0)),
                  pl.BlockSpec((k, LANES), lambda i: (0, 0))],
        out_specs=[pl.BlockSpec((tm, k), lambda i: (i, 0)),
                   pl.BlockSpec((tm, LANES), lambda i: (i, 0))],
        out_shape=[jax.ShapeDtypeStruct((m, k), BF16), jax.ShapeDtypeStruct((m, LANES), F32)],
        compiler_params=_params("parallel"),
        name="rms_router",
    )(x, gain.reshape(1, k).astype(F32), rh, rl)
    return h, lg[:, :e]


MOE_TM = 256


def _moe_glu_kernel(te_ref, tv_ref, tf_ref, a_ref, wg_ref, wu_ref, o_ref, wg_b, wu_b):
    i = pl.program_id(1)

    @pl.when(tf_ref[i] != 0)
    def _():
        wg_b[...] = wg_ref[...].astype(BF16)
        wu_b[...] = wu_ref[...].astype(BF16)

    @pl.when(tv_ref[i] != 0)
    def _():
        a = a_ref[...]
        gt = jnp.dot(a, wg_b[...], preferred_element_type=F32)
        up = jnp.dot(a, wu_b[...], preferred_element_type=F32)
        o_ref[...] = (gt * jax.nn.sigmoid(gt) * up).astype(o_ref.dtype)

    @pl.when(tv_ref[i] == 0)
    def _():
        o_ref[...] = jnp.zeros_like(o_ref)


def _moe_down_kernel(te_ref, tv_ref, tf_ref, a_ref, w_ref, o_ref, w_b):
    i = pl.program_id(1)

    @pl.when(tf_ref[i] != 0)
    def _():
        w_b[...] = w_ref[...].astype(BF16)

    @pl.when(tv_ref[i] != 0)
    def _():
        o_ref[...] = jnp.dot(a_ref[...], w_b[...], preferred_element_type=F32)

    @pl.when(tv_ref[i] == 0)
    def _():
        o_ref[...] = jnp.zeros_like(o_ref)


def moe_experts(a_sorted, tile_expert, tile_valid, tile_first, wg, wu, wd):
    r, d = a_sorted.shape
    f = wg.shape[2]
    tm = MOE_TM
    tf = _pick(f, (1024, 512))
    act = pl.pallas_call(
        _moe_glu_kernel,
        grid_spec=pltpu.PrefetchScalarGridSpec(
            num_scalar_prefetch=3,
            grid=(f // tf, r // tm),
            in_specs=[pl.BlockSpec((tm, d), lambda j, i, te, tv, t1: (i, 0)),
                      pl.BlockSpec((None, d, tf), lambda j, i, te, tv, t1: (te[i], 0, j)),
                      pl.BlockSpec((None, d, tf), lambda j, i, te, tv, t1: (te[i], 0, j))],
            out_specs=pl.BlockSpec((tm, tf), lambda j, i, te, tv, t1: (i, j)),
            scratch_shapes=[pltpu.VMEM((d, tf), BF16), pltpu.VMEM((d, tf), BF16)],
        ),
        out_shape=jax.ShapeDtypeStruct((r, f), BF16),
        compiler_params=_params("arbitrary", "arbitrary"),
        name="moe_glu",
    )(tile_expert, tile_valid, tile_first, a_sorted, wg, wu)
    tn = _pick(d, (512,))
    return pl.pallas_call(
        _moe_down_kernel,
        grid_spec=pltpu.PrefetchScalarGridSpec(
            num_scalar_prefetch=3,
            grid=(d // tn, r // tm),
            in_specs=[pl.BlockSpec((tm, f), lambda j, i, te, tv, t1: (i, 0)),
                      pl.BlockSpec((None, f, tn), lambda j, i, te, tv, t1: (te[i], 0, j))],
            out_specs=pl.BlockSpec((tm, tn), lambda j, i, te, tv, t1: (i, j)),
            scratch_shapes=[pltpu.VMEM((f, tn), BF16)],
        ),
        out_shape=jax.ShapeDtypeStruct((r, d), F32),
        compiler_params=_params("arbitrary", "arbitrary"),
        name="moe_down",
    )(tile_expert, tile_valid, tile_first, act, wd)


def moe_ffn(x, gain, router, wg, wu, wd):
    t, d = x.shape
    tm = MOE_TM
    h, logits = rms_router(x, gain, router)
    top_v, top_i = lax.top_k(logits, TOP_K)
    top_w = jax.nn.softmax(top_v, axis=-1)
    n_asg = t * TOP_K
    n_rows = (-(-n_asg // tm) + N_EXPERTS) * tm
    flat_e = top_i.reshape(-1).astype(jnp.int32)
    order = jnp.argsort(flat_e, stable=True).astype(jnp.int32)
    sorted_e = flat_e[order]
    counts = jnp.sum(flat_e[:, None] == jnp.arange(N_EXPERTS, dtype=jnp.int32)[None, :], axis=0).astype(jnp.int32)
    padded = ((counts + tm - 1) // tm) * tm
    pad_end = jnp.cumsum(padded)
    pad_start = pad_end - padded
    start = jnp.cumsum(counts) - counts
    dest = pad_start[sorted_e] + (jnp.arange(n_asg, dtype=jnp.int32) - start[sorted_e])
    row_src = jnp.zeros((n_rows,), jnp.int32).at[dest].set(order // TOP_K)
    pos = jnp.zeros((n_asg,), jnp.int32).at[order].set(dest)
    tile_start = jnp.arange(n_rows // tm, dtype=jnp.int32) * tm
    tile_valid = (tile_start < pad_end[-1]).astype(jnp.int32)
    tile_expert = jnp.minimum(jnp.searchsorted(pad_end, tile_start, side="right"), N_EXPERTS - 1).astype(jnp.int32)
    last_e = tile_expert[jnp.maximum(pad_end[-1] // tm - 1, 0)]
    tile_expert = jnp.where(tile_valid != 0, tile_expert, last_e)
    tile_first = jnp.concatenate([jnp.ones((1,), jnp.int32),
                                  (tile_expert[1:] != tile_expert[:-1]).astype(jnp.int32)])
    a_sorted = jnp.take(h, row_src, axis=0)
    y = moe_experts(a_sorted, tile_expert, tile_valid, tile_first, wg, wu, wd)
    pos = pos.reshape(t, TOP_K)
    return jnp.take(y, pos[:, 0], axis=0), jnp.take(y, pos[:, 1], axis=0), top_w


NEG_BIG = -1e30
NSA_TQ = 128
SEL_CHUNK = 512
CMP_HIDDEN = 256


def _dot_nt(a, b):
    return lax.dot_general(a, b, (((1,), (1,)), ((), ())), preferred_element_type=F32)


def _cmp_mlp_kernel(x_ref, pe_ref, w1_ref, w2_ref, o_ref, hi_scr, *, n_seg):
    lo = jnp.zeros((n_seg, CMP_HIDDEN), F32)
    hi = jnp.zeros((n_seg, CMP_HIDDEN), F32)
    for r in range(CMP_STRIDE):
        xr = x_ref[pl.ds(r, n_seg, stride=CMP_STRIDE), :]
        a_lo = (xr + pe_ref[r:r + 1, :]).astype(BF16)
        a_hi = (xr + pe_ref[CMP_STRIDE + r:CMP_STRIDE + r + 1, :]).astype(BF16)
        lo += jnp.dot(a_lo, w1_ref[r * HEAD_DIM:(r + 1) * HEAD_DIM, :], preferred_element_type=F32)
        hi += jnp.dot(a_hi, w1_ref[(CMP_STRIDE + r) * HEAD_DIM:(CMP_STRIDE + r + 1) * HEAD_DIM, :],
                      preferred_element_type=F32)
    hi_scr[0:n_seg, :] = hi
    hi_scr[n_seg:n_seg + 8, :] = jnp.zeros((8, CMP_HIDDEN), F32)
    pre = lo + hi_scr[pl.ds(1, n_seg), :]
    hid = pre * jax.nn.sigmoid(pre)
    o_ref[...] = jnp.dot(hid.astype(BF16), w2_ref[...], preferred_element_type=F32)


def cmp_mlp_prompt(rows2d, pe, w1b, w2b):
    if rows2d.ndim == 4:
        b, _, l, _ = rows2d.shape
        row_spec = pl.BlockSpec((None, None, l, HEAD_DIM), lambda i, s, k: (i, s * NSA_KV_HEADS + k, 0, 0))
    else:
        b, l, _ = rows2d.shape
        row_spec = pl.BlockSpec((None, l, HEAD_DIM), lambda i, s, k: (i, 0, s * NSA_KV_HEADS + k))
    n_seg = l // CMP_STRIDE
    return pl.pallas_call(
        functools.partial(_cmp_mlp_kernel, n_seg=n_seg),
        grid=(b, 2, NSA_KV_HEADS),
        in_specs=[row_spec,
                  pl.BlockSpec((None, CMP_BLOCK, HEAD_DIM), lambda i, s, k: (s, 0, 0)),
                  pl.BlockSpec((None, CMP_BLOCK * HEAD_DIM, CMP_HIDDEN), lambda i, s, k: (s, 0, 0)),
                  pl.BlockSpec((None, CMP_HIDDEN, HEAD_DIM), lambda i, s, k: (s, 0, 0))],
        out_specs=pl.BlockSpec((None, None, None, n_seg, HEAD_DIM), lambda i, s, k: (i, s, k, 0, 0)),
        out_shape=jax.ShapeDtypeStruct((b, 2, NSA_KV_HEADS, n_seg, HEAD_DIM), F32),
        scratch_shapes=[pltpu.VMEM((n_seg + 8, CMP_HIDDEN), F32)],
        compiler_params=_params("parallel", "parallel", "parallel"),
        name="cmp_mlp_prompt",
    )(rows2d, pe, w1b, w2b)


def _nsa_prompt_kernel(q_ref, kc_ref, vc_ref, ks_ref, vs_ref, kw_ref, vw_ref, g_ref, o_ref,
                       m_scr, acc_scr, o_scr, sc_scr, *, tq, n_cmp, n_sel):
    t = pl.program_id(2)
    kvh = pl.program_id(1)
    ng = NSA_GROUP
    rows = ng * tq
    q = q_ref[...].reshape(rows, HEAD_DIM)
    qpos = t * tq + lax.broadcasted_iota(jnp.int32, (tq, 1), 0)

    def tile_rows(x):
        return jnp.concatenate([x] * ng, axis=0)

    lane = lax.broadcasted_iota(jnp.int32, (LANES, 3 * ng * LANES), 0)
    slot = lax.broadcasted_iota(jnp.int32, (LANES, 3 * ng * LANES), 1) // LANES
    pick = (lane == NSA_GATE_LANE + (slot // ng) * NSA_HEADS + kvh * ng + slot % ng).astype(BF16)
    gate_all = _dot_exact01_right(g_ref[...], pick)

    def gate_rows(branch):
        return jnp.concatenate([gate_all[:, (branch * ng + h) * LANES:(branch * ng + h + 1) * LANES]
                                for h in range(ng)], axis=0)

    def reset():
        m_scr[...] = jnp.full(m_scr.shape, NEG_BIG, F32)
        acc_scr[...] = jnp.zeros(acc_scr.shape, F32)

    def update(k, v, mask):
        s = _dot_nt(q, k)
        mask4 = tile_rows(mask)
        sm = jnp.where(mask4, s, NEG_BIG)
        m_prev = m_scr[:, :1]
        m_new = jnp.maximum(m_prev, jnp.max(sm, axis=1, keepdims=True))
        e = jnp.where(mask4, jnp.exp(sm - m_new), 0.0)
        alpha = jnp.exp(m_prev - m_new)
        m_scr[...] = jnp.broadcast_to(m_new, m_scr.shape)
        v1 = jnp.concatenate([v, jnp.ones_like(v)], axis=1)
        acc_scr[...] = alpha * acc_scr[...] + jnp.dot(e.astype(BF16), v1, preferred_element_type=F32)

    def finish():
        den = acc_scr[:, HEAD_DIM:]
        return acc_scr[:, :HEAD_DIM] / jnp.where(den > 0, den, 1.0)

    n_cp = kc_ref.shape[0]
    kc = kc_ref[...].astype(BF16)
    vc = vc_ref[...].astype(BF16)
    s = _dot_nt(q, kc)
    cidx = lax.broadcasted_iota(jnp.int32, (1, n_cp), 1)
    mask_c = tile_rows((cidx * CMP_STRIDE + (CMP_BLOCK - 1) <= qpos) & (cidx < n_cmp))
    sm = jnp.where(mask_c, s, NEG_BIG)
    e = jnp.where(mask_c, jnp.exp(sm - jnp.max(sm, axis=1, keepdims=True)), 0.0)
    den = jnp.sum(e, axis=1, keepdims=True)
    p = e / jnp.where(den > 0, den, 1.0)
    o_scr[...] = gate_rows(0) * jnp.dot(p.astype(BF16), vc, preferred_element_type=F32)
    p_sum = p[0:tq]
    for h in range(1, ng):
        p_sum = p_sum + p[h * tq:(h + 1) * tq]
    p_hi = p_sum.astype(BF16)
    p_lo = (p_sum - p_hi.astype(F32)).astype(BF16)
    cs = lax.broadcasted_iota(jnp.int32, (n_cp, LANES), 0) * CMP_STRIDE
    ss = lax.broadcasted_iota(jnp.int32, (n_cp, LANES), 1) * SEL_BLOCK
    overlap = ((cs < ss + SEL_BLOCK) & (cs + CMP_BLOCK > ss) & (cs < n_cmp * CMP_STRIDE)).astype(BF16)
    imp = jnp.dot(p_hi, overlap, preferred_element_type=F32) + jnp.dot(p_lo, overlap, preferred_element_type=F32)

    blk = lax.broadcasted_iota(jnp.int32, (tq, LANES), 1)
    cur = qpos // SEL_BLOCK
    forced = (blk == 0) | (blk == cur) | (blk == cur - 1)
    score = jnp.where((blk <= cur) & (blk < n_sel), imp + jnp.where(forced, FORCE_BONUS, 0.0), NEG_BIG)
    n_sp = sc_scr.shape[0]
    s_t = score.T[0:n_sp]
    sc_scr[...] = s_t
    jidx = lax.broadcasted_iota(jnp.int32, (n_sp, tq), 0)
    rank = jnp.zeros((n_sp, tq), jnp.int32)
    for i in range(n_sel):
        si = sc_scr[i:i + 1, :]
        rank += ((si > s_t) | ((si == s_t) & (i < jidx))).astype(jnp.int32)
    sel_t = ((rank < SEL_TOPK) & (s_t > 0.5 * NEG_BIG)).astype(F32)
    if n_sp < LANES:
        sel_t = jnp.concatenate([sel_t, jnp.zeros((LANES - n_sp, tq), F32)], axis=0)
    sel = sel_t.T.astype(BF16)

    reset()

    def sel_step(c, carry):
        start = pl.multiple_of(c * SEL_CHUNK, SEL_CHUNK)
        jrow = lax.broadcasted_iota(jnp.int32, (LANES, SEL_CHUNK), 0)
        kblk = (start + lax.broadcasted_iota(jnp.int32, (LANES, SEL_CHUNK), 1)) // SEL_BLOCK
        expand = (jrow == kblk).astype(BF16)
        chosen = jnp.dot(sel, expand, preferred_element_type=F32) > 0.5
        kpos = start + lax.broadcasted_iota(jnp.int32, (1, SEL_CHUNK), 1)
        update(ks_ref[pl.ds(start, SEL_CHUNK), :], vs_ref[pl.ds(start, SEL_CHUNK), :], chosen & (kpos <= qpos))
        return carry

    lax.fori_loop(0, (t * tq + tq + SEL_CHUNK - 1) // SEL_CHUNK, sel_step, 0)
    o_scr[...] += gate_rows(1) * finish()

    reset()
    span = WINDOW + tq
    start = pl.multiple_of(jnp.maximum(t - WINDOW // tq, 0) * tq, tq)
    dist = qpos - (start + lax.broadcasted_iota(jnp.int32, (1, span), 1))
    update(kw_ref[pl.ds(start, span), :], vw_ref[pl.ds(start, span), :], (dist >= 0) & (dist < WINDOW))
    out = o_scr[...] + gate_rows(2) * finish()
    for h in range(ng):
        o_ref[:, h * HEAD_DIM:(h + 1) * HEAD_DIM] = out[h * tq:(h + 1) * tq]


def nsa_prompt(q, cmp2d, kvb, sig, pe, w1b, w2b):
    b, _, l, _ = q.shape
    tq = NSA_TQ
    n_cmp = (l - CMP_BLOCK) // CMP_STRIDE + 1
    n_sel = -(-l // SEL_BLOCK)
    kvc = cmp_mlp_prompt(cmp2d, pe, w1b, w2b)
    n_cp = kvc.shape[3]
    n_sp = -(-n_sel // 8) * 8
    rows = NSA_GROUP * tq
    kv_spec = lambda s: pl.BlockSpec((None, None, None, l, HEAD_DIM), lambda i, k, t: (i, s, k, 0, 0))
    return pl.pallas_call(
        functools.partial(_nsa_prompt_kernel, tq=tq, n_cmp=n_cmp, n_sel=n_sel),
        grid=(b, NSA_KV_HEADS, l // tq),
        in_specs=[pl.BlockSpec((None, NSA_GROUP, tq, HEAD_DIM), lambda i, k, t: (i, k, t, 0)),
                  pl.BlockSpec((None, None, None, n_cp, HEAD_DIM), lambda i, k, t: (i, 0, k, 0, 0)),
                  pl.BlockSpec((None, None, None, n_cp, HEAD_DIM), lambda i, k, t: (i, 1, k, 0, 0)),
                  kv_spec(0), kv_spec(1), kv_spec(2), kv_spec(3),
                  pl.BlockSpec((None, tq, LANES), lambda i, k, t: (i, t, 0))],
        out_specs=pl.BlockSpec((None, tq, NSA_GROUP * HEAD_DIM), lambda i, k, t: (i, t, k)),
        out_shape=jax.ShapeDtypeStruct((b, l, NSA_QW), F32),
        scratch_shapes=[pltpu.VMEM((rows, LANES), F32), pltpu.VMEM((rows, 2 * HEAD_DIM), F32),
                        pltpu.VMEM((rows, HEAD_DIM), F32), pltpu.VMEM((n_sp, tq), F32)],
        compiler_params=_params("parallel", "parallel", "arbitrary"),
        name="nsa_prompt",
    )(q, kvc, kvc, kvb, kvb, kvb, kvb, sig)


PAGES_PER_STEP = 8
NSA_SAMPLE_TQ = 8
NEW_ROWS = 128


def _page_gather_kernel(pt_ref, *refs, page, n_col):
    o_ref = refs[-1]
    for i, r in enumerate(refs[:-1]):
        for c in range(n_col):
            o_ref[c, i * page:(i + 1) * page, :] = r[pl.ds(c, page, stride=n_col), :].astype(o_ref.dtype)


def page_gather(pool, page_table, out_dtype):
    b, n_pages = page_table.shape
    n_phys, page = pool.shape[:2]
    n_col = 2 * NSA_KV_HEADS
    g = PAGES_PER_STEP
    pool = pool.reshape(n_phys, page * n_col, HEAD_DIM)

    def page_spec(k):
        return pl.BlockSpec((None, page * n_col, HEAD_DIM), lambda i, j, pt: (pt[i, j * g + k], 0, 0))

    return pl.pallas_call(
        functools.partial(_page_gather_kernel, page=page, n_col=n_col),
        grid_spec=pltpu.PrefetchScalarGridSpec(
            num_scalar_prefetch=1,
            grid=(b, n_pages // g),
            in_specs=[page_spec(k) for k in range(g)],
            out_specs=pl.BlockSpec((None, n_col, g * page, HEAD_DIM), lambda i, j, pt: (i, 0, j, 0)),
        ),
        out_shape=jax.ShapeDtypeStruct((b, n_col, n_pages * page, HEAD_DIM), out_dtype),
        compiler_params=_params("parallel", "arbitrary"),
        name="page_gather",
    )(page_table, *([pool] * g))


def _nsa_sample_kernel(q_ref, kc_ref, vc_ref, ks_ref, vs_ref, kn_ref, vn_ref, kw_ref, vw_ref, g_ref, o_ref,
                       m_scr, acc_scr, o_scr, *, tq, pos0, n_new, n_cmp, n_sel, past, w_buf):
    kvh = pl.program_id(1)
    ng = NSA_GROUP
    rows = ng * tq
    nb = -(-n_sel // LANES) * LANES
    q = q_ref[...].reshape(rows, HEAD_DIM)
    qpos = pos0 + lax.broadcasted_iota(jnp.int32, (tq, 1), 0)

    def tile_rows(x):
        return jnp.concatenate([x] * ng, axis=0)

    lane = lax.broadcasted_iota(jnp.int32, (LANES, 3 * ng * LANES), 0)
    slot = lax.broadcasted_iota(jnp.int32, (LANES, 3 * ng * LANES), 1) // LANES
    pick = (lane == NSA_GATE_LANE + (slot // ng) * NSA_HEADS + kvh * ng + slot % ng).astype(BF16)
    gate_all = _dot_exact01_right(g_ref[...], pick)

    def gate_rows(branch):
        return jnp.concatenate([gate_all[:, (branch * ng + h) * LANES:(branch * ng + h + 1) * LANES]
                                for h in range(ng)], axis=0)

    def reset():
        m_scr[...] = jnp.full(m_scr.shape, NEG_BIG, F32)
        acc_scr[...] = jnp.zeros(acc_scr.shape, F32)

    def update(k, v, mask):
        s = _dot_nt(q, k)
        mask4 = tile_rows(mask)
        sm = jnp.where(mask4, s, NEG_BIG)
        m_prev = m_scr[:, :1]
        m_new = jnp.maximum(m_prev, jnp.max(sm, axis=1, keepdims=True))
        e = jnp.where(mask4, jnp.exp(sm - m_new), 0.0)
        alpha = jnp.exp(m_prev - m_new)
        m_scr[...] = jnp.broadcast_to(m_new, m_scr.shape)
        v1 = jnp.concatenate([v, jnp.ones_like(v)], axis=1)
        acc_scr[...] = alpha * acc_scr[...] + jnp.dot(e.astype(BF16), v1, preferred_element_type=F32)

    def finish():
        den = acc_scr[:, HEAD_DIM:]
        return acc_scr[:, :HEAD_DIM] / jnp.where(den > 0, den, 1.0)

    n_cp = kc_ref.shape[0]
    kc = kc_ref[...].astype(BF16)
    vc = vc_ref[...].astype(BF16)
    s = _dot_nt(q, kc)
    cidx = lax.broadcasted_iota(jnp.int32, (1, n_cp), 1)
    mask_c = tile_rows((cidx * CMP_STRIDE + (CMP_BLOCK - 1) <= qpos) & (cidx < n_cmp))
    sm = jnp.where(mask_c, s, NEG_BIG)
    e = jnp.where(mask_c, jnp.exp(sm - jnp.max(sm, axis=1, keepdims=True)), 0.0)
    den = jnp.sum(e, axis=1, keepdims=True)
    p = e / jnp.where(den > 0, den, 1.0)
    o_scr[...] = gate_rows(0) * jnp.dot(p.astype(BF16), vc, preferred_element_type=F32)
    p_sum = p[0:tq]
    for h in range(1, ng):
        p_sum = p_sum + p[h * tq:(h + 1) * tq]
    p_hi = p_sum.astype(BF16)
    p_lo = (p_sum - p_hi.astype(F32)).astype(BF16)
    cs = lax.broadcasted_iota(jnp.int32, (n_cp, nb), 0) * CMP_STRIDE
    ss = lax.broadcasted_iota(jnp.int32, (n_cp, nb), 1) * SEL_BLOCK
    overlap = ((cs < ss + SEL_BLOCK) & (cs + CMP_BLOCK > ss) & (cs < n_cmp * CMP_STRIDE)).astype(BF16)
    imp = jnp.dot(p_hi, overlap, preferred_element_type=F32) + jnp.dot(p_lo, overlap, preferred_element_type=F32)

    blk = lax.broadcasted_iota(jnp.int32, (tq, nb), 1)
    cur = qpos // SEL_BLOCK
    forced = (blk == 0) | (blk == cur) | (blk == cur - 1)
    score = jnp.where((blk <= cur) & (blk < n_sel), imp + jnp.where(forced, FORCE_BONUS, 0.0), NEG_BIG)
    rank = jnp.zeros((tq, nb), jnp.int32)
    for i in range(n_sel):
        si = score[:, i:i + 1]
        rank += ((si > score) | ((si == score) & (i < blk))).astype(jnp.int32)
    sel_f = ((rank < SEL_TOPK) & (score > 0.5 * NEG_BIG)).astype(F32)
    sel = sel_f.astype(BF16)

    reset()

    def sel_step(c, carry):
        start = pl.multiple_of(c * SEL_CHUNK, SEL_CHUNK)
        jrow = lax.broadcasted_iota(jnp.int32, (nb, SEL_CHUNK), 0)
        kblk = (start + lax.broadcasted_iota(jnp.int32, (nb, SEL_CHUNK), 1)) // SEL_BLOCK
        expand = (jrow == kblk).astype(BF16)
        chosen = jnp.dot(sel, expand, preferred_element_type=F32) > 0.5
        kpos = start + lax.broadcasted_iota(jnp.int32, (1, SEL_CHUNK), 1)
        update(ks_ref[pl.ds(start, SEL_CHUNK), :].astype(BF16), vs_ref[pl.ds(start, SEL_CHUNK), :].astype(BF16),
               chosen & (kpos <= qpos))
        return carry

    lax.fori_loop(0, past // SEL_CHUNK, sel_step, 0)
    new_blk = past // SEL_BLOCK
    idx = lax.broadcasted_iota(jnp.int32, (1, NEW_ROWS), 1)
    update(kn_ref[...], vn_ref[...],
           (sel_f[:, new_blk:new_blk + 1] > 0.5) & (past + idx <= qpos) & (idx < n_new))
    o_scr[...] += gate_rows(1) * finish()

    reset()
    widx = lax.broadcasted_iota(jnp.int32, (1, kw_ref.shape[0]), 1)
    kpos = pos0 - w_buf + widx
    dist = qpos - kpos
    update(kw_ref[...], vw_ref[...], (dist >= 0) & (dist < WINDOW) & (kpos >= 0) & (widx < w_buf + n_new))
    out = o_scr[...] + gate_rows(2) * finish()
    for h in range(ng):
        o_ref[:, h * HEAD_DIM:(h + 1) * HEAD_DIM] = out[h * tq:(h + 1) * tq]


def nsa_sample(q, kvc, past_sel, new_kv, win_kv, sig, pos0, n_new, w_buf):
    b, _, tq, _ = q.shape
    past = past_sel.shape[2]
    n_cp = kvc.shape[3]
    n_cmp = (past + n_new - CMP_BLOCK) // CMP_STRIDE + 1
    n_sel = -(-(past + n_new) // SEL_BLOCK)
    wp = win_kv.shape[3]
    rows = NSA_GROUP * tq
    kv5 = lambda s, n: pl.BlockSpec((None, None, None, n, HEAD_DIM), lambda i, k: (i, s, k, 0, 0))
    return pl.pallas_call(
        functools.partial(_nsa_sample_kernel, tq=tq, pos0=pos0, n_new=n_new, n_cmp=n_cmp, n_sel=n_sel,
                          past=past, w_buf=w_buf),
        grid=(b, NSA_KV_HEADS),
        in_specs=[pl.BlockSpec((None, NSA_GROUP, tq, HEAD_DIM), lambda i, k: (i, k, 0, 0)),
                  kv5(0, n_cp), kv5(1, n_cp),
                  pl.BlockSpec((None, None, past, HEAD_DIM), lambda i, k: (i, k, 0, 0)),
                  pl.BlockSpec((None, None, past, HEAD_DIM), lambda i, k: (i, NSA_KV_HEADS + k, 0, 0)),
                  kv5(0, NEW_ROWS), kv5(1, NEW_ROWS), kv5(0, wp), kv5(1, wp),
                  pl.BlockSpec((None, tq, LANES), lambda i, k: (i, 0, 0))],
        out_specs=pl.BlockSpec((None, tq, NSA_GROUP * HEAD_DIM), lambda i, k: (i, 0, k)),
        out_shape=jax.ShapeDtypeStruct((b, tq, NSA_QW), F32),
        scratch_shapes=[pltpu.VMEM((rows, LANES), F32), pltpu.VMEM((rows, 2 * HEAD_DIM), F32),
                        pltpu.VMEM((rows, HEAD_DIM), F32)],
        compiler_params=_params("parallel", "arbitrary"),
        name="nsa_sample",
    )(q, kvc, kvc, past_sel, past_sel, new_kv, new_kv, win_kv, win_kv, sig)


def sample_nsa(q_b, past_cmp, past_sel, sel_rows, win_all, gates, pe, w1, w2, pos0, w_buf):
    b, l = q_b.shape[:2]
    tq = NSA_SAMPLE_TQ
    kvc = cmp_mlp_prompt(past_cmp, pe, w1.astype(BF16), w2.astype(BF16))
    q = jnp.pad((q_b * HEAD_DIM ** -0.5).astype(BF16).transpose(0, 2, 1, 3), ((0, 0), (0, 0), (0, tq - l), (0, 0)))

    def heads_first(r, n):
        return jnp.pad(r.transpose(0, 2, 3, 1, 4).astype(BF16), ((0, 0), (0, 0), (0, 0), (0, n - r.shape[1]), (0, 0)))

    new_kv = heads_first(sel_rows, NEW_ROWS)
    win_kv = heads_first(win_all, -(-win_all.shape[1] // LANES) * LANES)
    sig = jnp.zeros((b, tq, LANES), F32).at[:, :l, NSA_GATE_LANE:NSA_GATE_LANE + 3 * NSA_HEADS].set(
        gates.reshape(b, l, 3 * NSA_HEADS))
    return nsa_sample(q, kvc, past_sel, new_kv, win_kv, sig, pos0, l, w_buf)[:, :l]


HYB_ORDER = (0, 1, 2, 3, 6, 7, 8, 9, 10, 11, 12, 4, 5, 13)
HYB_QKV = 0
HYB_Z_BLK = 3
HYB_NQ = 2 * GDN_QK + 2 * GDN_VW
HYB_KV = HYB_NQ + NSA_QW
HYB_SMALL = HYB_KV + 6 * NSA_KVW
GDN_BETA_LANE = GDN_HEADS
NSA_GATE_LANE = 2 * GDN_HEADS
PREP_ROWS = 256


def reorder_hyb_cols(w):
    offs = [0]
    for n in HYB_SPLITS:
        offs.append(offs[-1] + n)
    return jnp.concatenate([w[:, offs[i]:offs[i + 1]] for i in HYB_ORDER], axis=1)


def _softplus(x):
    return jnp.maximum(x, 0.0) + jnp.log(1.0 + jnp.exp(-jnp.abs(x)))


def _hyb_prep_kernel(p_ref, cw_ref, alog_ref, dtb_ref, cos_ref, sin_ref, hist_ref,
                     qa_ref, ka_ref, va_ref, g_ref, sig_ref, qn_ref, cmp_ref, sel_ref, win_ref, kvb_ref,
                     ext_scr, *, tr):
    kw = GDN_CONV
    off = 8 - (kw - 1)

    @pl.when(pl.program_id(1) == 0)
    def _():
        ext_scr[0:8, :] = hist_ref[...]

    ext_scr[8:8 + tr, :] = p_ref[:, HYB_QKV:HYB_QKV + GDN_CONV_DIM]
    outs = (qa_ref, ka_ref, va_ref)
    for seg in range(3):
        for h in range(GDN_HEADS):
            lo = seg * GDN_QK + h * GDN_DK
            acc = cw_ref[0:1, lo:lo + GDN_DK] * ext_scr[pl.ds(off, tr), lo:lo + GDN_DK]
            for j in range(1, kw):
                acc = acc + cw_ref[j:j + 1, lo:lo + GDN_DK] * ext_scr[pl.ds(off + j, tr), lo:lo + GDN_DK]
            y = _silu(acc)
            if seg < 2:
                y = y * lax.rsqrt(jnp.sum(y * y, axis=-1, keepdims=True) + 1e-6)
            if seg == 0:
                y = y * GDN_DK ** -0.5
            outs[seg][:, h * GDN_DK:(h + 1) * GDN_DK] = y
    ext_scr[0:8, :] = ext_scr[tr:tr + 8, :]

    small = p_ref[:, HYB_SMALL:HYB_SMALL + LANES]
    g_ref[...] = -jnp.exp(alog_ref[...]) * _softplus(small + dtb_ref[...])
    sig_ref[...] = jax.nn.sigmoid(small)

    cos = cos_ref[...]
    sin = sin_ref[...]

    def rope(x):
        return x * cos + pltpu.roll(x, HEAD_DIM // 2, axis=1) * sin

    for h in range(NSA_HEADS):
        x = p_ref[:, HYB_NQ + h * HEAD_DIM:HYB_NQ + (h + 1) * HEAD_DIM]
        qn_ref[h] = (rope(x) * HEAD_DIM ** -0.5).astype(BF16)
    for i, o_ref in enumerate((cmp_ref, sel_ref, win_ref)):
        for kh in range(NSA_KV_HEADS):
            k0 = HYB_KV + 2 * i * NSA_KVW + kh * HEAD_DIM
            kr = rope(p_ref[:, k0:k0 + HEAD_DIM])
            vr = p_ref[:, k0 + NSA_KVW:k0 + NSA_KVW + HEAD_DIM]
            o_ref[:, kh * HEAD_DIM:(kh + 1) * HEAD_DIM] = kr
            o_ref[:, NSA_KVW + kh * HEAD_DIM:NSA_KVW + (kh + 1) * HEAD_DIM] = vr
            if i > 0:
                kvb_ref[2 * (i - 1), kh] = kr.astype(BF16)
                kvb_ref[2 * (i - 1) + 1, kh] = vr.astype(BF16)


def hyb_prep(proj, conv_w, a_log, dt_bias, pos, hist):
    b, l, _ = proj.shape
    tr = PREP_ROWS
    half = HEAD_DIM // 2
    inv_freq = 1.0 / (ROPE_THETA ** (jnp.arange(half, dtype=F32) / half))
    ang = pos.astype(F32)[:, None] * inv_freq[None, :]
    cos = jnp.concatenate([jnp.cos(ang), jnp.cos(ang)], axis=1)
    sin = jnp.concatenate([-jnp.sin(ang), jnp.sin(ang)], axis=1)
    lane_pad = lambda v: jnp.pad(v.astype(F32).reshape(1, -1), ((0, 0), (0, LANES - v.shape[0])))
    row = lambda w: pl.BlockSpec((None, tr, w), lambda i, n: (i, n, 0))
    f32 = lambda w: jax.ShapeDtypeStruct((b, l, w), F32)
    return pl.pallas_call(
        functools.partial(_hyb_prep_kernel, tr=tr),
        grid=(b, l // tr),
        in_specs=[row(proj.shape[2]),
                  pl.BlockSpec((GDN_CONV, GDN_CONV_DIM), lambda i, n: (0, 0)),
                  pl.BlockSpec((1, LANES), lambda i, n: (0, 0)),
                  pl.BlockSpec((1, LANES), lambda i, n: (0, 0)),
                  pl.BlockSpec((tr, HEAD_DIM), lambda i, n: (n, 0)),
                  pl.BlockSpec((tr, HEAD_DIM), lambda i, n: (n, 0)),
                  pl.BlockSpec((None, 8, GDN_CONV_DIM), lambda i, n: (i, 0, 0))],
        out_specs=[row(GDN_QK), row(GDN_QK), row(GDN_VW), row(LANES), row(LANES),
                   pl.BlockSpec((None, NSA_HEADS, tr, HEAD_DIM), lambda i, n: (i, 0, n, 0)),
                   row(2 * NSA_KVW), row(2 * NSA_KVW), row(2 * NSA_KVW),
                   pl.BlockSpec((None, 4, NSA_KV_HEADS, tr, HEAD_DIM), lambda i, n: (i, 0, 0, n, 0))],
        out_shape=[f32(GDN_QK), f32(GDN_QK), f32(GDN_VW), f32(LANES), f32(LANES),
                   jax.ShapeDtypeStruct((b, NSA_HEADS, l, HEAD_DIM), BF16),
                   f32(2 * NSA_KVW), f32(2 * NSA_KVW), f32(2 * NSA_KVW),
                   jax.ShapeDtypeStruct((b, 4, NSA_KV_HEADS, l, HEAD_DIM), BF16)],
        scratch_shapes=[pltpu.VMEM((tr + 8, GDN_CONV_DIM), F32)],
        compiler_params=_params("parallel", "arbitrary"),
        name="hyb_prep",
    )(proj, conv_w.astype(F32), lane_pad(a_log), lane_pad(dt_bias), cos, sin, hist)


def _dot_exact01(a01, x):
    hi = x.astype(BF16)
    r = x - hi.astype(F32)
    mid = r.astype(BF16)
    lo = (r - mid.astype(F32)).astype(BF16)
    return (jnp.dot(a01, hi, preferred_element_type=F32) + jnp.dot(a01, mid, preferred_element_type=F32)
            + jnp.dot(a01, lo, preferred_element_type=F32))


def _gdn_kernel(q_ref, k_ref, v_ref, g_ref, b_ref, z_ref, nw_ref, s0_ref, o_ref, s_ref, *, c, g_off, b_off):
    @pl.when(pl.program_id(1) == 0)
    def _():
        s_ref[...] = s0_ref[...]

    ii = lax.broadcasted_iota(jnp.int32, (c, c), 0)
    jj = lax.broadcasted_iota(jnp.int32, (c, c), 1)
    incl = ii >= jj
    strict = ii > jj
    t01 = incl.astype(BF16)
    w01 = strict.astype(F32)
    g_all = g_ref[...]
    b_all = b_ref[...]
    nw = nw_ref[...]
    hs = range(GDN_HEADS)
    sl = [slice(h * GDN_DV, (h + 1) * GDN_DV) for h in hs]
    dot = functools.partial(jnp.dot, preferred_element_type=F32)
    bf = lambda xs: [x.astype(BF16) for x in xs]
    q = [q_ref[:, x] for x in sl]
    k = [k_ref[:, x] for x in sl]
    v = [v_ref[:, x] for x in sl]
    g_b = [jnp.broadcast_to(g_all[:, g_off + h:g_off + h + 1], (c, GDN_DK)) for h in hs]
    b_b = [jnp.broadcast_to(b_all[:, b_off + h:b_off + h + 1], (c, GDN_DK)) for h in hs]
    gam = [_dot_exact01(t01, x) for x in g_b]
    decay = [jnp.exp(_dot_exact01(t01, x[:, :c] * w01)) for x in g_b]
    kb = [x * y for x, y in zip(k, b_b)]
    kbf = bf(k)
    n_mat = [jnp.where(strict, _dot_nt(x, y) * d, 0.0) for x, y, d in zip(bf(kb), kbf, decay)]
    e_gam = [jnp.exp(x) for x in gam]
    y = [jnp.concatenate([vv * bb, kk * ee], axis=1) for vv, bb, kk, ee in zip(v, b_b, kb, e_gam)]
    y = [yy - dot(nn, yb) for yy, nn, yb in zip(y, bf(n_mat), bf(y))]
    p = n_mat
    span = 2
    while span < c:
        pb = bf(p)
        p = [dot(x, x) for x in pb]
        y = [yy + dot(pp, yb) for yy, pp, yb in zip(y, bf(p), bf(y))]
        span *= 2
    s = [s_ref[h] for h in hs]
    sb = bf(s)
    v_new = [yy[:, :GDN_DV] - dot(yy[:, GDN_DV:].astype(BF16), ss) for yy, ss in zip(y, sb)]
    vnb = bf(v_new)
    attn = [jnp.where(incl, _dot_nt(x, y) * d, 0.0) for x, y, d in zip(bf(q), kbf, decay)]
    o = [dot((qq * ee).astype(BF16), ss) + dot(aa, vv)
         for qq, ee, ss, aa, vv in zip(q, e_gam, sb, bf(attn), vnb)]
    g_end = [x[c - 1:c, :] for x in gam]
    k_dec = [(kk * jnp.exp(ge - gg)).astype(BF16) for kk, ge, gg in zip(k, g_end, gam)]
    for h in hs:
        s_ref[h] = s[h] * jnp.exp(g_end[h]) + lax.dot_general(
            k_dec[h], vnb[h], (((0,), (0,)), ((), ())), preferred_element_type=F32)
        on = o[h] * lax.rsqrt(jnp.mean(o[h] * o[h], axis=-1, keepdims=True) + RMS_EPS) * nw
        o_ref[:, sl[h]] = on * _silu(z_ref[:, sl[h]])


def gdn_gated(q, k, v, g, beta, z, norm_w, s0, g_off=0, b_off=0, z_blk=0):
    b, l, _ = q.shape
    c = GDN_CHUNK
    row = lambda w: pl.BlockSpec((None, c, w), lambda i, n: (i, n, 0))
    st = pl.BlockSpec((None, GDN_HEADS, GDN_DK, GDN_DV), lambda i, n: (i, 0, 0, 0))
    return pl.pallas_call(
        functools.partial(_gdn_kernel, c=c, g_off=g_off, b_off=b_off),
        grid=(b, l // c),
        in_specs=[row(GDN_QK), row(GDN_QK), row(GDN_VW), row(g.shape[2]), row(beta.shape[2]),
                  pl.BlockSpec((None, c, GDN_VW), lambda i, n: (i, n, z_blk)),
                  pl.BlockSpec((1, GDN_DV), lambda i, n: (0, 0)), st],
        out_specs=[row(GDN_VW), st],
        out_shape=[jax.ShapeDtypeStruct((b, l, GDN_VW), F32),
                   jax.ShapeDtypeStruct((b, GDN_HEADS, GDN_DK, GDN_DV), F32)],
        compiler_params=_params("parallel", "arbitrary"),
        name="gdn_chunked",
    )(q, k, v, g, beta, z, norm_w.reshape(1, GDN_DV).astype(F32), s0.astype(F32))


def _split3(x):
    hi = x.astype(BF16)
    r = x - hi.astype(F32)
    mid = r.astype(BF16)
    lo = (r - mid.astype(F32)).astype(BF16)
    return hi, mid, lo


def _dot_exact01_right(x, b01):
    hi, mid, lo = _split3(x)
    return (jnp.dot(hi, b01, preferred_element_type=F32) + jnp.dot(mid, b01, preferred_element_type=F32)
            + jnp.dot(lo, b01, preferred_element_type=F32))


def _silu(x):
    return x * jax.nn.sigmoid(x)


def _ssd_kernel(z_ref, x_ref, b_ref, c_ref, dt_ref, cw_ref, cb_ref, an_ref, dtb_ref, dsk_ref, nw_ref,
                r128_ref, r64_ref, hist_ref, h0_ref, y_ref, h_ref, ex_scr, eb_scr, ec_scr, *, c, n_valid):
    n = pl.program_id(1)
    kw = SSM_CONV
    off = 8 - (kw - 1)
    gw = SSM_HPG * SSM_HEAD_DIM

    @pl.when(n == 0)
    def _():
        h_ref[...] = h0_ref[...]
        ex_scr[0:8, :] = hist_ref[:, 0:SSM_D_INNER]
        eb_scr[0:8, :] = hist_ref[:, SSM_D_INNER:SSM_D_INNER + SSM_BC]
        ec_scr[0:8, :] = hist_ref[:, SSM_D_INNER + SSM_BC:SSM_CONV_DIM]

    ex_scr[8:8 + c, :] = x_ref[...]
    eb_scr[8:8 + c, :] = b_ref[...]
    ec_scr[8:8 + c, :] = c_ref[...]

    def conv(scr, col0, lo, width):
        acc = cb_ref[:, col0 + lo:col0 + lo + width]
        for j in range(kw):
            acc = acc + cw_ref[j:j + 1, col0 + lo:col0 + lo + width] * scr[pl.ds(off + j, c), lo:lo + width]
        return _silu(acc)

    ii = lax.broadcasted_iota(jnp.int32, (c, c), 0)
    jj = lax.broadcasted_iota(jnp.int32, (c, c), 1)
    incl = ii >= jj
    t01 = incl.astype(BF16)
    x_dt = dt_ref[...] + dtb_ref[...]
    dtv = jnp.maximum(x_dt, 0.0) + jnp.log(1.0 + jnp.exp(-jnp.abs(x_dt)))
    if n_valid is not None:
        row = n * c + lax.broadcasted_iota(jnp.int32, dtv.shape, 0)
        dtv = jnp.where(row < n_valid, dtv, 0.0)
    gam = _dot_exact01(t01, dtv * an_ref[...])
    gam_t = gam.T
    for g in range(SSM_GROUPS):
        r64 = r64_ref[:, g * gw:(g + 1) * gw]
        r128 = r128_ref[:, g * SSM_HPG * LANES:(g + 1) * SSM_HPG * LANES]
        xg = conv(ex_scr, 0, g * gw, gw)
        bg = conv(eb_scr, SSM_D_INNER, g * SSM_STATE, SSM_STATE).astype(BF16)
        cg = conv(ec_scr, SSM_D_INNER + SSM_BC, g * SSM_STATE, SSM_STATE).astype(BF16)
        gam64 = _dot_exact01_right(gam, r64)
        gam128 = _dot_exact01_right(gam, r128)
        xdt = xg * _dot_exact01_right(dtv, r64)
        xdt_b = xdt.astype(BF16)
        cb = _dot_nt(cg, bg)
        parts = []
        for hl in range(SSM_HPG):
            h = g * SSM_HPG + hl
            dec = jnp.where(incl, jnp.exp(gam128[:, hl * LANES:(hl + 1) * LANES] - gam_t[h:h + 1, :]), 0.0)
            parts.append(jnp.dot((cb * dec).astype(BF16), xdt_b[:, hl * SSM_HEAD_DIM:(hl + 1) * SSM_HEAD_DIM],
                                 preferred_element_type=F32))
        hs = h_ref[:, g * gw:(g + 1) * gw]
        y = (jnp.concatenate(parts, axis=1)
             + jnp.dot(cg, hs.astype(BF16), preferred_element_type=F32) * jnp.exp(gam64))
        g_end = gam64[c - 1:c, :]
        xdec = (xdt * jnp.exp(g_end - gam64)).astype(BF16)
        h_ref[:, g * gw:(g + 1) * gw] = hs * jnp.exp(g_end) + lax.dot_general(
            bg, xdec, (((0,), (0,)), ((), ())), preferred_element_type=F32)
        y = (y + dsk_ref[:, g * gw:(g + 1) * gw] * xg) * _silu(z_ref[:, g * gw:(g + 1) * gw])
        y = y * lax.rsqrt(jnp.mean(y * y, axis=-1, keepdims=True) + RMS_EPS) * nw_ref[:, g * gw:(g + 1) * gw]
        y_ref[:, g * gw:(g + 1) * gw] = y.astype(y_ref.dtype)

    ex_scr[0:8, :] = ex_scr[c:c + 8, :]
    eb_scr[0:8, :] = eb_scr[c:c + 8, :]
    ec_scr[0:8, :] = ec_scr[c:c + 8, :]


def ssd_fused(proj, n_valid, conv_w, conv_b, a_log, dt_bias, d_skip, norm_w, conv_buf, h0):
    b, lp, _ = proj.shape
    c = SSM_CHUNK
    f = lambda a: a.astype(F32)
    lane_pad = lambda v: jnp.pad(f(v).reshape(1, -1), ((0, 0), (0, LANES - v.shape[0])))
    heads = jnp.arange(LANES)[:, None]
    r128 = (heads == jnp.arange(SSM_HEADS * LANES)[None, :] // LANES).astype(BF16)
    r64 = (heads == jnp.arange(SSM_D_INNER)[None, :] // SSM_HEAD_DIM).astype(BF16)
    hist = jnp.pad(f(conv_buf), ((0, 0), (8 - (SSM_CONV - 1), 0), (0, 0)))
    h0_t = f(h0).reshape(b, SSM_D_INNER, SSM_STATE).transpose(0, 2, 1)
    col = lambda w, blk: pl.BlockSpec((None, c, w), lambda i, n: (i, n, blk))
    full = lambda shape: pl.BlockSpec(shape, lambda i, n: (0,) * len(shape))
    per_seq = lambda shape: pl.BlockSpec((None,) + shape, lambda i, n: (i,) + (0,) * len(shape))
    y, h_t = pl.pallas_call(
        functools.partial(_ssd_kernel, c=c, n_valid=None if n_valid == lp else n_valid),
        grid=(b, lp // c),
        in_specs=[col(SSM_D_INNER, 0), col(SSM_D_INNER, 1),
                  col(SSM_BC, 2 * SSM_D_INNER // SSM_BC), col(SSM_BC, 2 * SSM_D_INNER // SSM_BC + 1),
                  col(LANES, (2 * SSM_D_INNER + 2 * SSM_BC) // LANES),
                  full((SSM_CONV, SSM_CONV_DIM)), full((1, SSM_CONV_DIM)), full((1, LANES)), full((1, LANES)),
                  full((1, SSM_D_INNER)), full((1, SSM_D_INNER)),
                  full((LANES, SSM_HEADS * LANES)), full((LANES, SSM_D_INNER)),
                  per_seq((8, SSM_CONV_DIM)), per_seq((SSM_STATE, SSM_D_INNER))],
        out_specs=[col(SSM_D_INNER, 0), per_seq((SSM_STATE, SSM_D_INNER))],
        out_shape=[jax.ShapeDtypeStruct((b, lp, SSM_D_INNER), BF16),
                   jax.ShapeDtypeStruct((b, SSM_STATE, SSM_D_INNER), F32)],
        scratch_shapes=[pltpu.VMEM((c + 8, SSM_D_INNER), F32), pltpu.VMEM((c + 8, SSM_BC), F32),
                        pltpu.VMEM((c + 8, SSM_BC), F32)],
        compiler_params=_params("parallel", "arbitrary"),
        name="ssd_fused",
    )(proj, proj, proj, proj, proj, f(conv_w), f(conv_b).reshape(1, -1), lane_pad(-jnp.exp(f(a_log))),
      lane_pad(dt_bias), jnp.repeat(f(d_skip), SSM_HEAD_DIM).reshape(1, -1), f(norm_w).reshape(1, -1),
      r128, r64, hist, h0_t)
    return y, h_t.transpose(0, 2, 1).reshape(b, SSM_HEADS, SSM_HEAD_DIM, SSM_STATE)


def rmsnorm(x, w):
    xf = x.astype(F32)
    y = xf * lax.rsqrt(jnp.mean(xf * xf, axis=-1, keepdims=True) + RMS_EPS)
    return (y * w.astype(F32)).astype(x.dtype)


def l2norm(x):
    xf = x.astype(F32)
    return (xf * lax.rsqrt(jnp.sum(xf * xf, axis=-1, keepdims=True) + 1e-6)).astype(x.dtype)


def rope(x, pos):
    half = x.shape[-1] // 2
    inv_freq = 1.0 / (ROPE_THETA ** (jnp.arange(half, dtype=F32) / half))
    ang = pos.astype(F32)[:, None] * inv_freq[None, :]
    cos = jnp.cos(ang)[None, :, None, :]
    sin = jnp.sin(ang)[None, :, None, :]
    xf = x.astype(F32)
    x1, x2 = xf[..., :half], xf[..., half:]
    return jnp.concatenate([x1 * cos - x2 * sin, x2 * cos + x1 * sin], axis=-1).astype(x.dtype)


def split_cols(a, sizes):
    out, s = [], 0
    for n in sizes:
        out.append(a[..., s:s + n])
        s += n
    return out


def causal_dwconv(x_ext, w):
    c = x_ext.shape[-1]
    return lax.conv_general_dilated(x_ext, w[:, None, :], window_strides=(1,), padding='VALID',
                                    dimension_numbers=('NWC', 'WIO', 'NWC'), feature_group_count=c)


def masked_softmax(s, mask):
    s = jnp.where(mask, s, -jnp.inf)
    m = jnp.max(s, axis=-1, keepdims=True)
    m = jnp.where(jnp.isfinite(m), m, 0.0)
    e = jnp.where(mask, jnp.exp(s - m), 0.0)
    den = jnp.sum(e, axis=-1, keepdims=True)
    return e / jnp.where(den > 0, den, 1.0)


def compress_blocks(rows, pe, w1, w2):
    b, lk = rows.shape[:2]
    n_cmp = (lk - CMP_BLOCK) // CMP_STRIDE + 1
    per = CMP_BLOCK // CMP_STRIDE
    segs = rows[:, :(n_cmp + per - 1) * CMP_STRIDE].reshape(
        b, n_cmp + per - 1, CMP_STRIDE, 2, NSA_KV_HEADS, HEAD_DIM)
    blocks = jnp.concatenate([segs[:, i:i + n_cmp] for i in range(per)], axis=2)
    blocks = blocks + jnp.transpose(pe, (1, 0, 2))[:, :, None, :]
    flat = jnp.transpose(blocks, (0, 1, 3, 4, 2, 5)).reshape(b, n_cmp, 2, NSA_KV_HEADS, CMP_BLOCK * HEAD_DIM)
    hid = jax.nn.silu(jnp.einsum('bcskf,sfh->bcskh', flat, w1))
    out = jnp.einsum('bcskh,she->bcske', hid, w2)
    c_end = jnp.arange(n_cmp) * CMP_STRIDE + CMP_BLOCK - 1
    return out[:, :, 0], out[:, :, 1], c_end


def selection_blocks(rows):
    b, lk = rows.shape[:2]
    n_sel = -(-lk // SEL_BLOCK)
    rows = jnp.pad(rows, ((0, 0), (0, n_sel * SEL_BLOCK - lk), (0, 0), (0, 0), (0, 0)))
    blk = rows.reshape(b, n_sel, SEL_BLOCK, 2, NSA_KV_HEADS, HEAD_DIM).transpose(3, 0, 4, 1, 2, 5)
    return blk[0], blk[1]


def cmp_to_sel_map(n_cmp, n_sel):
    cs = jnp.arange(n_cmp)[:, None] * CMP_STRIDE
    ss = jnp.arange(n_sel)[None, :] * SEL_BLOCK
    return ((cs < ss + SEL_BLOCK) & (cs + CMP_BLOCK > ss)).astype(F32)


def nsa_core(q, q_pos, kc, vc, c_end, ks_blk, vs_blk, kw, vw, kw_pos, gates):
    b, lq = q.shape[:2]
    qg = q.astype(F32).reshape(b, lq, NSA_KV_HEADS, NSA_GROUP, HEAD_DIM) * HEAD_DIM ** -0.5
    s_c = jnp.einsum('bqkgd,bckd->bqkgc', qg, kc.astype(F32))
    m_c = c_end[None, :] <= q_pos[:, None]
    p_c = masked_softmax(s_c, m_c[None, :, None, None, :])
    o_c = jnp.einsum('bqkgc,bckd->bqkgd', p_c, vc.astype(F32))
    n_cmp, n_sel = kc.shape[1], ks_blk.shape[2]
    imp = jnp.einsum('bqkgc,cj->bqkj', p_c, cmp_to_sel_map(n_cmp, n_sel))
    blk = jnp.arange(n_sel)[None, :]
    cur = (q_pos // SEL_BLOCK)[:, None]
    forced = (blk == 0) | (blk == cur) | (blk == cur - 1)
    score = jnp.where((blk <= cur)[None, :, None, :],
                      imp + jnp.where(forced, FORCE_BONUS, 0.0)[None, :, None, :], -jnp.inf)
    n_top = min(SEL_TOPK, n_sel)
    top_s, top_i = lax.top_k(score, n_top)
    bi = jnp.arange(b)[:, None, None, None]
    ki = jnp.arange(NSA_KV_HEADS)[None, None, :, None]
    k_s = ks_blk[bi, ki, top_i].astype(F32).reshape(b, lq, NSA_KV_HEADS, n_top * SEL_BLOCK, HEAD_DIM)
    v_s = vs_blk[bi, ki, top_i].astype(F32).reshape(b, lq, NSA_KV_HEADS, n_top * SEL_BLOCK, HEAD_DIM)
    key_pos = top_i[..., None] * SEL_BLOCK + jnp.arange(SEL_BLOCK)
    m_s = jnp.isfinite(top_s)[..., None] & (key_pos <= q_pos[None, :, None, None, None])
    s_s = jnp.einsum('bqkgd,bqkjd->bqkgj', qg, k_s)
    p_s = masked_softmax(s_s, m_s.reshape(b, lq, NSA_KV_HEADS, 1, n_top * SEL_BLOCK))
    o_s = jnp.einsum('bqkgj,bqkjd->bqkgd', p_s, v_s)
    s_w = jnp.einsum('bqkgd,blkd->bqkgl', qg, kw.astype(F32))
    dist = q_pos[:, None] - kw_pos[None, :]
    m_w = (dist >= 0) & (dist < WINDOW) & (kw_pos[None, :] >= 0)
    p_w = masked_softmax(s_w, m_w[None, :, None, None, :])
    o_w = jnp.einsum('bqkgl,blkd->bqkgd', p_w, vw.astype(F32))
    g = gates.astype(F32)[..., None]
    o = g[:, :, 0] * o_c + g[:, :, 1] * o_s + g[:, :, 2] * o_w
    return o.reshape(b, lq, NSA_QW).astype(q.dtype)


def hybrid_prompt(x, norm_w, w_in_p, gdn_conv_w, gdn_a_log, gdn_dt_bias, gdn_norm_w, cmp_pe, cmp_w1, cmp_w2, w_out_b):
    b, l, d = x.shape
    x2 = x.reshape(b * l, d)
    proj = rms_mm(x2, norm_w, w_in_p).reshape(b, l, -1)
    hist = jnp.zeros((b, 8, GDN_CONV_DIM), F32)
    q_a, k_a, v_a, g_log, sig, q_n, cmp2d, sel2d, win2d, kvb = hyb_prep(
        proj, gdn_conv_w, gdn_a_log, gdn_dt_bias, jnp.arange(l), hist)
    s0 = jnp.zeros((b, GDN_HEADS, GDN_DK, GDN_DV), F32)
    gdn_out, s_new = gdn_gated(q_a, k_a, v_a, g_log, sig, proj, gdn_norm_w, s0,
                               g_off=0, b_off=GDN_BETA_LANE, z_blk=HYB_Z_BLK)
    nsa_out = nsa_prompt(q_n, cmp2d, kvb, sig, cmp_pe, cmp_w1.astype(BF16), cmp_w2.astype(BF16))
    mix = jnp.concatenate([gdn_out, nsa_out], axis=-1).reshape(b * l, -1).astype(BF16)
    out = mm_resid(mix, w_out_b, x2).reshape(b, l, d)
    rows5 = lambda r: r.reshape(b, l, 2, NSA_KV_HEADS, HEAD_DIM)
    new_conv = proj[:, -(GDN_CONV - 1):, HYB_QKV:HYB_QKV + GDN_CONV_DIM]
    return out, rows5(cmp2d), rows5(sel2d), rows5(win2d)[:, -min(WINDOW, l):], new_conv, s_new


def hybrid_sample(x, norm_w, pos0, w_in_p, gdn_conv_w, gdn_a_log, gdn_dt_bias, gdn_norm_w, cmp_pe, cmp_w1, cmp_w2,
                  w_out_b, gdn_conv_buf, gdn_s0, past_cmp, past_sel, win_buf):
    b, l, d = x.shape
    x2 = x.reshape(b * l, d)
    proj = rms_mm(x2, norm_w, w_in_p)[:, :HYB_IN].reshape(b, l, HYB_IN)
    (gq, gk, gv, gz, nq, ck, cv, sk, sv, wk, wv, ga, gb, ng) = split_cols(proj, [HYB_SPLITS[i] for i in HYB_ORDER])
    qkv_ext = jnp.concatenate([gdn_conv_buf, jnp.concatenate([gq, gk, gv], axis=-1)], axis=1)
    new_conv = qkv_ext[:, -(GDN_CONV - 1):]
    qkv = jax.nn.silu(causal_dwconv(qkv_ext, gdn_conv_w))
    q_a, k_a, v_a = split_cols(qkv, (GDN_QK, GDN_QK, GDN_VW))
    q_a = l2norm(q_a.reshape(b, l, GDN_HEADS, GDN_DK)) * GDN_DK ** -0.5
    k_a = l2norm(k_a.reshape(b, l, GDN_HEADS, GDN_DK))
    v_a = v_a.reshape(b, l, GDN_HEADS, GDN_DV)
    beta = jax.nn.sigmoid(gb.astype(F32))
    g_log = -jnp.exp(gdn_a_log.astype(F32)) * jax.nn.softplus(ga.astype(F32) + gdn_dt_bias.astype(F32))
    lpad = (-l) % GDN_CHUNK
    flat_pad = lambda a: jnp.pad(a.reshape(b, l, -1), ((0, 0), (0, lpad), (0, 0)))
    gdn_out, s_new = gdn_gated(flat_pad(q_a), flat_pad(k_a), flat_pad(v_a), flat_pad(g_log), flat_pad(beta),
                               flat_pad(gz), gdn_norm_w, gdn_s0)
    gdn_out = gdn_out[:, :l]
    pos = pos0 + jnp.arange(l)
    q_b = rope(nq.reshape(b, l, NSA_HEADS, HEAD_DIM), pos)
    kvr = lambda a: a.reshape(b, l, NSA_KV_HEADS, HEAD_DIM)
    cmp_rows = jnp.stack([rope(kvr(ck), pos), kvr(cv)], axis=2)
    sel_rows = jnp.stack([rope(kvr(sk), pos), kvr(sv)], axis=2)
    win_rows = jnp.stack([rope(kvr(wk), pos), kvr(wv)], axis=2)
    gates = jax.nn.sigmoid(ng.astype(F32)).reshape(b, l, 3, NSA_KV_HEADS, NSA_GROUP)
    w_buf = win_buf.shape[1]
    win_all = jnp.concatenate([win_buf, win_rows], axis=1)
    new_win = win_all[:, -w_buf:]
    nsa_out = sample_nsa(q_b, past_cmp, past_sel, sel_rows, win_all, gates, cmp_pe, cmp_w1, cmp_w2, pos0, w_buf)
    mix = jnp.concatenate([gdn_out, nsa_out], axis=-1).reshape(b * l, -1).astype(BF16)
    out = mm_resid(mix, w_out_b, x2).reshape(b, l, d)
    return out, cmp_rows, sel_rows, new_win, new_conv, s_new


def ssm_mixer(x, norm_w, w_in_p, conv_w, conv_b, a_log, dt_bias, d_skip, gn_w, w_out_b, conv_buf, h0):
    b, l, d = x.shape
    x2 = x.reshape(b * l, d)
    proj = rms_mm(x2, norm_w, w_in_p).reshape(b, l, -1)
    xbc = proj[:, -(SSM_CONV - 1):, SSM_D_INNER:SSM_D_INNER + SSM_CONV_DIM]
    new_conv = jnp.concatenate([conv_buf, xbc], axis=1)[:, -(SSM_CONV - 1):]
    proj = jnp.pad(proj, ((0, 0), (0, (-l) % SSM_CHUNK), (0, 0)))
    y, h_new = ssd_fused(proj, l, conv_w, conv_b, a_log, dt_bias, d_skip, gn_w, conv_buf, h0)
    y = y[:, :l].reshape(b * l, SSM_D_INNER)
    return mm_resid(y, w_out_b, x2).reshape(b, l, d), new_conv, h_new


def gather_pages(pool, page_table):
    g = pool[page_table]
    return g.reshape((g.shape[0], g.shape[1] * g.shape[2]) + g.shape[3:])


def _pad_cols(w, mult):
    n = w.shape[-1]
    return jnp.pad(w, ((0, 0), (0, (-n) % mult)))


def kernel(x_prompt, x_sample, cache_cmp_kv, cache_sel_kv, cache_win_kv, state_gdn_conv, state_gdn, state_ssm_conv, state_ssm, page_table, hyb_norm_mix, hyb_w_in, hyb_gdn_conv_w, hyb_gdn_a_log, hyb_gdn_dt_bias, hyb_gdn_norm_w, hyb_cmp_pe, hyb_cmp_w1, hyb_cmp_w2, hyb_w_out, hyb_norm_ffn, ffn_w_gate, ffn_w_up, ffn_w_down, ssm_norm_mix, ssm_w_in, ssm_conv_w, ssm_conv_b, ssm_a_log, ssm_dt_bias, ssm_d_skip, ssm_norm_w, ssm_w_out, ssm_norm_ffn, moe_router, moe_w_gate, moe_w_up, moe_w_down, final_norm):
    hp, hs = x_prompt, x_sample
    bp, lp, d = hp.shape
    bs, ls, _ = hs.shape

    w_in_p = _pad_cols(reorder_hyb_cols(hyb_w_in[0]).astype(BF16), 512)
    w_out_b = hyb_w_out[0].astype(BF16)
    hw = (w_in_p, hyb_gdn_conv_w[0], hyb_gdn_a_log[0], hyb_gdn_dt_bias[0], hyb_gdn_norm_w[0],
          hyb_cmp_pe[0], hyb_cmp_w1[0], hyb_cmp_w2[0], w_out_b)
    hp, cmp_p, sel_p, win_p, gconv_p, gst_p = hybrid_prompt(hp, hyb_norm_mix[0], *hw)
    past_cmp = page_gather(cache_cmp_kv[0], page_table, F32)
    past_sel = page_gather(cache_sel_kv[0], page_table, BF16)
    hs, cmp_s, sel_s, win_s, gconv_s, gst_s = hybrid_sample(
        hs, hyb_norm_mix[0], PAST_LEN, *hw, state_gdn_conv[0], state_gdn[0], past_cmp, past_sel, cache_win_kv[0])

    wg, wu, wd = ffn_w_gate[0].astype(BF16), ffn_w_up[0].astype(BF16), ffn_w_down[0].astype(BF16)

    def dense_ffn(x):
        b, l, _ = x.shape
        x2 = x.reshape(b * l, d)
        return mm_resid(rms_glu(x2, hyb_norm_ffn[0], wg, wu), wd, x2).reshape(b, l, d)

    hp = dense_ffn(hp)
    hs = dense_ffn(hs)

    sw_in_p = _pad_cols(ssm_w_in[0].astype(BF16), 512)
    sw_out_b = ssm_w_out[0].astype(BF16)
    sw = (sw_in_p, ssm_conv_w[0], ssm_conv_b[0], ssm_a_log[0], ssm_dt_bias[0], ssm_d_skip[0], ssm_norm_w[0], sw_out_b)
    zero_conv = jnp.zeros((bp, SSM_CONV - 1, SSM_CONV_DIM), hp.dtype)
    zero_h = jnp.zeros((bp, SSM_HEADS, SSM_HEAD_DIM, SSM_STATE), hp.dtype)
    hp, sconv_p, sst_p = ssm_mixer(hp, ssm_norm_mix[0], *sw, zero_conv, zero_h)
    hs, sconv_s, sst_s = ssm_mixer(hs, ssm_norm_mix[0], *sw, state_ssm_conv[0], state_ssm[0])

    tok = jnp.concatenate([hp.reshape(bp * lp, d), hs.reshape(bs * ls, d)], axis=0)
    n_tok = tok.shape[0]
    tok_pad = jnp.pad(tok, ((0, (-n_tok) % MOE_TM), (0, 0)))
    y0, y1, top_w = moe_ffn(tok_pad, ssm_norm_ffn[0], moe_router[0], moe_w_gate[0], moe_w_up[0], moe_w_down[0])
    y = combine_rms(tok_pad, y0, y1, top_w, final_norm)
    y_prompt = y[:bp * lp].reshape(bp, lp, d)
    y_sample = y[bp * lp:n_tok].reshape(bs, ls, d)
    st = lambda a: a[None]
    return (y_prompt, y_sample,
            st(cmp_p), st(cmp_s), st(sel_p), st(sel_s), st(win_p), st(win_s),
            st(gconv_p), st(gconv_s), st(gst_p), st(gst_s),
            st(sconv_p), st(sconv_s), st(sst_p), st(sst_s))
```

```python
import functools
import math

import jax
import jax.numpy as jnp
from jax import lax
from jax.experimental import pallas as pl
from jax.experimental.pallas import tpu as pltpu

D_MODEL = 2048
PAST_LEN = 16384
HEAD_DIM = 128
ROPE_THETA = 10000.0
RMS_EPS = 1e-6

GDN_HEADS = D_MODEL // 256
GDN_DK = 128
GDN_DV = 128
GDN_CONV = 4
GDN_CHUNK = 64
GDN_QK = GDN_HEADS * GDN_DK
GDN_VW = GDN_HEADS * GDN_DV
GDN_CONV_DIM = 2 * GDN_QK + GDN_VW

NSA_HEADS = D_MODEL // 256
NSA_KV_HEADS = 2
NSA_GROUP = NSA_HEADS // NSA_KV_HEADS
NSA_QW = NSA_HEADS * HEAD_DIM
NSA_KVW = NSA_KV_HEADS * HEAD_DIM
CMP_BLOCK = 32
CMP_STRIDE = 16
SEL_BLOCK = 64
SEL_TOPK = 16
WINDOW = 512
Q_BLOCK = 128
FORCE_BONUS = 1e4

HYB_SPLITS = (GDN_QK, GDN_QK, GDN_VW, GDN_VW, GDN_HEADS, GDN_HEADS,
              NSA_QW, NSA_KVW, NSA_KVW, NSA_KVW, NSA_KVW, NSA_KVW, NSA_KVW, 3 * NSA_HEADS)
HYB_IN = sum(HYB_SPLITS)

SSM_D_INNER = 2 * D_MODEL
SSM_HEAD_DIM = 64
SSM_HEADS = SSM_D_INNER // SSM_HEAD_DIM
SSM_GROUPS = 8
SSM_HPG = SSM_HEADS // SSM_GROUPS
SSM_STATE = 128
SSM_CONV = 4
SSM_CHUNK = 128
SSM_BC = SSM_GROUPS * SSM_STATE
SSM_CONV_DIM = SSM_D_INNER + 2 * SSM_BC
SSM_IN = SSM_D_INNER + SSM_CONV_DIM + SSM_HEADS

N_EXPERTS = 8
TOP_K = 2

VMEM_LIMIT_BYTES = 56 * 1024 * 1024
LANES = 128

F32 = jnp.float32
BF16 = jnp.bfloat16


def _params(*sem):
    return pltpu.CompilerParams(dimension_semantics=sem, vmem_limit_bytes=VMEM_LIMIT_BYTES)


def _pick(n, prefs):
    for p in prefs:
        if n % p == 0:
            return p
    return n


def _rms_to_bf16(x, g):
    ms = jnp.mean(x * x, axis=-1, keepdims=True)
    return (x * lax.rsqrt(ms + RMS_EPS) * g).astype(BF16)


def _rms_mm_kernel(x_ref, g_ref, w_ref, o_ref, a_scr):
    @pl.when(pl.program_id(1) == 0)
    def _():
        a_scr[...] = _rms_to_bf16(x_ref[...], g_ref[...])

    o_ref[...] = jnp.dot(a_scr[...], w_ref[...], preferred_element_type=F32).astype(o_ref.dtype)


def rms_mm(x, gain, w, out_dtype=F32):
    m, k = x.shape
    n = w.shape[1]
    tm = _pick(m, (1024, 512, 256, 128))
    tn = _pick(n, (1024, 768, 512, 256, 128))
    return pl.pallas_call(
        _rms_mm_kernel,
        grid=(m // tm, n // tn),
        in_specs=[pl.BlockSpec((tm, k), lambda i, j: (i, 0)),
                  pl.BlockSpec((1, k), lambda i, j: (0, 0)),
                  pl.BlockSpec((k, tn), lambda i, j: (0, j))],
        out_specs=pl.BlockSpec((tm, tn), lambda i, j: (i, j)),
        out_shape=jax.ShapeDtypeStruct((m, n), out_dtype),
        scratch_shapes=[pltpu.VMEM((tm, k), BF16)],
        compiler_params=_params("parallel", "arbitrary"),
        name="rms_mm",
    )(x, gain.reshape(1, k).astype(F32), w)


def _rms_glu_kernel(x_ref, g_ref, wg_ref, wu_ref, o_ref, a_scr):
    @pl.when(pl.program_id(1) == 0)
    def _():
        a_scr[...] = _rms_to_bf16(x_ref[...], g_ref[...])

    a = a_scr[...]
    gt = jnp.dot(a, wg_ref[...], preferred_element_type=F32)
    up = jnp.dot(a, wu_ref[...], preferred_element_type=F32)
    o_ref[...] = (gt * jax.nn.sigmoid(gt) * up).astype(o_ref.dtype)


def rms_glu(x, gain, wg, wu):
    m, k = x.shape
    n = wg.shape[1]
    tm = _pick(m, (1024, 512, 256, 128))
    tn = _pick(n, (512, 256, 128))
    return pl.pallas_call(
        _rms_glu_kernel,
        grid=(m // tm, n // tn),
        in_specs=[pl.BlockSpec((tm, k), lambda i, j: (i, 0)),
                  pl.BlockSpec((1, k), lambda i, j: (0, 0)),
                  pl.BlockSpec((k, tn), lambda i, j: (0, j)),
                  pl.BlockSpec((k, tn), lambda i, j: (0, j))],
        out_specs=pl.BlockSpec((tm, tn), lambda i, j: (i, j)),
        out_shape=jax.ShapeDtypeStruct((m, n), BF16),
        scratch_shapes=[pltpu.VMEM((tm, k), BF16)],
        compiler_params=_params("parallel", "arbitrary"),
        name="rms_glu",
    )(x, gain.reshape(1, k).astype(F32), wg, wu)


def _mm_resid_kernel(a_ref, w_ref, r_ref, o_ref, acc_ref, *, nk):
    kk = pl.program_id(2)

    @pl.when(kk == 0)
    def _():
        acc_ref[...] = r_ref[...]

    acc_ref[...] += jnp.dot(a_ref[...], w_ref[...], preferred_element_type=F32)

    @pl.when(kk == nk - 1)
    def _():
        o_ref[...] = acc_ref[...]


def mm_resid(a, w, resid):
    m, k = a.shape
    n = w.shape[1]
    tm = _pick(m, (1024, 512, 256, 128))
    tn = _pick(n, (1024, 512, 256, 128))
    tk = _pick(k, (2048, 1792, 1408, 1024, 512))
    nk = k // tk
    return pl.pallas_call(
        functools.partial(_mm_resid_kernel, nk=nk),
        grid=(m // tm, n // tn, nk),
        in_specs=[pl.BlockSpec((tm, tk), lambda i, j, kk: (i, kk)),
                  pl.BlockSpec((tk, tn), lambda i, j, kk: (kk, j)),
                  pl.BlockSpec((tm, tn), lambda i, j, kk: (i, j))],
        out_specs=pl.BlockSpec((tm, tn), lambda i, j, kk: (i, j)),
        out_shape=jax.ShapeDtypeStruct((m, n), F32),
        scratch_shapes=[pltpu.VMEM((tm, tn), F32)],
        compiler_params=_params("parallel", "parallel", "arbitrary"),
        name="mm_resid",
    )(a, w, resid)


def _combine_rms_kernel(x_ref, y0_ref, y1_ref, w_ref, g_ref, o_ref):
    w = w_ref[...]
    x = x_ref[...] + (y0_ref[...] * w[:, 0:1] + y1_ref[...] * w[:, 1:2])
    ms = jnp.mean(x * x, axis=-1, keepdims=True)
    o_ref[...] = x * lax.rsqrt(ms + RMS_EPS) * g_ref[...]


def combine_rms(x, y0, y1, w, gain):
    m, k = x.shape
    tm = _pick(m, (256, 128))
    row = pl.BlockSpec((tm, k), lambda i: (i, 0))
    return pl.pallas_call(
        _combine_rms_kernel,
        grid=(m // tm,),
        in_specs=[row, row, row, pl.BlockSpec((tm, TOP_K), lambda i: (i, 0)),
                  pl.BlockSpec((1, k), lambda i: (0, 0))],
        out_specs=row,
        out_shape=jax.ShapeDtypeStruct((m, k), F32),
        compiler_params=_params("parallel"),
        name="combine_rms",
    )(x, y0, y1, w, gain.reshape(1, k).astype(F32))


def _rms_router_kernel(x_ref, g_ref, wh_ref, wl_ref, h_ref, lg_ref):
    x = x_ref[...]
    ms = jnp.mean(x * x, axis=-1, keepdims=True)
    h = x * lax.rsqrt(ms + RMS_EPS) * g_ref[...]
    hh = h.astype(BF16)
    hl = (h - hh.astype(F32)).astype(BF16)
    h_ref[...] = hh
    wh = wh_ref[...]
    wl = wl_ref[...]
    lg = jnp.dot(hh, wh, preferred_element_type=F32)
    lg += jnp.dot(hl, wh, preferred_element_type=F32)
    lg += jnp.dot(hh, wl, preferred_element_type=F32)
    lg_ref[...] = lg


def rms_router(x, gain, router):
    m, k = x.shape
    e = router.shape[1]
    rp = jnp.pad(router.astype(F32), ((0, 0), (0, LANES - e)))
    rh = rp.astype(BF16)
    rl = (rp - rh.astype(F32)).astype(BF16)
    tm = _pick(m, (512, 256, 128, 32))
    h, lg = pl.pallas_call(
        _rms_router_kernel,
        grid=(m // tm,),
        in_specs=[pl.BlockSpec((tm, k), lambda i: (i, 0)),
                  pl.BlockSpec((1, k), lambda i: (0, 0)),
                  pl.BlockSpec((k, LANES), lambda i: (0, 0)),
                  pl.BlockSpec((k, LANES), lambda i: (0, 0))],
        out_specs=[pl.BlockSpec((tm, k), lambda i: (i, 0)),
                   pl.BlockSpec((tm, LANES), lambda i: (i, 0))],
        out_shape=[jax.ShapeDtypeStruct((m, k), BF16), jax.ShapeDtypeStruct((m, LANES), F32)],
        compiler_params=_params("parallel"),
        name="rms_router",
    )(x, gain.reshape(1, k).astype(F32), rh, rl)
    return h, lg[:, :e]


MOE_TM = 256


def _moe_glu_kernel(te_ref, tv_ref, tf_ref, a_ref, wg_ref, wu_ref, o_ref, wg_b, wu_b):
    i = pl.program_id(1)

    @pl.when(tf_ref[i] != 0)
    def _():
        wg_b[...] = wg_ref[...].astype(BF16)
        wu_b[...] = wu_ref[...].astype(BF16)

    @pl.when(tv_ref[i] != 0)
    def _():
        a = a_ref[...]
        gt = jnp.dot(a, wg_b[...], preferred_element_type=F32)
        up = jnp.dot(a, wu_b[...], preferred_element_type=F32)
        o_ref[...] = (gt * jax.nn.sigmoid(gt) * up).astype(o_ref.dtype)

    @pl.when(tv_ref[i] == 0)
    def _():
        o_ref[...] = jnp.zeros_like(o_ref)


def _moe_down_kernel(te_ref, tv_ref, tf_ref, a_ref, w_ref, o_ref, w_b):
    i = pl.program_id(1)

    @pl.when(tf_ref[i] != 0)
    def _():
        w_b[...] = w_ref[...].astype(BF16)

    @pl.when(tv_ref[i] != 0)
    def _():
        o_ref[...] = jnp.dot(a_ref[...], w_b[...], preferred_element_type=F32)

    @pl.when(tv_ref[i] == 0)
    def _():
        o_ref[...] = jnp.zeros_like(o_ref)


def moe_experts(a_sorted, tile_expert, tile_valid, tile_first, wg, wu, wd):
    r, d = a_sorted.shape
    f = wg.shape[2]
    tm = MOE_TM
    tf = _pick(f, (1024, 512))
    act = pl.pallas_call(
        _moe_glu_kernel,
        grid_spec=pltpu.PrefetchScalarGridSpec(
            num_scalar_prefetch=3,
            grid=(f // tf, r // tm),
            in_specs=[pl.BlockSpec((tm, d), lambda j, i, te, tv, t1: (i, 0)),
                      pl.BlockSpec((None, d, tf), lambda j, i, te, tv, t1: (te[i], 0, j)),
                      pl.BlockSpec((None, d, tf), lambda j, i, te, tv, t1: (te[i], 0, j))],
            out_specs=pl.BlockSpec((tm, tf), lambda j, i, te, tv, t1: (i, j)),
            scratch_shapes=[pltpu.VMEM((d, tf), BF16), pltpu.VMEM((d, tf), BF16)],
        ),
        out_shape=jax.ShapeDtypeStruct((r, f), BF16),
        compiler_params=_params("arbitrary", "arbitrary"),
        name="moe_glu",
    )(tile_expert, tile_valid, tile_first, a_sorted, wg, wu)
    tn = _pick(d, (512,))
    return pl.pallas_call(
        _moe_down_kernel,
        grid_spec=pltpu.PrefetchScalarGridSpec(
            num_scalar_prefetch=3,
            grid=(d // tn, r // tm),
            in_specs=[pl.BlockSpec((tm, f), lambda j, i, te, tv, t1: (i, 0)),
                      pl.BlockSpec((None, f, tn), lambda j, i, te, tv, t1: (te[i], 0, j))],
            out_specs=pl.BlockSpec((tm, tn), lambda j, i, te, tv, t1: (i, j)),
            scratch_shapes=[pltpu.VMEM((f, tn), BF16)],
        ),
        out_shape=jax.ShapeDtypeStruct((r, d), F32),
        compiler_params=_params("arbitrary", "arbitrary"),
        name="moe_down",
    )(tile_expert, tile_valid, tile_first, act, wd)


def moe_ffn(x, gain, router, wg, wu, wd):
    t, d = x.shape
    tm = MOE_TM
    h, logits = rms_router(x, gain, router)
    top_v, top_i = lax.top_k(logits, TOP_K)
    top_w = jax.nn.softmax(top_v, axis=-1)
    n_asg = t * TOP_K
    n_rows = (-(-n_asg // tm) + N_EXPERTS) * tm
    flat_e = top_i.reshape(-1).astype(jnp.int32)
    order = jnp.argsort(flat_e, stable=True).astype(jnp.int32)
    sorted_e = flat_e[order]
    counts = jnp.sum(flat_e[:, None] == jnp.arange(N_EXPERTS, dtype=jnp.int32)[None, :], axis=0).astype(jnp.int32)
    padded = ((counts + tm - 1) // tm) * tm
    pad_end = jnp.cumsum(padded)
    pad_start = pad_end - padded
    start = jnp.cumsum(counts) - counts
    dest = pad_start[sorted_e] + (jnp.arange(n_asg, dtype=jnp.int32) - start[sorted_e])
    row_src = jnp.zeros((n_rows,), jnp.int32).at[dest].set(order // TOP_K)
    pos = jnp.zeros((n_asg,), jnp.int32).at[order].set(dest)
    tile_start = jnp.arange(n_rows // tm, dtype=jnp.int32) * tm
    tile_valid = (tile_start < pad_end[-1]).astype(jnp.int32)
    tile_expert = jnp.minimum(jnp.searchsorted(pad_end, tile_start, side="right"), N_EXPERTS - 1).astype(jnp.int32)
    last_e = tile_expert[jnp.maximum(pad_end[-1] // tm - 1, 0)]
    tile_expert = jnp.where(tile_valid != 0, tile_expert, last_e)
    tile_first = jnp.concatenate([jnp.ones((1,), jnp.int32),
                                  (tile_expert[1:] != tile_expert[:-1]).astype(jnp.int32)])
    a_sorted = jnp.take(h, row_src, axis=0)
    y = moe_experts(a_sorted, tile_expert, tile_valid, tile_first, wg, wu, wd)
    pos = pos.reshape(t, TOP_K)
    return jnp.take(y, pos[:, 0], axis=0), jnp.take(y, pos[:, 1], axis=0), top_w


NEG_BIG = -1e30
NSA_TQ = 128
SEL_CHUNK = 512
CMP_HIDDEN = 256


def _dot_nt(a, b):
    return lax.dot_general(a, b, (((1,), (1,)), ((), ())), preferred_element_type=F32)


def _cmp_mlp_kernel(x_ref, pe_ref, w1_ref, w2_ref, o_ref, hi_scr, *, n_seg):
    lo = jnp.zeros((n_seg, CMP_HIDDEN), F32)
    hi = jnp.zeros((n_seg, CMP_HIDDEN), F32)
    for r in range(CMP_STRIDE):
        xr = x_ref[pl.ds(r, n_seg, stride=CMP_STRIDE), :]
        a_lo = (xr + pe_ref[r:r + 1, :]).astype(BF16)
        a_hi = (xr + pe_ref[CMP_STRIDE + r:CMP_STRIDE + r + 1, :]).astype(BF16)
        lo += jnp.dot(a_lo, w1_ref[r * HEAD_DIM:(r + 1) * HEAD_DIM, :], preferred_element_type=F32)
        hi += jnp.dot(a_hi, w1_ref[(CMP_STRIDE + r) * HEAD_DIM:(CMP_STRIDE + r + 1) * HEAD_DIM, :],
                      preferred_element_type=F32)
    hi_scr[0:n_seg, :] = hi
    hi_scr[n_seg:n_seg + 8, :] = jnp.zeros((8, CMP_HIDDEN), F32)
    pre = lo + hi_scr[pl.ds(1, n_seg), :]
    hid = pre * jax.nn.sigmoid(pre)
    o_ref[...] = jnp.dot(hid.astype(BF16), w2_ref[...], preferred_element_type=F32)


def cmp_mlp_prompt(rows2d, pe, w1b, w2b):
    if rows2d.ndim == 4:
        b, _, l, _ = rows2d.shape
        row_spec = pl.BlockSpec((None, None, l, HEAD_DIM), lambda i, s, k: (i, s * NSA_KV_HEADS + k, 0, 0))
    else:
        b, l, _ = rows2d.shape
        row_spec = pl.BlockSpec((None, l, HEAD_DIM), lambda i, s, k: (i, 0, s * NSA_KV_HEADS + k))
    n_seg = l // CMP_STRIDE
    return pl.pallas_call(
        functools.partial(_cmp_mlp_kernel, n_seg=n_seg),
        grid=(b, 2, NSA_KV_HEADS),
        in_specs=[row_spec,
                  pl.BlockSpec((None, CMP_BLOCK, HEAD_DIM), lambda i, s, k: (s, 0, 0)),
                  pl.BlockSpec((None, CMP_BLOCK * HEAD_DIM, CMP_HIDDEN), lambda i, s, k: (s, 0, 0)),
                  pl.BlockSpec((None, CMP_HIDDEN, HEAD_DIM), lambda i, s, k: (s, 0, 0))],
        out_specs=pl.BlockSpec((None, None, None, n_seg, HEAD_DIM), lambda i, s, k: (i, s, k, 0, 0)),
        out_shape=jax.ShapeDtypeStruct((b, 2, NSA_KV_HEADS, n_seg, HEAD_DIM), F32),
        scratch_shapes=[pltpu.VMEM((n_seg + 8, CMP_HIDDEN), F32)],
        compiler_params=_params("parallel", "parallel", "parallel"),
        name="cmp_mlp_prompt",
    )(rows2d, pe, w1b, w2b)


def _nsa_prompt_kernel(q_ref, kc_ref, vc_ref, ks_ref, vs_ref, kw_ref, vw_ref, g_ref, o_ref,
                       m_scr, acc_scr, o_scr, sc_scr, *, tq, n_cmp, n_sel):
    t = pl.program_id(2)
    kvh = pl.program_id(1)
    ng = NSA_GROUP
    rows = ng * tq
    q = q_ref[...].reshape(rows, HEAD_DIM)
    qpos = t * tq + lax.broadcasted_iota(jnp.int32, (tq, 1), 0)

    def tile_rows(x):
        return jnp.concatenate([x] * ng, axis=0)

    lane = lax.broadcasted_iota(jnp.int32, (LANES, 3 * ng * LANES), 0)
    slot = lax.broadcasted_iota(jnp.int32, (LANES, 3 * ng * LANES), 1) // LANES
    pick = (lane == NSA_GATE_LANE + (slot // ng) * NSA_HEADS + kvh * ng + slot % ng).astype(BF16)
    gate_all = _dot_exact01_right(g_ref[...], pick)

    def gate_rows(branch):
        return jnp.concatenate([gate_all[:, (branch * ng + h) * LANES:(branch * ng + h + 1) * LANES]
                                for h in range(ng)], axis=0)

    def reset():
        m_scr[...] = jnp.full(m_scr.shape, NEG_BIG, F32)
        acc_scr[...] = jnp.zeros(acc_scr.shape, F32)

    def update(k, v, mask):
        s = _dot_nt(q, k)
        mask4 = tile_rows(mask)
        sm = jnp.where(mask4, s, NEG_BIG)
        m_prev = m_scr[:, :1]
        m_new = jnp.maximum(m_prev, jnp.max(sm, axis=1, keepdims=True))
        e = jnp.where(mask4, jnp.exp(sm - m_new), 0.0)
        alpha = jnp.exp(m_prev - m_new)
        m_scr[...] = jnp.broadcast_to(m_new, m_scr.shape)
        v1 = jnp.concatenate([v, jnp.ones_like(v)], axis=1)
        acc_scr[...] = alpha * acc_scr[...] + jnp.dot(e.astype(BF16), v1, preferred_element_type=F32)

    def finish():
        den = acc_scr[:, HEAD_DIM:]
        return acc_scr[:, :HEAD_DIM] / jnp.where(den > 0, den, 1.0)

    n_cp = kc_ref.shape[0]
    kc = kc_ref[...].astype(BF16)
    vc = vc_ref[...].astype(BF16)
    s = _dot_nt(q, kc)
    cidx = lax.broadcasted_iota(jnp.int32, (1, n_cp), 1)
    mask_c = tile_rows((cidx * CMP_STRIDE + (CMP_BLOCK - 1) <= qpos) & (cidx < n_cmp))
    sm = jnp.where(mask_c, s, NEG_BIG)
    e = jnp.where(mask_c, jnp.exp(sm - jnp.max(sm, axis=1, keepdims=True)), 0.0)
    den = jnp.sum(e, axis=1, keepdims=True)
    p = e / jnp.where(den > 0, den, 1.0)
    o_scr[...] = gate_rows(0) * jnp.dot(p.astype(BF16), vc, preferred_element_type=F32)
    p_sum = p[0:tq]
    for h in range(1, ng):
        p_sum = p_sum + p[h * tq:(h + 1) * tq]
    p_hi = p_sum.astype(BF16)
    p_lo = (p_sum - p_hi.astype(F32)).astype(BF16)
    cs = lax.broadcasted_iota(jnp.int32, (n_cp, LANES), 0) * CMP_STRIDE
    ss = lax.broadcasted_iota(jnp.int32, (n_cp, LANES), 1) * SEL_BLOCK
    overlap = ((cs < ss + SEL_BLOCK) & (cs + CMP_BLOCK > ss) & (cs < n_cmp * CMP_STRIDE)).astype(BF16)
    imp = jnp.dot(p_hi, overlap, preferred_element_type=F32) + jnp.dot(p_lo, overlap, preferred_element_type=F32)

    blk = lax.broadcasted_iota(jnp.int32, (tq, LANES), 1)
    cur = qpos // SEL_BLOCK
    forced = (blk == 0) | (blk == cur) | (blk == cur - 1)
    score = jnp.where((blk <= cur) & (blk < n_sel), imp + jnp.where(forced, FORCE_BONUS, 0.0), NEG_BIG)
    n_sp = sc_scr.shape[0]
    s_t = score.T[0:n_sp]
    sc_scr[...] = s_t
    jidx = lax.broadcasted_iota(jnp.int32, (n_sp, tq), 0)
    rank = jnp.zeros((n_sp, tq), jnp.int32)
    for i in range(n_sel):
        si = sc_scr[i:i + 1, :]
        rank += ((si > s_t) | ((si == s_t) & (i < jidx))).astype(jnp.int32)
    sel_t = ((rank < SEL_TOPK) & (s_t > 0.5 * NEG_BIG)).astype(F32)
    if n_sp < LANES:
        sel_t = jnp.concatenate([sel_t, jnp.zeros((LANES - n_sp, tq), F32)], axis=0)
    sel = sel_t.T.astype(BF16)

    reset()

    def sel_step(c, carry):
        start = pl.multiple_of(c * SEL_CHUNK, SEL_CHUNK)
        jrow = lax.broadcasted_iota(jnp.int32, (LANES, SEL_CHUNK), 0)
        kblk = (start + lax.broadcasted_iota(jnp.int32, (LANES, SEL_CHUNK), 1)) // SEL_BLOCK
        expand = (jrow == kblk).astype(BF16)
        chosen = jnp.dot(sel, expand, preferred_element_type=F32) > 0.5
        kpos = start + lax.broadcasted_iota(jnp.int32, (1, SEL_CHUNK), 1)
        update(ks_ref[pl.ds(start, SEL_CHUNK), :], vs_ref[pl.ds(start, SEL_CHUNK), :], chosen & (kpos <= qpos))
        return carry

    lax.fori_loop(0, (t * tq + tq + SEL_CHUNK - 1) // SEL_CHUNK, sel_step, 0)
    o_scr[...] += gate_rows(1) * finish()

    reset()
    span = WINDOW + tq
    start = pl.multiple_of(jnp.maximum(t - WINDOW // tq, 0) * tq, tq)
    dist = qpos - (start + lax.broadcasted_iota(jnp.int32, (1, span), 1))
    update(kw_ref[pl.ds(start, span), :], vw_ref[pl.ds(start, span), :], (dist >= 0) & (dist < WINDOW))
    out = o_scr[...] + gate_rows(2) * finish()
    for h in range(ng):
        o_ref[:, h * HEAD_DIM:(h + 1) * HEAD_DIM] = out[h * tq:(h + 1) * tq]


def nsa_prompt(q, cmp2d, kvb, sig, pe, w1b, w2b):
    b, _, l, _ = q.shape
    tq = NSA_TQ
    n_cmp = (l - CMP_BLOCK) // CMP_STRIDE + 1
    n_sel = -(-l // SEL_BLOCK)
    kvc = cmp_mlp_prompt(cmp2d, pe, w1b, w2b)
    n_cp = kvc.shape[3]
    n_sp = -(-n_sel // 8) * 8
    rows = NSA_GROUP * tq
    kv_spec = lambda s: pl.BlockSpec((None, None, None, l, HEAD_DIM), lambda i, k, t: (i, s, k, 0, 0))
    return pl.pallas_call(
        functools.partial(_nsa_prompt_kernel, tq=tq, n_cmp=n_cmp, n_sel=n_sel),
        grid=(b, NSA_KV_HEADS, l // tq),
        in_specs=[pl.BlockSpec((None, NSA_GROUP, tq, HEAD_DIM), lambda i, k, t: (i, k, t, 0)),
                  pl.BlockSpec((None, None, None, n_cp, HEAD_DIM), lambda i, k, t: (i, 0, k, 0, 0)),
                  pl.BlockSpec((None, None, None, n_cp, HEAD_DIM), lambda i, k, t: (i, 1, k, 0, 0)),
                  kv_spec(0), kv_spec(1), kv_spec(2), kv_spec(3),
                  pl.BlockSpec((None, tq, LANES), lambda i, k, t: (i, t, 0))],
        out_specs=pl.BlockSpec((None, tq, NSA_GROUP * HEAD_DIM), lambda i, k, t: (i, t, k)),
        out_shape=jax.ShapeDtypeStruct((b, l, NSA_QW), F32),
        scratch_shapes=[pltpu.VMEM((rows, LANES), F32), pltpu.VMEM((rows, 2 * HEAD_DIM), F32),
                        pltpu.VMEM((rows, HEAD_DIM), F32), pltpu.VMEM((n_sp, tq), F32)],
        compiler_params=_params("parallel", "parallel", "arbitrary"),
        name="nsa_prompt",
    )(q, kvc, kvc, kvb, kvb, kvb, kvb, sig)


PAGES_PER_STEP = 8
NSA_SAMPLE_TQ = 8
NEW_ROWS = 128


def _page_gather_kernel(pt_ref, *refs, page, n_col):
    o_ref = refs[-1]
    for i, r in enumerate(refs[:-1]):
        for c in range(n_col):
            o_ref[c, i * page:(i + 1) * page, :] = r[pl.ds(c, page, stride=n_col), :].astype(o_ref.dtype)


def page_gather(pool, page_table, out_dtype):
    b, n_pages = page_table.shape
    n_phys, page = pool.shape[:2]
    n_col = 2 * NSA_KV_HEADS
    g = PAGES_PER_STEP
    pool = pool.reshape(n_phys, page * n_col, HEAD_DIM)

    def page_spec(k):
        return pl.BlockSpec((None, page * n_col, HEAD_DIM), lambda i, j, pt: (pt[i, j * g + k], 0, 0))

    return pl.pallas_call(
        functools.partial(_page_gather_kernel, page=page, n_col=n_col),
        grid_spec=pltpu.PrefetchScalarGridSpec(
            num_scalar_prefetch=1,
            grid=(b, n_pages // g),
            in_specs=[page_spec(k) for k in range(g)],
            out_specs=pl.BlockSpec((None, n_col, g * page, HEAD_DIM), lambda i, j, pt: (i, 0, j, 0)),
        ),
        out_shape=jax.ShapeDtypeStruct((b, n_col, n_pages * page, HEAD_DIM), out_dtype),
        compiler_params=_params("parallel", "arbitrary"),
        name="page_gather",
    )(page_table, *([pool] * g))


def _nsa_sample_kernel(q_ref, kc_ref, vc_ref, ks_ref, vs_ref, kn_ref, vn_ref, kw_ref, vw_ref, g_ref, o_ref,
                       m_scr, acc_scr, o_scr, *, tq, pos0, n_new, n_cmp, n_sel, past, w_buf):
    kvh = pl.program_id(1)
    ng = NSA_GROUP
    rows = ng * tq
    nb = -(-n_sel // LANES) * LANES
    q = q_ref[...].reshape(rows, HEAD_DIM)
    qpos = pos0 + lax.broadcasted_iota(jnp.int32, (tq, 1), 0)

    def tile_rows(x):
        return jnp.concatenate([x] * ng, axis=0)

    lane = lax.broadcasted_iota(jnp.int32, (LANES, 3 * ng * LANES), 0)
    slot = lax.broadcasted_iota(jnp.int32, (LANES, 3 * ng * LANES), 1) // LANES
    pick = (lane == NSA_GATE_LANE + (slot // ng) * NSA_HEADS + kvh * ng + slot % ng).astype(BF16)
    gate_all = _dot_exact01_right(g_ref[...], pick)

    def gate_rows(branch):
        return jnp.concatenate([gate_all[:, (branch * ng + h) * LANES:(branch * ng + h + 1) * LANES]
                                for h in range(ng)], axis=0)

    def reset():
        m_scr[...] = jnp.full(m_scr.shape, NEG_BIG, F32)
        acc_scr[...] = jnp.zeros(acc_scr.shape, F32)

    def update(k, v, mask):
        s = _dot_nt(q, k)
        mask4 = tile_rows(mask)
        sm = jnp.where(mask4, s, NEG_BIG)
        m_prev = m_scr[:, :1]
        m_new = jnp.maximum(m_prev, jnp.max(sm, axis=1, keepdims=True))
        e = jnp.where(mask4, jnp.exp(sm - m_new), 0.0)
        alpha = jnp.exp(m_prev - m_new)
        m_scr[...] = jnp.broadcast_to(m_new, m_scr.shape)
        v1 = jnp.concatenate([v, jnp.ones_like(v)], axis=1)
        acc_scr[...] = alpha * acc_scr[...] + jnp.dot(e.astype(BF16), v1, preferred_element_type=F32)

    def finish():
        den = acc_scr[:, HEAD_DIM:]
        return acc_scr[:, :HEAD_DIM] / jnp.where(den > 0, den, 1.0)

    n_cp = kc_ref.shape[0]
    kc = kc_ref[...].astype(BF16)
    vc = vc_ref[...].astype(BF16)
    s = _dot_nt(q, kc)
    cidx = lax.broadcasted_iota(jnp.int32, (1, n_cp), 1)
    mask_c = tile_rows((cidx * CMP_STRIDE + (CMP_BLOCK - 1) <= qpos) & (cidx < n_cmp))
    sm = jnp.where(mask_c, s, NEG_BIG)
    e = jnp.where(mask_c, jnp.exp(sm - jnp.max(sm, axis=1, keepdims=True)), 0.0)
    den = jnp.sum(e, axis=1, keepdims=True)
    p = e / jnp.where(den > 0, den, 1.0)
    o_scr[...] = gate_rows(0) * jnp.dot(p.astype(BF16), vc, preferred_element_type=F32)
    p_sum = p[0:tq]
    for h in range(1, ng):
        p_sum = p_sum + p[h * tq:(h + 1) * tq]
    p_hi = p_sum.astype(BF16)
    p_lo = (p_sum - p_hi.astype(F32)).astype(BF16)
    cs = lax.broadcasted_iota(jnp.int32, (n_cp, nb), 0) * CMP_STRIDE
    ss = lax.broadcasted_iota(jnp.int32, (n_cp, nb), 1) * SEL_BLOCK
    overlap = ((cs < ss + SEL_BLOCK) & (cs + CMP_BLOCK > ss) & (cs < n_cmp * CMP_STRIDE)).astype(BF16)
    imp = jnp.dot(p_hi, overlap, preferred_element_type=F32) + jnp.dot(p_lo, overlap, preferred_element_type=F32)

    blk = lax.broadcasted_iota(jnp.int32, (tq, nb), 1)
    cur = qpos // SEL_BLOCK
    forced = (blk == 0) | (blk == cur) | (blk == cur - 1)
    score = jnp.where((blk <= cur) & (blk < n_sel), imp + jnp.where(forced, FORCE_BONUS, 0.0), NEG_BIG)
    rank = jnp.zeros((tq, nb), jnp.int32)
    for i in range(n_sel):
        si = score[:, i:i + 1]
        rank += ((si > score) | ((si == score) & (i < blk))).astype(jnp.int32)
    sel_f = ((rank < SEL_TOPK) & (score > 0.5 * NEG_BIG)).astype(F32)
    sel = sel_f.astype(BF16)

    reset()

    def sel_step(c, carry):
        start = pl.multiple_of(c * SEL_CHUNK, SEL_CHUNK)
        jrow = lax.broadcasted_iota(jnp.int32, (nb, SEL_CHUNK), 0)
        kblk = (start + lax.broadcasted_iota(jnp.int32, (nb, SEL_CHUNK), 1)) // SEL_BLOCK
        expand = (jrow == kblk).astype(BF16)
        chosen = jnp.dot(sel, expand, preferred_element_type=F32) > 0.5
        kpos = start + lax.broadcasted_iota(jnp.int32, (1, SEL_CHUNK), 1)
        update(ks_ref[pl.ds(start, SEL_CHUNK), :].astype(BF16), vs_ref[pl.ds(start, SEL_CHUNK), :].astype(BF16),
               chosen & (kpos <= qpos))
        return carry

    lax.fori_loop(0, past // SEL_CHUNK, sel_step, 0)
    new_blk = past // SEL_BLOCK
    idx = lax.broadcasted_iota(jnp.int32, (1, NEW_ROWS), 1)
    update(kn_ref[...], vn_ref[...],
           (sel_f[:, new_blk:new_blk + 1] > 0.5) & (past + idx <= qpos) & (idx < n_new))
    o_scr[...] += gate_rows(1) * finish()

    reset()
    widx = lax.broadcasted_iota(jnp.int32, (1, kw_ref.shape[0]), 1)
    kpos = pos0 - w_buf + widx
    dist = qpos - kpos
    update(kw_ref[...], vw_ref[...], (dist >= 0) & (dist < WINDOW) & (kpos >= 0) & (widx < w_buf + n_new))
    out = o_scr[...] + gate_rows(2) * finish()
    for h in range(ng):
        o_ref[:, h * HEAD_DIM:(h + 1) * HEAD_DIM] = out[h * tq:(h + 1) * tq]


def nsa_sample(q, kvc, past_sel, new_kv, win_kv, sig, pos0, n_new, w_buf):
    b, _, tq, _ = q.shape
    past = past_sel.shape[2]
    n_cp = kvc.shape[3]
    n_cmp = (past + n_new - CMP_BLOCK) // CMP_STRIDE + 1
    n_sel = -(-(past + n_new) // SEL_BLOCK)
    wp = win_kv.shape[3]
    rows = NSA_GROUP * tq
    kv5 = lambda s, n: pl.BlockSpec((None, None, None, n, HEAD_DIM), lambda i, k: (i, s, k, 0, 0))
    return pl.pallas_call(
        functools.partial(_nsa_sample_kernel, tq=tq, pos0=pos0, n_new=n_new, n_cmp=n_cmp, n_sel=n_sel,
                          past=past, w_buf=w_buf),
        grid=(b, NSA_KV_HEADS),
        in_specs=[pl.BlockSpec((None, NSA_GROUP, tq, HEAD_DIM), lambda i, k: (i, k, 0, 0)),
                  kv5(0, n_cp), kv5(1, n_cp),
                  pl.BlockSpec((None, None, past, HEAD_DIM), lambda i, k: (i, k, 0, 0)),
                  pl.BlockSpec((None, None, past, HEAD_DIM), lambda i, k: (i, NSA_KV_HEADS + k, 0, 0)),
                  kv5(0, NEW_ROWS), kv5(1, NEW_ROWS), kv5(0, wp), kv5(1, wp),
                  pl.BlockSpec((None, tq, LANES), lambda i, k: (i, 0, 0))],
        out_specs=pl.BlockSpec((None, tq, NSA_GROUP * HEAD_DIM), lambda i, k: (i, 0, k)),
        out_shape=jax.ShapeDtypeStruct((b, tq, NSA_QW), F32),
        scratch_shapes=[pltpu.VMEM((rows, LANES), F32), pltpu.VMEM((rows, 2 * HEAD_DIM), F32),
                        pltpu.VMEM((rows, HEAD_DIM), F32)],
        compiler_params=_params("parallel", "arbitrary"),
        name="nsa_sample",
    )(q, kvc, kvc, past_sel, past_sel, new_kv, new_kv, win_kv, win_kv, sig)


def sample_nsa(q_n, past_cmp, past_sel, kvb, win_all, sig, pe, w1, w2, pos0, n_new, w_buf):
    kvc = cmp_mlp_prompt(past_cmp, pe, w1.astype(BF16), w2.astype(BF16))
    new_kv = jnp.pad(kvb[:, 0:2], ((0, 0), (0, 0), (0, 0), (0, NEW_ROWS - kvb.shape[3]), (0, 0)))
    wp = -(-win_all.shape[1] // LANES) * LANES
    win_kv = jnp.pad(win_all.transpose(0, 2, 3, 1, 4).astype(BF16),
                     ((0, 0), (0, 0), (0, 0), (0, wp - win_all.shape[1]), (0, 0)))
    return nsa_sample(q_n, kvc, past_sel, new_kv, win_kv, sig, pos0, n_new, w_buf)[:, :n_new]


HYB_ORDER = (0, 1, 2, 3, 6, 7, 8, 9, 10, 11, 12, 4, 5, 13)
HYB_QKV = 0
HYB_Z_BLK = 3
HYB_NQ = 2 * GDN_QK + 2 * GDN_VW
HYB_KV = HYB_NQ + NSA_QW
HYB_SMALL = HYB_KV + 6 * NSA_KVW
GDN_BETA_LANE = GDN_HEADS
NSA_GATE_LANE = 2 * GDN_HEADS
PREP_ROWS = 256


def reorder_hyb_cols(w):
    offs = [0]
    for n in HYB_SPLITS:
        offs.append(offs[-1] + n)
    return jnp.concatenate([w[:, offs[i]:offs[i + 1]] for i in HYB_ORDER], axis=1)


def _softplus(x):
    return jnp.maximum(x, 0.0) + jnp.log(1.0 + jnp.exp(-jnp.abs(x)))


def _hyb_prep_kernel(p_ref, cw_ref, alog_ref, dtb_ref, cos_ref, sin_ref, hist_ref,
                     qa_ref, ka_ref, va_ref, g_ref, sig_ref, qn_ref, cmp_ref, sel_ref, win_ref, kvb_ref,
                     ext_scr, *, tr):
    kw = GDN_CONV
    off = 8 - (kw - 1)

    @pl.when(pl.program_id(1) == 0)
    def _():
        ext_scr[0:8, :] = hist_ref[...]

    ext_scr[8:8 + tr, :] = p_ref[:, HYB_QKV:HYB_QKV + GDN_CONV_DIM]
    outs = (qa_ref, ka_ref, va_ref)
    for seg in range(3):
        for h in range(GDN_HEADS):
            lo = seg * GDN_QK + h * GDN_DK
            acc = cw_ref[0:1, lo:lo + GDN_DK] * ext_scr[pl.ds(off, tr), lo:lo + GDN_DK]
            for j in range(1, kw):
                acc = acc + cw_ref[j:j + 1, lo:lo + GDN_DK] * ext_scr[pl.ds(off + j, tr), lo:lo + GDN_DK]
            y = _silu(acc)
            if seg < 2:
                y = y * lax.rsqrt(jnp.sum(y * y, axis=-1, keepdims=True) + 1e-6)
            if seg == 0:
                y = y * GDN_DK ** -0.5
            outs[seg][:, h * GDN_DK:(h + 1) * GDN_DK] = y
    ext_scr[0:8, :] = ext_scr[tr:tr + 8, :]

    small = p_ref[:, HYB_SMALL:HYB_SMALL + LANES]
    g_ref[...] = -jnp.exp(alog_ref[...]) * _softplus(small + dtb_ref[...])
    sig_ref[...] = jax.nn.sigmoid(small)

    cos = cos_ref[...]
    sin = sin_ref[...]

    def rope(x):
        return x * cos + pltpu.roll(x, HEAD_DIM // 2, axis=1) * sin

    for h in range(NSA_HEADS):
        x = p_ref[:, HYB_NQ + h * HEAD_DIM:HYB_NQ + (h + 1) * HEAD_DIM]
        qn_ref[h] = (rope(x) * HEAD_DIM ** -0.5).astype(BF16)
    for i, o_ref in enumerate((cmp_ref, sel_ref, win_ref)):
        for kh in range(NSA_KV_HEADS):
            k0 = HYB_KV + 2 * i * NSA_KVW + kh * HEAD_DIM
            kr = rope(p_ref[:, k0:k0 + HEAD_DIM])
            vr = p_ref[:, k0 + NSA_KVW:k0 + NSA_KVW + HEAD_DIM]
            o_ref[:, kh * HEAD_DIM:(kh + 1) * HEAD_DIM] = kr
            o_ref[:, NSA_KVW + kh * HEAD_DIM:NSA_KVW + (kh + 1) * HEAD_DIM] = vr
            if i > 0:
                kvb_ref[2 * (i - 1), kh] = kr.astype(BF16)
                kvb_ref[2 * (i - 1) + 1, kh] = vr.astype(BF16)


def hyb_prep(proj, conv_w, a_log, dt_bias, pos, hist, tr=PREP_ROWS):
    b, l, _ = proj.shape
    half = HEAD_DIM // 2
    inv_freq = 1.0 / (ROPE_THETA ** (jnp.arange(half, dtype=F32) / half))
    ang = pos.astype(F32)[:, None] * inv_freq[None, :]
    cos = jnp.concatenate([jnp.cos(ang), jnp.cos(ang)], axis=1)
    sin = jnp.concatenate([-jnp.sin(ang), jnp.sin(ang)], axis=1)
    lane_pad = lambda v: jnp.pad(v.astype(F32).reshape(1, -1), ((0, 0), (0, LANES - v.shape[0])))
    row = lambda w: pl.BlockSpec((None, tr, w), lambda i, n: (i, n, 0))
    f32 = lambda w: jax.ShapeDtypeStruct((b, l, w), F32)
    return pl.pallas_call(
        functools.partial(_hyb_prep_kernel, tr=tr),
        grid=(b, l // tr),
        in_specs=[row(proj.shape[2]),
                  pl.BlockSpec((GDN_CONV, GDN_CONV_DIM), lambda i, n: (0, 0)),
                  pl.BlockSpec((1, LANES), lambda i, n: (0, 0)),
                  pl.BlockSpec((1, LANES), lambda i, n: (0, 0)),
                  pl.BlockSpec((tr, HEAD_DIM), lambda i, n: (n, 0)),
                  pl.BlockSpec((tr, HEAD_DIM), lambda i, n: (n, 0)),
                  pl.BlockSpec((None, 8, GDN_CONV_DIM), lambda i, n: (i, 0, 0))],
        out_specs=[row(GDN_QK), row(GDN_QK), row(GDN_VW), row(LANES), row(LANES),
                   pl.BlockSpec((None, NSA_HEADS, tr, HEAD_DIM), lambda i, n: (i, 0, n, 0)),
                   row(2 * NSA_KVW), row(2 * NSA_KVW), row(2 * NSA_KVW),
                   pl.BlockSpec((None, 4, NSA_KV_HEADS, tr, HEAD_DIM), lambda i, n: (i, 0, 0, n, 0))],
        out_shape=[f32(GDN_QK), f32(GDN_QK), f32(GDN_VW), f32(LANES), f32(LANES),
                   jax.ShapeDtypeStruct((b, NSA_HEADS, l, HEAD_DIM), BF16),
                   f32(2 * NSA_KVW), f32(2 * NSA_KVW), f32(2 * NSA_KVW),
                   jax.ShapeDtypeStruct((b, 4, NSA_KV_HEADS, l, HEAD_DIM), BF16)],
        scratch_shapes=[pltpu.VMEM((tr + 8, GDN_CONV_DIM), F32)],
        compiler_params=_params("parallel", "arbitrary"),
        name="hyb_prep",
    )(proj, conv_w.astype(F32), lane_pad(a_log), lane_pad(dt_bias), cos, sin, hist)


def _dot_exact01(a01, x):
    hi = x.astype(BF16)
    r = x - hi.astype(F32)
    mid = r.astype(BF16)
    lo = (r - mid.astype(F32)).astype(BF16)
    return (jnp.dot(a01, hi, preferred_element_type=F32) + jnp.dot(a01, mid, preferred_element_type=F32)
            + jnp.dot(a01, lo, preferred_element_type=F32))


def _gdn_kernel(q_ref, k_ref, v_ref, g_ref, b_ref, z_ref, nw_ref, s0_ref, o_ref, s_ref, *, c, g_off, b_off):
    @pl.when(pl.program_id(1) == 0)
    def _():
        s_ref[...] = s0_ref[...]

    ii = lax.broadcasted_iota(jnp.int32, (c, c), 0)
    jj = lax.broadcasted_iota(jnp.int32, (c, c), 1)
    incl = ii >= jj
    strict = ii > jj
    t01 = incl.astype(BF16)
    w01 = strict.astype(F32)
    g_all = g_ref[...]
    b_all = b_ref[...]
    nw = nw_ref[...]
    hs = range(GDN_HEADS)
    sl = [slice(h * GDN_DV, (h + 1) * GDN_DV) for h in hs]
    dot = functools.partial(jnp.dot, preferred_element_type=F32)
    bf = lambda xs: [x.astype(BF16) for x in xs]
    q = [q_ref[:, x] for x in sl]
    k = [k_ref[:, x] for x in sl]
    v = [v_ref[:, x] for x in sl]
    g_b = [jnp.broadcast_to(g_all[:, g_off + h:g_off + h + 1], (c, GDN_DK)) for h in hs]
    b_b = [jnp.broadcast_to(b_all[:, b_off + h:b_off + h + 1], (c, GDN_DK)) for h in hs]
    gam = [_dot_exact01(t01, x) for x in g_b]
    decay = [jnp.exp(_dot_exact01(t01, x[:, :c] * w01)) for x in g_b]
    kb = [x * y for x, y in zip(k, b_b)]
    kbf = bf(k)
    n_mat = [jnp.where(strict, _dot_nt(x, y) * d, 0.0) for x, y, d in zip(bf(kb), kbf, decay)]
    e_gam = [jnp.exp(x) for x in gam]
    y = [jnp.concatenate([vv * bb, kk * ee], axis=1) for vv, bb, kk, ee in zip(v, b_b, kb, e_gam)]
    y = [yy - dot(nn, yb) for yy, nn, yb in zip(y, bf(n_mat), bf(y))]
    p = n_mat
    span = 2
    while span < c:
        pb = bf(p)
        p = [dot(x, x) for x in pb]
        y = [yy + dot(pp, yb) for yy, pp, yb in zip(y, bf(p), bf(y))]
        span *= 2
    s = [s_ref[h] for h in hs]
    sb = bf(s)
    v_new = [yy[:, :GDN_DV] - dot(yy[:, GDN_DV:].astype(BF16), ss) for yy, ss in zip(y, sb)]
    vnb = bf(v_new)
    attn = [jnp.where(incl, _dot_nt(x, y) * d, 0.0) for x, y, d in zip(bf(q), kbf, decay)]
    o = [dot((qq * ee).astype(BF16), ss) + dot(aa, vv)
         for qq, ee, ss, aa, vv in zip(q, e_gam, sb, bf(attn), vnb)]
    g_end = [x[c - 1:c, :] for x in gam]
    k_dec = [(kk * jnp.exp(ge - gg)).astype(BF16) for kk, ge, gg in zip(k, g_end, gam)]
    for h in hs:
        s_ref[h] = s[h] * jnp.exp(g_end[h]) + lax.dot_general(
            k_dec[h], vnb[h], (((0,), (0,)), ((), ())), preferred_element_type=F32)
        on = o[h] * lax.rsqrt(jnp.mean(o[h] * o[h], axis=-1, keepdims=True) + RMS_EPS) * nw
        o_ref[:, sl[h]] = on * _silu(z_ref[:, sl[h]])


def gdn_gated(q, k, v, g, beta, z, norm_w, s0, g_off=0, b_off=0, z_blk=0):
    b, l, _ = q.shape
    c = GDN_CHUNK
    row = lambda w: pl.BlockSpec((None, c, w), lambda i, n: (i, n, 0))
    st = pl.BlockSpec((None, GDN_HEADS, GDN_DK, GDN_DV), lambda i, n: (i, 0, 0, 0))
    return pl.pallas_call(
        functools.partial(_gdn_kernel, c=c, g_off=g_off, b_off=b_off),
        grid=(b, l // c),
        in_specs=[row(GDN_QK), row(GDN_QK), row(GDN_VW), row(g.shape[2]), row(beta.shape[2]),
                  pl.BlockSpec((None, c, GDN_VW), lambda i, n: (i, n, z_blk)),
                  pl.BlockSpec((1, GDN_DV), lambda i, n: (0, 0)), st],
        out_specs=[row(GDN_VW), st],
        out_shape=[jax.ShapeDtypeStruct((b, l, GDN_VW), F32),
                   jax.ShapeDtypeStruct((b, GDN_HEADS, GDN_DK, GDN_DV), F32)],
        compiler_params=_params("parallel", "arbitrary"),
        name="gdn_chunked",
    )(q, k, v, g, beta, z, norm_w.reshape(1, GDN_DV).astype(F32), s0.astype(F32))


def _split3(x):
    hi = x.astype(BF16)
    r = x - hi.astype(F32)
    mid = r.astype(BF16)
    lo = (r - mid.astype(F32)).astype(BF16)
    return hi, mid, lo


def _dot_exact01_right(x, b01):
    hi, mid, lo = _split3(x)
    return (jnp.dot(hi, b01, preferred_element_type=F32) + jnp.dot(mid, b01, preferred_element_type=F32)
            + jnp.dot(lo, b01, preferred_element_type=F32))


def _silu(x):
    return x * jax.nn.sigmoid(x)


def _ssd_kernel(z_ref, x_ref, b_ref, c_ref, dt_ref, cw_ref, cb_ref, an_ref, dtb_ref, dsk_ref, nw_ref,
                r128_ref, r64_ref, hist_ref, h0_ref, y_ref, h_ref, ex_scr, eb_scr, ec_scr, *, c, n_valid):
    n = pl.program_id(1)
    kw = SSM_CONV
    off = 8 - (kw - 1)
    gw = SSM_HPG * SSM_HEAD_DIM

    @pl.when(n == 0)
    def _():
        h_ref[...] = h0_ref[...]
        ex_scr[0:8, :] = hist_ref[:, 0:SSM_D_INNER]
        eb_scr[0:8, :] = hist_ref[:, SSM_D_INNER:SSM_D_INNER + SSM_BC]
        ec_scr[0:8, :] = hist_ref[:, SSM_D_INNER + SSM_BC:SSM_CONV_DIM]

    ex_scr[8:8 + c, :] = x_ref[...]
    eb_scr[8:8 + c, :] = b_ref[...]
    ec_scr[8:8 + c, :] = c_ref[...]

    def conv(scr, col0, lo, width):
        acc = cb_ref[:, col0 + lo:col0 + lo + width]
        for j in range(kw):
            acc = acc + cw_ref[j:j + 1, col0 + lo:col0 + lo + width] * scr[pl.ds(off + j, c), lo:lo + width]
        return _silu(acc)

    ii = lax.broadcasted_iota(jnp.int32, (c, c), 0)
    jj = lax.broadcasted_iota(jnp.int32, (c, c), 1)
    incl = ii >= jj
    t01 = incl.astype(BF16)
    x_dt = dt_ref[...] + dtb_ref[...]
    dtv = jnp.maximum(x_dt, 0.0) + jnp.log(1.0 + jnp.exp(-jnp.abs(x_dt)))
    if n_valid is not None:
        row = n * c + lax.broadcasted_iota(jnp.int32, dtv.shape, 0)
        dtv = jnp.where(row < n_valid, dtv, 0.0)
    gam = _dot_exact01(t01, dtv * an_ref[...])
    gam_t = gam.T
    for g in range(SSM_GROUPS):
        r64 = r64_ref[:, g * gw:(g + 1) * gw]
        r128 = r128_ref[:, g * SSM_HPG * LANES:(g + 1) * SSM_HPG * LANES]
        xg = conv(ex_scr, 0, g * gw, gw)
        bg = conv(eb_scr, SSM_D_INNER, g * SSM_STATE, SSM_STATE).astype(BF16)
        cg = conv(ec_scr, SSM_D_INNER + SSM_BC, g * SSM_STATE, SSM_STATE).astype(BF16)
        gam64 = _dot_exact01_right(gam, r64)
        gam128 = _dot_exact01_right(gam, r128)
        xdt = xg * _dot_exact01_right(dtv, r64)
        xdt_b = xdt.astype(BF16)
        cb = _dot_nt(cg, bg)
        parts = []
        for hl in range(SSM_HPG):
            h = g * SSM_HPG + hl
            dec = jnp.where(incl, jnp.exp(gam128[:, hl * LANES:(hl + 1) * LANES] - gam_t[h:h + 1, :]), 0.0)
            parts.append(jnp.dot((cb * dec).astype(BF16), xdt_b[:, hl * SSM_HEAD_DIM:(hl + 1) * SSM_HEAD_DIM],
                                 preferred_element_type=F32))
        hs = h_ref[:, g * gw:(g + 1) * gw]
        y = (jnp.concatenate(parts, axis=1)
             + jnp.dot(cg, hs.astype(BF16), preferred_element_type=F32) * jnp.exp(gam64))
        g_end = gam64[c - 1:c, :]
        xdec = (xdt * jnp.exp(g_end - gam64)).astype(BF16)
        h_ref[:, g * gw:(g + 1) * gw] = hs * jnp.exp(g_end) + lax.dot_general(
            bg, xdec, (((0,), (0,)), ((), ())), preferred_element_type=F32)
        y = (y + dsk_ref[:, g * gw:(g + 1) * gw] * xg) * _silu(z_ref[:, g * gw:(g + 1) * gw])
        y = y * lax.rsqrt(jnp.mean(y * y, axis=-1, keepdims=True) + RMS_EPS) * nw_ref[:, g * gw:(g + 1) * gw]
        y_ref[:, g * gw:(g + 1) * gw] = y.astype(y_ref.dtype)

    ex_scr[0:8, :] = ex_scr[c:c + 8, :]
    eb_scr[0:8, :] = eb_scr[c:c + 8, :]
    ec_scr[0:8, :] = ec_scr[c:c + 8, :]


def ssd_fused(proj, n_valid, conv_w, conv_b, a_log, dt_bias, d_skip, norm_w, conv_buf, h0):
    b, lp, _ = proj.shape
    c = SSM_CHUNK
    f = lambda a: a.astype(F32)
    lane_pad = lambda v: jnp.pad(f(v).reshape(1, -1), ((0, 0), (0, LANES - v.shape[0])))
    heads = jnp.arange(LANES)[:, None]
    r128 = (heads == jnp.arange(SSM_HEADS * LANES)[None, :] // LANES).astype(BF16)
    r64 = (heads == jnp.arange(SSM_D_INNER)[None, :] // SSM_HEAD_DIM).astype(BF16)
    hist = jnp.pad(f(conv_buf), ((0, 0), (8 - (SSM_CONV - 1), 0), (0, 0)))
    h0_t = f(h0).reshape(b, SSM_D_INNER, SSM_STATE).transpose(0, 2, 1)
    col = lambda w, blk: pl.BlockSpec((None, c, w), lambda i, n: (i, n, blk))
    full = lambda shape: pl.BlockSpec(shape, lambda i, n: (0,) * len(shape))
    per_seq = lambda shape: pl.BlockSpec((None,) + shape, lambda i, n: (i,) + (0,) * len(shape))
    y, h_t = pl.pallas_call(
        functools.partial(_ssd_kernel, c=c, n_valid=None if n_valid == lp else n_valid),
        grid=(b, lp // c),
        in_specs=[col(SSM_D_INNER, 0), col(SSM_D_INNER, 1),
                  col(SSM_BC, 2 * SSM_D_INNER // SSM_BC), col(SSM_BC, 2 * SSM_D_INNER // SSM_BC + 1),
                  col(LANES, (2 * SSM_D_INNER + 2 * SSM_BC) // LANES),
                  full((SSM_CONV, SSM_CONV_DIM)), full((1, SSM_CONV_DIM)), full((1, LANES)), full((1, LANES)),
                  full((1, SSM_D_INNER)), full((1, SSM_D_INNER)),
                  full((LANES, SSM_HEADS * LANES)), full((LANES, SSM_D_INNER)),
                  per_seq((8, SSM_CONV_DIM)), per_seq((SSM_STATE, SSM_D_INNER))],
        out_specs=[col(SSM_D_INNER, 0), per_seq((SSM_STATE, SSM_D_INNER))],
        out_shape=[jax.ShapeDtypeStruct((b, lp, SSM_D_INNER), BF16),
                   jax.ShapeDtypeStruct((b, SSM_STATE, SSM_D_INNER), F32)],
        scratch_shapes=[pltpu.VMEM((c + 8, SSM_D_INNER), F32), pltpu.VMEM((c + 8, SSM_BC), F32),
                        pltpu.VMEM((c + 8, SSM_BC), F32)],
        compiler_params=_params("parallel", "arbitrary"),
        name="ssd_fused",
    )(proj, proj, proj, proj, proj, f(conv_w), f(conv_b).reshape(1, -1), lane_pad(-jnp.exp(f(a_log))),
      lane_pad(dt_bias), jnp.repeat(f(d_skip), SSM_HEAD_DIM).reshape(1, -1), f(norm_w).reshape(1, -1),
      r128, r64, hist, h0_t)
    return y, h_t.transpose(0, 2, 1).reshape(b, SSM_HEADS, SSM_HEAD_DIM, SSM_STATE)


def hybrid_prompt(x, norm_w, w_in_p, gdn_conv_w, gdn_a_log, gdn_dt_bias, gdn_norm_w, cmp_pe, cmp_w1, cmp_w2, w_out_b):
    b, l, d = x.shape
    x2 = x.reshape(b * l, d)
    proj = rms_mm(x2, norm_w, w_in_p).reshape(b, l, -1)
    hist = jnp.zeros((b, 8, GDN_CONV_DIM), F32)
    q_a, k_a, v_a, g_log, sig, q_n, cmp2d, sel2d, win2d, kvb = hyb_prep(
        proj, gdn_conv_w, gdn_a_log, gdn_dt_bias, jnp.arange(l), hist)
    s0 = jnp.zeros((b, GDN_HEADS, GDN_DK, GDN_DV), F32)
    gdn_out, s_new = gdn_gated(q_a, k_a, v_a, g_log, sig, proj, gdn_norm_w, s0,
                               g_off=0, b_off=GDN_BETA_LANE, z_blk=HYB_Z_BLK)
    nsa_out = nsa_prompt(q_n, cmp2d, kvb, sig, cmp_pe, cmp_w1.astype(BF16), cmp_w2.astype(BF16))
    mix = jnp.concatenate([gdn_out, nsa_out], axis=-1).reshape(b * l, -1).astype(BF16)
    out = mm_resid(mix, w_out_b, x2).reshape(b, l, d)
    rows5 = lambda r: r.reshape(b, l, 2, NSA_KV_HEADS, HEAD_DIM)
    new_conv = proj[:, -(GDN_CONV - 1):, HYB_QKV:HYB_QKV + GDN_CONV_DIM]
    return out, rows5(cmp2d), rows5(sel2d), rows5(win2d)[:, -min(WINDOW, l):], new_conv, s_new


def hybrid_sample(x, norm_w, pos0, w_in_p, gdn_conv_w, gdn_a_log, gdn_dt_bias, gdn_norm_w, cmp_pe, cmp_w1, cmp_w2,
                  w_out_b, gdn_conv_buf, gdn_s0, past_cmp, past_sel, win_buf):
    b, l, d = x.shape
    tr = NSA_SAMPLE_TQ
    x2 = x.reshape(b * l, d)
    proj = rms_mm(x2, norm_w, w_in_p).reshape(b, l, -1)
    new_conv = jnp.concatenate([gdn_conv_buf, proj[:, :, HYB_QKV:HYB_QKV + GDN_CONV_DIM]],
                               axis=1)[:, -(GDN_CONV - 1):]
    proj = jnp.pad(proj, ((0, 0), (0, tr - l), (0, 0)))
    hist = jnp.pad(gdn_conv_buf.astype(F32), ((0, 0), (8 - (GDN_CONV - 1), 0), (0, 0)))
    q_a, k_a, v_a, g_log, sig, q_n, cmp2d, sel2d, win2d, kvb = hyb_prep(
        proj, gdn_conv_w, gdn_a_log, gdn_dt_bias, pos0 + jnp.arange(tr), hist, tr=tr)
    chunk = lambda a: jnp.pad(a[:, :l], ((0, 0), (0, GDN_CHUNK - l), (0, 0)))
    gdn_out, s_new = gdn_gated(chunk(q_a), chunk(k_a), chunk(v_a), chunk(g_log), chunk(sig), chunk(proj),
                               gdn_norm_w, gdn_s0, g_off=0, b_off=GDN_BETA_LANE, z_blk=HYB_Z_BLK)
    gdn_out = gdn_out[:, :l]
    rows5 = lambda r: r[:, :l].reshape(b, l, 2, NSA_KV_HEADS, HEAD_DIM)
    cmp_rows, sel_rows, win_rows = rows5(cmp2d), rows5(sel2d), rows5(win2d)
    w_buf = win_buf.shape[1]
    win_all = jnp.concatenate([win_buf, win_rows], axis=1)
    new_win = win_all[:, -w_buf:]
    nsa_out = sample_nsa(q_n, past_cmp, past_sel, kvb, win_all, sig, cmp_pe, cmp_w1, cmp_w2, pos0, l, w_buf)
    mix = jnp.concatenate([gdn_out, nsa_out], axis=-1).reshape(b * l, -1).astype(BF16)
    out = mm_resid(mix, w_out_b, x2).reshape(b, l, d)
    return out, cmp_rows, sel_rows, new_win, new_conv, s_new


def ssm_mixer(x, norm_w, w_in_p, conv_w, conv_b, a_log, dt_bias, d_skip, gn_w, w_out_b, conv_buf, h0):
    b, l, d = x.shape
    x2 = x.reshape(b * l, d)
    proj = rms_mm(x2, norm_w, w_in_p).reshape(b, l, -1)
    xbc = proj[:, -(SSM_CONV - 1):, SSM_D_INNER:SSM_D_INNER + SSM_CONV_DIM]
    new_conv = jnp.concatenate([conv_buf, xbc], axis=1)[:, -(SSM_CONV - 1):]
    proj = jnp.pad(proj, ((0, 0), (0, (-l) % SSM_CHUNK), (0, 0)))
    y, h_new = ssd_fused(proj, l, conv_w, conv_b, a_log, dt_bias, d_skip, gn_w, conv_buf, h0)
    y = y[:, :l].reshape(b * l, SSM_D_INNER)
    return mm_resid(y, w_out_b, x2).reshape(b, l, d), new_conv, h_new


def _pad_cols(w, mult):
    n = w.shape[-1]
    return jnp.pad(w, ((0, 0), (0, (-n) % mult)))


def kernel(x_prompt, x_sample, cache_cmp_kv, cache_sel_kv, cache_win_kv, state_gdn_conv, state_gdn, state_ssm_conv, state_ssm, page_table, hyb_norm_mix, hyb_w_in, hyb_gdn_conv_w, hyb_gdn_a_log, hyb_gdn_dt_bias, hyb_gdn_norm_w, hyb_cmp_pe, hyb_cmp_w1, hyb_cmp_w2, hyb_w_out, hyb_norm_ffn, ffn_w_gate, ffn_w_up, ffn_w_down, ssm_norm_mix, ssm_w_in, ssm_conv_w, ssm_conv_b, ssm_a_log, ssm_dt_bias, ssm_d_skip, ssm_norm_w, ssm_w_out, ssm_norm_ffn, moe_router, moe_w_gate, moe_w_up, moe_w_down, final_norm):
    hp, hs = x_prompt, x_sample
    bp, lp, d = hp.shape
    bs, ls, _ = hs.shape

    w_in_p = _pad_cols(reorder_hyb_cols(hyb_w_in[0]).astype(BF16), 512)
    w_out_b = hyb_w_out[0].astype(BF16)
    hw = (w_in_p, hyb_gdn_conv_w[0], hyb_gdn_a_log[0], hyb_gdn_dt_bias[0], hyb_gdn_norm_w[0],
          hyb_cmp_pe[0], hyb_cmp_w1[0], hyb_cmp_w2[0], w_out_b)
    hp, cmp_p, sel_p, win_p, gconv_p, gst_p = hybrid_prompt(hp, hyb_norm_mix[0], *hw)
    past_cmp = page_gather(cache_cmp_kv[0], page_table, F32)
    past_sel = page_gather(cache_sel_kv[0], page_table, BF16)
    hs, cmp_s, sel_s, win_s, gconv_s, gst_s = hybrid_sample(
        hs, hyb_norm_mix[0], PAST_LEN, *hw, state_gdn_conv[0], state_gdn[0], past_cmp, past_sel, cache_win_kv[0])

    wg, wu, wd = ffn_w_gate[0].astype(BF16), ffn_w_up[0].astype(BF16), ffn_w_down[0].astype(BF16)

    def dense_ffn(x):
        b, l, _ = x.shape
        x2 = x.reshape(b * l, d)
        return mm_resid(rms_glu(x2, hyb_norm_ffn[0], wg, wu), wd, x2).reshape(b, l, d)

    hp = dense_ffn(hp)
    hs = dense_ffn(hs)

    sw_in_p = _pad_cols(ssm_w_in[0].astype(BF16), 512)
    sw_out_b = ssm_w_out[0].astype(BF16)
    sw = (sw_in_p, ssm_conv_w[0], ssm_conv_b[0], ssm_a_log[0], ssm_dt_bias[0], ssm_d_skip[0], ssm_norm_w[0], sw_out_b)
    zero_conv = jnp.zeros((bp, SSM_CONV - 1, SSM_CONV_DIM), hp.dtype)
    zero_h = jnp.zeros((bp, SSM_HEADS, SSM_HEAD_DIM, SSM_STATE), hp.dtype)
    hp, sconv_p, sst_p = ssm_mixer(hp, ssm_norm_mix[0], *sw, zero_conv, zero_h)
    hs, sconv_s, sst_s = ssm_mixer(hs, ssm_norm_mix[0], *sw, state_ssm_conv[0], state_ssm[0])

    tok = jnp.concatenate([hp.reshape(bp * lp, d), hs.reshape(bs * ls, d)], axis=0)
    n_tok = tok.shape[0]
    tok_pad = jnp.pad(tok, ((0, (-n_tok) % MOE_TM), (0, 0)))
    y0, y1, top_w = moe_ffn(tok_pad, ssm_norm_ffn[0], moe_router[0], moe_w_gate[0], moe_w_up[0], moe_w_down[0])
    y = combine_rms(tok_pad, y0, y1, top_w, final_norm)
    y_prompt = y[:bp * lp].reshape(bp, lp, d)
    y_sample = y[bp * lp:n_tok].reshape(bs, ls, d)
    st = lambda a: a[None]
    return (y_prompt, y_sample,
            st(cmp_p), st(cmp_s), st(sel_p), st(sel_s), st(win_p), st(win_s),
            st(gconv_p), st(gconv_s), st(gst_p), st(gst_s),
            st(sconv_p), st(sconv_s), st(sst_p), st(sst_s))
```
